```python
import jax
import jax.numpy as jnp
from jax import lax
import numpy as np

D_MODEL = 2048
BATCH = 2
SEQ = 8192
DEPTH = 1

N_MEM = 256
HEAD_DIM = 64
FOX_HEADS = 12
SWA_HEADS = 12
SWA_KV_HEADS = 4
SWA_GROUP = SWA_HEADS // SWA_KV_HEADS
WINDOW = 128
XMEM_HEADS = 4
XMEM_HEAD_DIM = 128
Q_BLOCK = 128
N_EXPERTS = 32
TOP_K = 4
D_EXPERT = D_MODEL
SWIGLU_LIMIT = 7.0
SWIGLU_ALPHA = 1.702
ROPE_THETA = 10000.0
NORM_EPS = 1e-6
MOE_BLOCK = 256
POS_OFFSET_MAX = 1024
FORGET_BIAS_INIT = 3.0

FOX_W = FOX_HEADS * HEAD_DIM
SWA_QW = SWA_HEADS * HEAD_DIM
SWA_KVW = SWA_KV_HEADS * HEAD_DIM
XMEM_W = XMEM_HEADS * XMEM_HEAD_DIM
IN_SPLITS = (FOX_W, FOX_W, FOX_W, FOX_HEADS, SWA_QW, SWA_KVW, SWA_KVW, XMEM_W, D_MODEL, D_MODEL, D_MODEL)
IN_WIDTH = sum(IN_SPLITS)

kernel_name = 'hybrid_fox_swa_xmem_moe_layer'


def rms_norm(x, gain):
    x32 = x.astype(jnp.float32)
    x32 = x32 * lax.rsqrt(jnp.mean(x32 * x32, axis=-1, keepdims=True) + NORM_EPS)
    return x32.astype(x.dtype) * gain


def rope(x, positions):
    d = x.shape[-1]
    inv_freq = jnp.power(ROPE_THETA, -jnp.arange(0, d, 2, dtype=jnp.float32) / d)
    ang = positions.astype(jnp.float32)[..., None] * inv_freq
    cos = jnp.cos(ang)[:, :, None, :]
    sin = jnp.sin(ang)[:, :, None, :]
    x32 = x.astype(jnp.float32)
    x1, x2 = jnp.split(x32, 2, axis=-1)
    return jnp.concatenate([x1 * cos - x2 * sin, x2 * cos + x1 * sin], axis=-1).astype(x.dtype)


def forgetting_attention(q, k, v, log_f):
    B, S, H, d = q.shape
    nb = S // Q_BLOCK
    scale = d ** -0.5
    c = jnp.cumsum(log_f, axis=1).transpose(0, 2, 1)
    k_pos = jnp.arange(S)
    q_blocks = q.reshape(B, nb, Q_BLOCK, H, d).transpose(1, 0, 2, 3, 4)
    c_blocks = c.reshape(B, H, nb, Q_BLOCK).transpose(2, 0, 1, 3)

    def one_block(args):
        qb, cb, blk = args
        s = jnp.einsum('bqhd,bshd->bhqs', qb, k, preferred_element_type=jnp.float32) * scale
        s = s + cb[..., :, None] - c[:, :, None, :]
        q_pos = blk * Q_BLOCK + jnp.arange(Q_BLOCK)
        causal = k_pos[None, :] <= q_pos[:, None]
        s = jnp.where(causal, s, -jnp.inf)
        p = jax.nn.softmax(s, axis=-1)
        return jnp.einsum('bhqs,bshd->bqhd', p.astype(v.dtype), v)

    out = lax.map(one_block, (q_blocks, c_blocks, jnp.arange(nb)))
    return out.transpose(1, 0, 2, 3, 4).reshape(B, S, H * d)


def sliding_window_attention(q, k, v, sinks):
    B, S, Hq, d = q.shape
    nb = S // WINDOW
    qb = q.reshape(B, nb, WINDOW, SWA_KV_HEADS, SWA_GROUP, d)

    def band(t):
        tb = t.reshape(B, nb, WINDOW, SWA_KV_HEADS, d)
        prev = jnp.concatenate([jnp.zeros_like(tb[:, :1]), tb[:, :-1]], axis=1)
        return jnp.concatenate([prev, tb], axis=2)

    kb, vb = band(k), band(v)
    s = jnp.einsum('bnqkgd,bnskd->bnkgqs', qb, kb, preferred_element_type=jnp.float32) * (d ** -0.5)
    q_loc = WINDOW + jnp.arange(WINDOW)
    k_loc = jnp.arange(2 * WINDOW)
    rel = q_loc[:, None] - k_loc[None, :]
    in_window = (rel >= 0) & (rel < WINDOW)
    has_prev = (jnp.arange(nb) > 0)[:, None, None] | (k_loc >= WINDOW)[None, None, :]
    mask = in_window[None] & has_prev
    s = jnp.where(mask[None, :, None, None], s, -jnp.inf)
    sink = sinks.astype(jnp.float32).reshape(SWA_KV_HEADS, SWA_GROUP)[None, None, :, :, None, None]
    sink = jnp.broadcast_to(sink, s.shape[:-1] + (1,))
    p = jax.nn.softmax(jnp.concatenate([s, sink], axis=-1), axis=-1)[..., :-1]
    out = jnp.einsum('bnkgqs,bnskd->bnqkgd', p.astype(v.dtype), vb)
    return out.reshape(B, S, Hq * d)


def memory_attention(q, k, v):
    B, S, H, d = q.shape
    s = jnp.einsum('bshd,bmhd->bhsm', q, k, preferred_element_type=jnp.float32) * (d ** -0.5)
    p = jax.nn.softmax(s, axis=-1)
    out = jnp.einsum('bhsm,bmhd->bshd', p.astype(v.dtype), v)
    return out.reshape(B, S, H * d)


def moe_ffn(hn, w_router, b_router, w_gate_up, b_gate_up, w_down, b_down):
    B, S, D = hn.shape
    T = B * S
    A = T * TOP_K
    xt = hn.reshape(T, D)
    logits = jnp.matmul(xt, w_router, preferred_element_type=jnp.float32) + b_router.astype(jnp.float32)
    top_vals, top_idx = lax.top_k(logits, TOP_K)
    gates = jax.nn.softmax(top_vals, axis=-1)
    e_flat = top_idx.reshape(A)
    tok_flat = jnp.arange(A, dtype=jnp.int32) // TOP_K
    g_flat = gates.reshape(A)
    order = jnp.argsort(e_flat)
    e_s, tok_s, g_s = e_flat[order], tok_flat[order], g_flat[order]
    counts = jnp.bincount(e_flat, length=N_EXPERTS)
    padded = ((counts + MOE_BLOCK - 1) // MOE_BLOCK) * MOE_BLOCK
    raw_start = jnp.cumsum(counts) - counts
    pad_end = jnp.cumsum(padded)
    pad_start = pad_end - padded
    slot = pad_start[e_s] + (jnp.arange(A) - raw_start[e_s])
    P = ((A + MOE_BLOCK - 1) // MOE_BLOCK) * MOE_BLOCK + N_EXPERTS * MOE_BLOCK
    n_blk = P // MOE_BLOCK
    slot_tok = jnp.zeros((P,), jnp.int32).at[slot].set(tok_s)
    slot_gate = jnp.zeros((P,), jnp.float32).at[slot].set(g_s)
    blk_expert = jnp.clip(jnp.searchsorted(pad_end, jnp.arange(n_blk) * MOE_BLOCK, side='right'), 0, N_EXPERTS - 1)

    def expert_block(args):
        toks, e = args
        xb = xt[toks]
        hmid = xb @ w_gate_up[e] + b_gate_up[e]
        gate, up = jnp.split(hmid, 2, axis=-1)
        gate = jnp.minimum(gate, SWIGLU_LIMIT)
        up = jnp.clip(up, -SWIGLU_LIMIT, SWIGLU_LIMIT)
        glu = gate * jax.nn.sigmoid(gate * SWIGLU_ALPHA)
        return ((up + 1.0) * glu) @ w_down[e] + b_down[e]

    ys = lax.map(expert_block, (slot_tok.reshape(n_blk, MOE_BLOCK), blk_expert))
    ys = ys.reshape(P, D) * slot_gate[:, None].astype(ys.dtype)
    out = jnp.zeros((T, D), ys.dtype).at[slot_tok].add(ys)
    return out.reshape(B, S, D)


def hybrid_layer(h, mem, positions, norm_mix, w_in, b_forget, fox_q_norm, fox_k_norm,
                 swa_q_norm, swa_k_norm, swa_sinks, xmem_q_norm, xmem_k_norm, norm_mem,
                 w_mem_kv, w_up_fox, w_up_swa, w_up_xmem, w_out, norm_ffn, w_router,
                 b_router, w_gate_up, b_gate_up, w_down, b_down):
    B, S, _ = h.shape
    M = mem.shape[1]
    xn = rms_norm(h, norm_mix)
    proj = xn @ w_in
    fq, fk, fv, f_logit, sq, sk, sv, mq, g_fox, g_swa, g_xmem = jnp.split(
        proj, np.cumsum(IN_SPLITS)[:-1].tolist(), axis=-1)

    fq = rms_norm(fq.reshape(B, S, FOX_HEADS, HEAD_DIM), fox_q_norm)
    fk = rms_norm(fk.reshape(B, S, FOX_HEADS, HEAD_DIM), fox_k_norm)
    log_f = jax.nn.log_sigmoid(f_logit.astype(jnp.float32) + b_forget.astype(jnp.float32))
    o_fox = forgetting_attention(fq, fk, fv.reshape(B, S, FOX_HEADS, HEAD_DIM), log_f)

    sq = rope(rms_norm(sq.reshape(B, S, SWA_HEADS, HEAD_DIM), swa_q_norm), positions)
    sk = rope(rms_norm(sk.reshape(B, S, SWA_KV_HEADS, HEAD_DIM), swa_k_norm), positions)
    o_swa = sliding_window_attention(sq, sk, sv.reshape(B, S, SWA_KV_HEADS, HEAD_DIM), swa_sinks)

    mn = rms_norm(mem, norm_mem)
    mk, mv = jnp.split(mn @ w_mem_kv, 2, axis=-1)
    mk = rms_norm(mk.reshape(B, M, XMEM_HEADS, XMEM_HEAD_DIM), xmem_k_norm)
    mq = rms_norm(mq.reshape(B, S, XMEM_HEADS, XMEM_HEAD_DIM), xmem_q_norm)
    o_xmem = memory_attention(mq, mk, mv.reshape(B, M, XMEM_HEADS, XMEM_HEAD_DIM))

    merged = (jax.nn.sigmoid(g_fox) * (o_fox @ w_up_fox)
              + jax.nn.sigmoid(g_swa) * (o_swa @ w_up_swa)
              + jax.nn.sigmoid(g_xmem) * (o_xmem @ w_up_xmem))
    h = h + merged @ w_out

    hn = rms_norm(h, norm_ffn)
    return h + moe_ffn(hn, w_router, b_router, w_gate_up, b_gate_up, w_down, b_down)


def setup_inputs(seed: int = 0) -> dict:
    key = jax.random.key(seed)
    ks = jax.random.split(key, 26)
    f32 = jnp.float32
    L = DEPTH

    def normal(k, shape, scale):
        return jax.random.normal(k, shape, f32) * scale

    def gain(k, shape):
        return 1.0 + 0.05 * jax.random.normal(k, shape, f32)

    offset = jax.random.randint(ks[2], (BATCH, 1), 0, POS_OFFSET_MAX, dtype=jnp.int32)
    return {
        'x': normal(ks[0], (BATCH, SEQ, D_MODEL), 1.0),
        'mem': normal(ks[1], (BATCH, N_MEM, D_MODEL), 1.0),
        'positions': offset + jnp.arange(SEQ, dtype=jnp.int32)[None, :],
        'norm_mix': gain(ks[3], (L, D_MODEL)),
        'w_in': normal(ks[4], (L, D_MODEL, IN_WIDTH), D_MODEL ** -0.5),
        'b_forget': FORGET_BIAS_INIT + 0.5 * jax.random.normal(ks[5], (L, FOX_HEADS), f32),
        'fox_q_norm': gain(ks[6], (L, HEAD_DIM)),
        'fox_k_norm': gain(ks[7], (L, HEAD_DIM)),
        'swa_q_norm': gain(ks[8], (L, HEAD_DIM)),
        'swa_k_norm': gain(ks[9], (L, HEAD_DIM)),
        'swa_sinks': normal(ks[10], (L, SWA_HEADS), 1.0),
        'xmem_q_norm': gain(ks[11], (L, XMEM_HEAD_DIM)),
        'xmem_k_norm': gain(ks[12], (L, XMEM_HEAD_DIM)),
        'norm_mem': gain(ks[13], (L, D_MODEL)),
        'w_mem_kv': normal(ks[14], (L, D_MODEL, 2 * XMEM_W), D_MODEL ** -0.5),
        'w_up_fox': normal(ks[15], (L, FOX_W, D_MODEL), FOX_W ** -0.5),
        'w_up_swa': normal(ks[16], (L, SWA_QW, D_MODEL), SWA_QW ** -0.5),
        'w_up_xmem': normal(ks[17], (L, XMEM_W, D_MODEL), XMEM_W ** -0.5),
        'w_out': normal(ks[18], (L, D_MODEL, D_MODEL), D_MODEL ** -0.5),
        'norm_ffn': gain(ks[19], (L, D_MODEL)),
        'w_router': normal(ks[20], (L, D_MODEL, N_EXPERTS), D_MODEL ** -0.5),
        'b_router': normal(ks[21], (L, N_EXPERTS), 0.01),
        'w_gate_up': normal(ks[22], (L, N_EXPERTS, D_MODEL, 2 * D_EXPERT), D_MODEL ** -0.5),
        'b_gate_up': normal(ks[23], (L, N_EXPERTS, 2 * D_EXPERT), 0.01),
        'w_down': normal(ks[24], (L, N_EXPERTS, D_EXPERT, D_MODEL), D_EXPERT ** -0.5),
        'b_down': normal(ks[25], (L, N_EXPERTS, D_MODEL), 0.01),
    }


def reference(x, mem, positions, norm_mix, w_in, b_forget, fox_q_norm, fox_k_norm,
              swa_q_norm, swa_k_norm, swa_sinks, xmem_q_norm, xmem_k_norm, norm_mem,
              w_mem_kv, w_up_fox, w_up_swa, w_up_xmem, w_out, norm_ffn, w_router,
              b_router, w_gate_up, b_gate_up, w_down, b_down):
    h = x
    for layer in range(DEPTH):
        h = hybrid_layer(
            h, mem, positions, norm_mix[layer], w_in[layer], b_forget[layer],
            fox_q_norm[layer], fox_k_norm[layer], swa_q_norm[layer], swa_k_norm[layer],
            swa_sinks[layer], xmem_q_norm[layer], xmem_k_norm[layer], norm_mem[layer],
            w_mem_kv[layer], w_up_fox[layer], w_up_swa[layer], w_up_xmem[layer],
            w_out[layer], norm_ffn[layer], w_router[layer], b_router[layer],
            w_gate_up[layer], b_gate_up[layer], w_down[layer], b_down[layer])
    return h
```

```python
import functools

import numpy as np
import jax
import jax.numpy as jnp
from jax import lax
from jax.experimental import pallas as pl
from jax.experimental.pallas import tpu as pltpu

F32 = jnp.float32
BF16 = jnp.bfloat16

D_MODEL = 2048
HEAD_DIM = 64
FOX_HEADS = 12
SWA_HEADS = 12
SWA_KV_HEADS = 4
SWA_GROUP = SWA_HEADS // SWA_KV_HEADS
WINDOW = 128
XMEM_HEADS = 4
XMEM_HEAD_DIM = 128
N_EXPERTS = 32
TOP_K = 4
SWIGLU_LIMIT = 7.0
SWIGLU_ALPHA = 1.702
ROPE_THETA = 10000.0
NORM_EPS = 1e-6

FOX_W = FOX_HEADS * HEAD_DIM
SWA_QW = SWA_HEADS * HEAD_DIM
SWA_KVW = SWA_KV_HEADS * HEAD_DIM
XMEM_W = XMEM_HEADS * XMEM_HEAD_DIM
FOX_PAIRS = FOX_HEADS // 2

LANES = 128
NEG_BIG = -1e30
MIB = 1024 * 1024

PRE_TM = 512
MM_TM = 1024
MM_TN = 1024
PREP_TM = 512
FOX_TQ = 512
FOX_TK = 512
MEM_TQ = 512
MERGE_TM = 256
MOE_BM = 512
MOE_TN = 1024
DISPATCH_TB = 128
COMBINE_TB = 128


def _cparams(semantics, vmem_mib):
    return pltpu.CompilerParams(dimension_semantics=semantics, vmem_limit_bytes=vmem_mib * MIB)


def _dot(a, b):
    return jnp.dot(a, b, preferred_element_type=F32)


def _dot_nt(a, b):
    return lax.dot_general(a, b, (((1,), (1,)), ((), ())), preferred_element_type=F32)


def _split2(x):
    hi = x.astype(BF16)
    lo = (x - hi.astype(F32)).astype(BF16)
    return hi, lo


def _split3(x):
    hi = x.astype(BF16)
    r = x - hi.astype(F32)
    mid = r.astype(BF16)
    lo = (r - mid.astype(F32)).astype(BF16)
    return hi, mid, lo


def _prenorm_kernel(x_ref, g_ref, wf_hi_ref, wf_lo_ref, bf_ref, xn_ref, c_ref, carry_ref, *, blocks_per_seq):
    i = pl.program_id(0)

    @pl.when(i % blocks_per_seq == 0)
    def _():
        carry_ref[...] = jnp.zeros_like(carry_ref)

    x = x_ref[...]
    tm = x.shape[0]
    ms = jnp.mean(x * x, axis=-1, keepdims=True)
    xn = x * lax.rsqrt(ms + NORM_EPS) * g_ref[...]
    xn_hi, xn_lo = _split2(xn)
    xn_ref[...] = xn_hi
    wf_hi = wf_hi_ref[...]
    z = _dot(xn_hi, wf_hi) + _dot(xn_lo, wf_hi) + _dot(xn_hi, wf_lo_ref[...]) + bf_ref[...]
    logf = jnp.minimum(z, 0.0) - jnp.log1p(jnp.exp(-jnp.abs(z)))
    lane = lax.broadcasted_iota(jnp.int32, logf.shape, 1)
    logf = jnp.where(lane < FOX_HEADS, logf, 0.0)
    hi, mid, lo = _split3(logf)
    r = lax.broadcasted_iota(jnp.int32, (tm, tm), 0)
    cc = lax.broadcasted_iota(jnp.int32, (tm, tm), 1)
    tri = jnp.where(cc <= r, 1.0, 0.0).astype(BF16)
    c = _dot(tri, hi) + _dot(tri, mid) + _dot(tri, lo) + carry_ref[...]
    c_ref[...] = c
    carry_ref[...] = c[tm - 1:tm, :]


def _prenorm(x2, gain, wf_hi, wf_lo, bf, seq):
    T = x2.shape[0]
    tm = PRE_TM
    return pl.pallas_call(
        functools.partial(_prenorm_kernel, blocks_per_seq=seq // tm),
        grid=(T // tm,),
        in_specs=[
            pl.BlockSpec((tm, D_MODEL), lambda i: (i, 0)),
            pl.BlockSpec((1, D_MODEL), lambda i: (0, 0)),
            pl.BlockSpec((D_MODEL, LANES), lambda i: (0, 0)),
            pl.BlockSpec((D_MODEL, LANES), lambda i: (0, 0)),
            pl.BlockSpec((1, LANES), lambda i: (0, 0)),
        ],
        out_specs=[
            pl.BlockSpec((tm, D_MODEL), lambda i: (i, 0)),
            pl.BlockSpec((tm, LANES), lambda i: (i, 0)),
        ],
        out_shape=[
            jax.ShapeDtypeStruct((T, D_MODEL), BF16),
            jax.ShapeDtypeStruct((T, LANES), F32),
        ],
        scratch_shapes=[pltpu.VMEM((1, LANES), F32)],
        compiler_params=_cparams(("arbitrary",), 40),
        name="prenorm",
    )(x2, gain, wf_hi, wf_lo, bf)


def _mm_kernel(x_ref, w_ref, o_ref, *, sigmoid):
    acc = _dot(x_ref[...], w_ref[...])
    if sigmoid:
        acc = jax.nn.sigmoid(acc)
    o_ref[...] = acc.astype(o_ref.dtype)


def _matmul(x, w, out_dtype, sigmoid, name):
    M, K = x.shape
    N = w.shape[1]
    tm, tn = MM_TM, MM_TN
    return pl.pallas_call(
        functools.partial(_mm_kernel, sigmoid=sigmoid),
        grid=(M // tm, N // tn),
        in_specs=[
            pl.BlockSpec((tm, K), lambda i, j: (i, 0)),
            pl.BlockSpec((K, tn), lambda i, j: (0, j)),
        ],
        out_specs=pl.BlockSpec((tm, tn), lambda i, j: (i, j)),
        out_shape=jax.ShapeDtypeStruct((M, N), out_dtype),
        compiler_params=_cparams(("parallel", "parallel"), 48),
        name=name,
    )(x, w)


def _foxprep_kernel(q_ref, k_ref, v_ref, c_ref, gq_ref, gk_ref, bd_ref, selq_ref, selk_ref,
                    onesq_ref, onesk_ref, qa_ref, ka_ref, vb_ref):
    bd = bd_ref[...]

    def headnorm(x, g):
        hi, lo = _split2(x * x)
        ssq = _dot(hi, bd) + _dot(lo, bd)
        return x * lax.rsqrt(ssq * (1.0 / HEAD_DIM) + NORM_EPS) * g

    qn = headnorm(q_ref[...], gq_ref[...]) * (HEAD_DIM ** -0.5)
    kn = headnorm(k_ref[...], gk_ref[...])
    vb_ref[...] = v_ref[...].astype(BF16)
    hi, mid, lo = _split3(c_ref[...])
    c3 = jnp.concatenate([hi, mid, lo], axis=1)
    lane = lax.broadcasted_iota(jnp.int32, (qn.shape[0], LANES), 1)
    for p in range(FOX_PAIRS):
        qp = qn[:, p * LANES:(p + 1) * LANES]
        for j in range(2):
            h = 2 * p + j
            keep = (lane < HEAD_DIM) if j == 0 else (lane >= HEAD_DIM)
            qa_ref[h, :, 0:LANES] = jnp.where(keep, qp, 0.0).astype(BF16)
            qa_ref[h, :, LANES:2 * LANES] = (_dot(c3, selq_ref[h]) + onesq_ref[h]).astype(BF16)
        ka_ref[p, :, 0:LANES] = kn[:, p * LANES:(p + 1) * LANES].astype(BF16)
        ka_ref[p, :, LANES:2 * LANES] = (_dot(c3, selk_ref[p]) + onesk_ref[...]).astype(BF16)


def _fox_tables():
    selq = np.zeros((FOX_HEADS, 3 * LANES, LANES), np.float32)
    selk = np.zeros((FOX_PAIRS, 3 * LANES, LANES), np.float32)
    onesq = np.zeros((FOX_HEADS, 1, LANES), np.float32)
    onesk = np.zeros((1, LANES), np.float32)
    for h in range(FOX_HEADS):
        p, j = divmod(h, 2)
        for piece in range(3):
            selq[h, piece * LANES + h, 6 * j + piece] = 1.0
            selk[p, piece * LANES + h, 6 * j + 3 + piece] = -1.0
            onesq[h, 0, 6 * j + 3 + piece] = 1.0
            onesk[0, 6 * j + piece] = 1.0
    bd = np.kron(np.eye(FOX_HEADS, dtype=np.float32), np.ones((HEAD_DIM, HEAD_DIM), np.float32))
    return (jnp.asarray(bd, BF16), jnp.asarray(selq, BF16), jnp.asarray(selk, BF16),
            jnp.asarray(onesq, F32), jnp.asarray(onesk, F32))


def _fox_prep(proj, c, gq, gk):
    T = proj.shape[0]
    tm = PREP_TM
    bd, selq, selk, onesq, onesk = _fox_tables()
    const2 = lambda i: (0, 0)
    const3 = lambda i: (0, 0, 0)
    return pl.pallas_call(
        _foxprep_kernel,
        grid=(T // tm,),
        in_specs=[
            pl.BlockSpec((tm, FOX_W), lambda i: (i, 0)),
            pl.BlockSpec((tm, FOX_W), lambda i: (i, 1)),
            pl.BlockSpec((tm, FOX_W), lambda i: (i, 2)),
            pl.BlockSpec((tm, LANES), lambda i: (i, 0)),
            pl.BlockSpec((1, FOX_W), const2),
            pl.BlockSpec((1, FOX_W), const2),
            pl.BlockSpec((FOX_W, FOX_W), const2),
            pl.BlockSpec((FOX_HEADS, 3 * LANES, LANES), const3),
            pl.BlockSpec((FOX_PAIRS, 3 * LANES, LANES), const3),
            pl.BlockSpec((FOX_HEADS, 1, LANES), const3),
            pl.BlockSpec((1, LANES), const2),
        ],
        out_specs=[
            pl.BlockSpec((FOX_HEADS, tm, 2 * LANES), lambda i: (0, i, 0)),
            pl.BlockSpec((FOX_PAIRS, tm, 2 * LANES), lambda i: (0, i, 0)),
            pl.BlockSpec((tm, FOX_W), lambda i: (i, 0)),
        ],
        out_shape=[
            jax.ShapeDtypeStruct((FOX_HEADS, T, 2 * LANES), BF16),
            jax.ShapeDtypeStruct((FOX_PAIRS, T, 2 * LANES), BF16),
            jax.ShapeDtypeStruct((T, FOX_W), BF16),
        ],
        compiler_params=_cparams(("parallel",), 48),
        name="fox_prep",
    )(proj, proj, proj, c, gq, gk, bd, selq, selk, onesq, onesk)


def _fox_attn_kernel(q_ref, k_ref, v_ref, o_ref, m_ref, l_ref, acc_ref):
    qi = pl.program_id(2)
    tq = q_ref.shape[1]
    tk = FOX_TK
    m_ref[...] = jnp.full(m_ref.shape, NEG_BIG, F32)
    l_ref[...] = jnp.zeros(l_ref.shape, F32)
    acc_ref[...] = jnp.zeros(acc_ref.shape, F32)

    def step(kb, masked):
        off = pl.multiple_of(kb * tk, tk)
        k = k_ref[pl.ds(off, tk), :]
        v = v_ref[pl.ds(off, tk), :]
        for j in range(2):
            s = _dot_nt(q_ref[j], k)
            if masked:
                row = lax.broadcasted_iota(jnp.int32, s.shape, 0)
                col = lax.broadcasted_iota(jnp.int32, s.shape, 1)
                s = jnp.where(row >= col, s, NEG_BIG)
            m_old = m_ref[j]
            m_new = jnp.maximum(m_old, jnp.max(s, axis=-1, keepdims=True))
            alpha = jnp.exp(m_old - m_new)
            p = jnp.exp(s - m_new)
            l_ref[j] = alpha * l_ref[j] + jnp.sum(p, axis=-1, keepdims=True)
            acc_ref[j] = alpha * acc_ref[j] + _dot(p.astype(BF16), v)
            m_ref[j] = m_new

    def body(kb, carry):
        step(kb, False)
        return carry

    lax.fori_loop(0, qi, body, 0)
    step(qi, True)
    lane = lax.broadcasted_iota(jnp.int32, (tq, LANES), 1)
    o0 = acc_ref[0] / l_ref[0]
    o1 = acc_ref[1] / l_ref[1]
    o_ref[...] = jnp.where(lane < HEAD_DIM, o0, o1).astype(o_ref.dtype)


def _fox_attn(qa, ka, vb, batch, seq):
    T = vb.shape[0]
    tq = FOX_TQ
    assert FOX_TQ == FOX_TK
    nq = seq // tq
    qa4 = qa.reshape(FOX_PAIRS, 2, T, 2 * LANES)
    return pl.pallas_call(
        _fox_attn_kernel,
        grid=(batch, FOX_PAIRS, nq),
        in_specs=[
            pl.BlockSpec((None, 2, tq, 2 * LANES), lambda b, p, q: (p, 0, b * nq + q, 0)),
            pl.BlockSpec((None, seq, 2 * LANES), lambda b, p, q: (p, b, 0)),
            pl.BlockSpec((seq, LANES), lambda b, p, q: (b, p)),
        ],
        out_specs=pl.BlockSpec((tq, LANES), lambda b, p, q: (b * nq + q, p)),
        out_shape=jax.ShapeDtypeStruct((T, FOX_W), BF16),
        scratch_shapes=[
            pltpu.VMEM((2, tq, 1), F32),
            pltpu.VMEM((2, tq, 1), F32),
            pltpu.VMEM((2, tq, LANES), F32),
        ],
        compiler_params=_cparams(("parallel", "parallel", "arbitrary"), 48),
        name="fox_attn",
    )(qa4, ka, vb)


def _swa_kernel(sinks_ref, q_ref, k_ref, v_ref, pos_ref, freq_ref, sign_ref, gq_ref, gk_ref,
                o_ref, kprev_ref, vprev_ref):
    n = pl.program_id(1)
    W = WINDOW

    @pl.when(n == 0)
    def _():
        kprev_ref[...] = jnp.zeros_like(kprev_ref)
        vprev_ref[...] = jnp.zeros_like(vprev_ref)

    lane = lax.broadcasted_iota(jnp.int32, (W, LANES), 1)
    low = lane < HEAD_DIM
    ang = pos_ref[...].astype(F32) * freq_ref[...]
    cos_t = jnp.where(low, jnp.cos(ang), 0.0)
    sin_t = jnp.sin(ang) * sign_ref[...]

    def head_slice(x, h):
        blk = x[:, (h // 2) * LANES:(h // 2 + 1) * LANES]
        if h % 2 == 1:
            blk = pltpu.roll(blk, HEAD_DIM, axis=1)
        return jnp.where(low, blk, 0.0)

    def norm_rope(x, g):
        ms = jnp.sum(x * x, axis=-1, keepdims=True) * (1.0 / HEAD_DIM)
        xn = x * lax.rsqrt(ms + NORM_EPS) * g
        half = HEAD_DIM // 2
        swapped = jnp.where(lane < half, pltpu.roll(xn, LANES - half, axis=1), pltpu.roll(xn, half, axis=1))
        return xn * cos_t + swapped * sin_t

    row = lax.broadcasted_iota(jnp.int32, (W, W), 0)
    col = lax.broadcasted_iota(jnp.int32, (W, W), 1)
    cur_ok = col <= row
    prev_ok = (col > row) & (n > 0)

    q_all = q_ref[...]
    k_all = k_ref[...]
    v_all = v_ref[...]
    outs = []
    for g in range(SWA_KV_HEADS):
        k_cur = norm_rope(head_slice(k_all, g), gk_ref[...]).astype(BF16)
        v_cur = head_slice(v_all, g).astype(BF16)
        k_prev = kprev_ref[g]
        v_prev = vprev_ref[g]
        for r in range(SWA_GROUP):
            h = g * SWA_GROUP + r
            qh = (norm_rope(head_slice(q_all, h), gq_ref[...]) * (HEAD_DIM ** -0.5)).astype(BF16)
            s_cur = jnp.where(cur_ok, _dot_nt(qh, k_cur), NEG_BIG)
            s_prev = jnp.where(prev_ok, _dot_nt(qh, k_prev), NEG_BIG)
            sink = sinks_ref[h]
            m = jnp.maximum(jnp.maximum(jnp.max(s_cur, axis=-1, keepdims=True),
                                        jnp.max(s_prev, axis=-1, keepdims=True)), sink)
            p_cur = jnp.exp(s_cur - m)
            p_prev = jnp.exp(s_prev - m)
            denom = (jnp.sum(p_cur, axis=-1, keepdims=True) + jnp.sum(p_prev, axis=-1, keepdims=True)
                     + jnp.exp(sink - m))
            o = (_dot(p_cur.astype(BF16), v_cur) + _dot(p_prev.astype(BF16), v_prev)) / denom
            outs.append(o)
        kprev_ref[g] = k_cur
        vprev_ref[g] = v_cur
    for pr in range(SWA_HEADS // 2):
        o_ref[:, pr * LANES:(pr + 1) * LANES] = (
            outs[2 * pr] + pltpu.roll(outs[2 * pr + 1], HEAD_DIM, axis=1)).astype(o_ref.dtype)


def _swa_attn(proj, pos2, sinks, gq, gk, batch, seq):
    T = proj.shape[0]
    W = WINDOW
    nb = seq // W
    half = HEAD_DIM // 2
    inv_freq = np.power(ROPE_THETA, -np.arange(0, HEAD_DIM, 2, dtype=np.float32) / HEAD_DIM).astype(np.float32)
    freq = np.zeros((1, LANES), np.float32)
    freq[0, :half] = inv_freq
    freq[0, half:HEAD_DIM] = inv_freq
    sign = np.zeros((1, LANES), np.float32)
    sign[0, :half] = -1.0
    sign[0, half:HEAD_DIM] = 1.0
    q_col = (3 * FOX_W) // SWA_QW
    k_col = (3 * FOX_W + SWA_QW) // SWA_KVW
    const2 = lambda b, n, s: (0, 0)
    grid_spec = pltpu.PrefetchScalarGridSpec(
        num_scalar_prefetch=1,
        grid=(batch, nb),
        in_specs=[
            pl.BlockSpec((W, SWA_QW), lambda b, n, s: (b * nb + n, q_col)),
            pl.BlockSpec((W, SWA_KVW), lambda b, n, s: (b * nb + n, k_col)),
            pl.BlockSpec((W, SWA_KVW), lambda b, n, s: (b * nb + n, k_col + 1)),
            pl.BlockSpec((W, 1), lambda b, n, s: (b * nb + n, 0)),
            pl.BlockSpec((1, LANES), const2),
            pl.BlockSpec((1, LANES), const2),
            pl.BlockSpec((1, LANES), const2),
            pl.BlockSpec((1, LANES), const2),
        ],
        out_specs=pl.BlockSpec((W, SWA_QW), lambda b, n, s: (b * nb + n, 0)),
        scratch_shapes=[
            pltpu.VMEM((SWA_KV_HEADS, W, LANES), BF16),
            pltpu.VMEM((SWA_KV_HEADS, W, LANES), BF16),
        ],
    )
    return pl.pallas_call(
        _swa_kernel,
        grid_spec=grid_spec,
        out_shape=jax.ShapeDtypeStruct((T, SWA_QW), BF16),
        compiler_params=_cparams(("parallel", "arbitrary"), 32),
        name="swa_attn",
    )(sinks, proj, proj, proj, pos2, jnp.asarray(freq), jnp.asarray(sign), gq, gk)


def _memkv_kernel(mem_ref, g_ref, w_ref, gk_ref, k_ref, v_ref):
    x = mem_ref[...]
    ms = jnp.mean(x * x, axis=-1, keepdims=True)
    mn = (x * lax.rsqrt(ms + NORM_EPS) * g_ref[...]).astype(BF16)
    kv = _dot(mn, w_ref[...])
    for h in range(XMEM_HEADS):
        kh = kv[:, h * LANES:(h + 1) * LANES]
        ms_h = jnp.mean(kh * kh, axis=-1, keepdims=True)
        k_ref[:, h * LANES:(h + 1) * LANES] = (kh * lax.rsqrt(ms_h + NORM_EPS) * gk_ref[...]).astype(BF16)
    v_ref[...] = kv[:, XMEM_W:].astype(BF16)


def _mem_kv(mem2, gain, w, gk):
    R = mem2.shape[0]
    tm = 256
    return pl.pallas_call(
        _memkv_kernel,
        grid=(R // tm,),
        in_specs=[
            pl.BlockSpec((tm, D_MODEL), lambda i: (i, 0)),
            pl.BlockSpec((1, D_MODEL), lambda i: (0, 0)),
            pl.BlockSpec((D_MODEL, 2 * XMEM_W), lambda i: (0, 0)),
            pl.BlockSpec((1, XMEM_HEAD_DIM), lambda i: (0, 0)),
        ],
        out_specs=[
            pl.BlockSpec((tm, XMEM_W), lambda i: (i, 0)),
            pl.BlockSpec((tm, XMEM_W), lambda i: (i, 0)),
        ],
        out_shape=[
            jax.ShapeDtypeStruct((R, XMEM_W), BF16),
            jax.ShapeDtypeStruct((R, XMEM_W), BF16),
        ],
        compiler_params=_cparams(("parallel",), 32),
        name="mem_kv",
    )(mem2, gain, w, gk)


def _memattn_kernel(q_ref, k_ref, v_ref, gq_ref, o_ref):
    q = q_ref[...]
    for h in range(XMEM_HEADS):
        sl = slice(h * LANES, (h + 1) * LANES)
        qh = q[:, sl]
        ms = jnp.mean(qh * qh, axis=-1, keepdims=True)
        qn = (qh * lax.rsqrt(ms + NORM_EPS) * gq_ref[...] * (XMEM_HEAD_DIM ** -0.5)).astype(BF16)
        s = _dot_nt(qn, k_ref[:, sl])
        m = jnp.max(s, axis=-1, keepdims=True)
        p = jnp.exp(s - m)
        l = jnp.sum(p, axis=-1, keepdims=True)
        o_ref[:, sl] = (_dot(p.astype(BF16), v_ref[:, sl]) / l).astype(o_ref.dtype)


def _mem_attn(proj, mk, mv, gq, seq, n_mem):
    T = proj.shape[0]
    tq = MEM_TQ
    per_seq = seq // tq
    q_col = (3 * FOX_W + SWA_QW + 2 * SWA_KVW) // XMEM_W
    return pl.pallas_call(
        _memattn_kernel,
        grid=(T // tq,),
        in_specs=[
            pl.BlockSpec((tq, XMEM_W), lambda i: (i, q_col)),
            pl.BlockSpec((n_mem, XMEM_W), lambda i: (i // per_seq, 0)),
            pl.BlockSpec((n_mem, XMEM_W), lambda i: (i // per_seq, 0)),
            pl.BlockSpec((1, XMEM_HEAD_DIM), lambda i: (0, 0)),
        ],
        out_specs=pl.BlockSpec((tq, XMEM_W), lambda i: (i, 0)),
        out_shape=jax.ShapeDtypeStruct((T, XMEM_W), BF16),
        compiler_params=_cparams(("parallel",), 32),
        name="mem_attn",
    )(proj, mk, mv, gq)


def _merge_kernel(of_ref, os_ref, ox_ref, g0_ref, g1_ref, g2_ref, x_ref, wf_ref, ws_ref, wx_ref, wo_ref,
                  gn_ref, wr_hi_ref, wr_lo_ref, br_ref,
                  h_ref, hn_ref, ri_ref, rg_ref, cnt_ref, carry_ref):
    i = pl.program_id(0)

    @pl.when(i == 0)
    def _():
        carry_ref[...] = jnp.zeros_like(carry_ref)

    merged = (g0_ref[...].astype(F32) * _dot(of_ref[...], wf_ref[...])
              + g1_ref[...].astype(F32) * _dot(os_ref[...], ws_ref[...])
              + g2_ref[...].astype(F32) * _dot(ox_ref[...], wx_ref[...]))
    h = x_ref[...] + _dot(merged.astype(BF16), wo_ref[...])
    h_ref[...] = h
    ms = jnp.mean(h * h, axis=-1, keepdims=True)
    hn = h * lax.rsqrt(ms + NORM_EPS) * gn_ref[...]
    hn_hi, hn_lo = _split2(hn)
    hn_ref[...] = hn_hi
    wr_hi = wr_hi_ref[...]
    logits = _dot(hn_hi, wr_hi) + _dot(hn_lo, wr_hi) + _dot(hn_hi, wr_lo_ref[...]) + br_ref[...]
    tm = logits.shape[0]
    lane = lax.broadcasted_iota(jnp.int32, (tm, LANES), 1).astype(F32)
    work = jnp.where(lane < N_EXPERTS, logits, NEG_BIG)
    vals, idxs = [], []
    for _ in range(TOP_K):
        mx = jnp.max(work, axis=-1, keepdims=True)
        ix = jnp.min(jnp.where(work == mx, lane, float(LANES)), axis=-1, keepdims=True)
        vals.append(mx)
        idxs.append(ix)
        work = jnp.where(lane == ix, NEG_BIG, work)
    es = [jnp.exp(v - vals[0]) for v in vals]
    den = es[0] + es[1] + es[2] + es[3]
    onehot = jnp.zeros((tm, LANES), F32)
    for ix in idxs:
        onehot = onehot + jnp.where(lane == ix, 1.0, 0.0)
    r = lax.broadcasted_iota(jnp.int32, (tm, tm), 0)
    cc = lax.broadcasted_iota(jnp.int32, (tm, tm), 1)
    tri = jnp.where(cc < r, 1.0, 0.0).astype(BF16)
    before = _dot(tri, onehot.astype(BF16)) + carry_ref[...]
    ri = jnp.zeros((tm, LANES), jnp.int32)
    rg = jnp.zeros((tm, LANES), F32)
    for k in range(TOP_K):
        rank = jnp.sum(jnp.where(lane == idxs[k], before, 0.0), axis=-1, keepdims=True)
        ri = jnp.where(lane == k, idxs[k].astype(jnp.int32), ri)
        ri = jnp.where(lane == TOP_K + k, rank.astype(jnp.int32), ri)
        rg = jnp.where(lane == k, es[k] / den, rg)
    ri_ref[...] = ri
    rg_ref[...] = rg
    total = carry_ref[...] + jnp.sum(onehot, axis=0, keepdims=True)
    carry_ref[...] = total
    cnt_ref[...] = total


def _merge(o_fox, o_swa, o_x, gates, x2, wf, ws, wx, wo, gn, wr_hi, wr_lo, br):
    T = x2.shape[0]
    tm = MERGE_TM
    row = lambda i: (i, 0)
    const = lambda i: (0, 0)
    resident = functools.partial(pl.BlockSpec, index_map=const, pipeline_mode=pl.Buffered(1))
    return pl.pallas_call(
        _merge_kernel,
        grid=(T // tm,),
        in_specs=[
            pl.BlockSpec((tm, FOX_W), row),
            pl.BlockSpec((tm, SWA_QW), row),
            pl.BlockSpec((tm, XMEM_W), row),
            pl.BlockSpec((tm, D_MODEL), lambda i: (i, 0)),
            pl.BlockSpec((tm, D_MODEL), lambda i: (i, 1)),
            pl.BlockSpec((tm, D_MODEL), lambda i: (i, 2)),
            pl.BlockSpec((tm, D_MODEL), row),
            resident((FOX_W, D_MODEL)),
            resident((SWA_QW, D_MODEL)),
            resident((XMEM_W, D_MODEL)),
            resident((D_MODEL, D_MODEL)),
            resident((1, D_MODEL)),
            resident((D_MODEL, LANES)),
            resident((D_MODEL, LANES)),
            resident((1, LANES)),
        ],
        out_specs=[
            pl.BlockSpec((tm, D_MODEL), row),
            pl.BlockSpec((tm, D_MODEL), row),
            pl.BlockSpec((tm, LANES), row),
            pl.BlockSpec((tm, LANES), row),
            pl.BlockSpec((1, LANES), const),
        ],
        out_shape=[
            jax.ShapeDtypeStruct((T, D_MODEL), F32),
            jax.ShapeDtypeStruct((T, D_MODEL), BF16),
            jax.ShapeDtypeStruct((T, LANES), jnp.int32),
            jax.ShapeDtypeStruct((T, LANES), F32),
            jax.ShapeDtypeStruct((1, LANES), F32),
        ],
        scratch_shapes=[pltpu.VMEM((1, LANES), F32)],
        compiler_params=_cparams(("arbitrary",), 56),
        name="merge_router",
    )(o_fox, o_swa, o_x, gates, gates, gates, x2, wf, ws, wx, wo, gn, wr_hi, wr_lo, br)


def _dispatch_kernel(start_ref, e_ref, r_ref, hn_ref, xs_in_ref, xs_ref, sem):
    del xs_in_ref
    i = pl.program_id(0)
    tb = DISPATCH_TB

    def row_copy(t, a):
        slot = start_ref[e_ref[a]] + r_ref[a]
        return pltpu.make_async_copy(hn_ref.at[i * tb + t], xs_ref.at[slot], sem)

    def issue(t, carry):
        for k in range(TOP_K):
            row_copy(t, t * TOP_K + k).start()
        return carry

    def drain(t, carry):
        for k in range(TOP_K):
            row_copy(t, t * TOP_K + k).wait()
        return carry

    lax.fori_loop(0, tb, issue, 0)
    lax.fori_loop(0, tb, drain, 0)


def _dispatch(pad_start, e_flat, r_flat, hn3, xs_init):
    T = hn3.shape[0]
    tb = DISPATCH_TB
    grid_spec = pltpu.PrefetchScalarGridSpec(
        num_scalar_prefetch=1,
        grid=(T // tb,),
        in_specs=[
            pl.BlockSpec((tb * TOP_K,), lambda i, s: (i,), memory_space=pltpu.SMEM),
            pl.BlockSpec((tb * TOP_K,), lambda i, s: (i,), memory_space=pltpu.SMEM),
            pl.BlockSpec(memory_space=pl.ANY),
            pl.BlockSpec(memory_space=pl.ANY),
        ],
        out_specs=pl.BlockSpec(memory_space=pl.ANY),
        scratch_shapes=[pltpu.SemaphoreType.DMA(())],
    )
    return pl.pallas_call(
        _dispatch_kernel,
        grid_spec=grid_spec,
        out_shape=jax.ShapeDtypeStruct(xs_init.shape, xs_init.dtype),
        input_output_aliases={4: 0},
        compiler_params=_cparams(("arbitrary",), 16),
        name="dispatch",
    )(pad_start, e_flat, r_flat, hn3, xs_init)


def _expert_changed(be_ref, i):
    return (i == 0) | (be_ref[i] != be_ref[jnp.maximum(i - 1, 0)])


def _gateup_kernel(be_ref, nu_ref, xs_ref, wg_ref, wu_ref, bg_ref, bu_ref, o_ref, wg_s, wu_s):
    i = pl.program_id(1)

    @pl.when(i < nu_ref[0])
    def _():
        @pl.when(_expert_changed(be_ref, i))
        def _():
            wg_s[...] = wg_ref[0].astype(BF16)
            wu_s[...] = wu_ref[0].astype(BF16)

        x = xs_ref[...]
        gate = _dot(x, wg_s[...]) + bg_ref[0]
        up = _dot(x, wu_s[...]) + bu_ref[0]
        gate = jnp.minimum(gate, SWIGLU_LIMIT)
        up = jnp.clip(up, -SWIGLU_LIMIT, SWIGLU_LIMIT)
        glu = gate * jax.nn.sigmoid(gate * SWIGLU_ALPHA)
        o_ref[...] = ((up + 1.0) * glu).astype(o_ref.dtype)

    @pl.when(i >= nu_ref[0])
    def _():
        o_ref[...] = jnp.zeros_like(o_ref)


def _down_kernel(be_ref, nu_ref, h_ref, w_ref, b_ref, o_ref, w_s):
    i = pl.program_id(1)

    @pl.when(i < nu_ref[0])
    def _():
        @pl.when(_expert_changed(be_ref, i))
        def _():
            w_s[...] = w_ref[0].astype(BF16)

        o_ref[...] = _dot(h_ref[...], w_s[...]) + b_ref[0]

    @pl.when(i >= nu_ref[0])
    def _():
        o_ref[...] = jnp.zeros_like(o_ref)


def _experts(blk_expert, n_used, xs, w_gate_up, b_gate_up, w_down, b_down):
    P = xs.shape[0]
    bm, tn = MOE_BM, MOE_TN
    n_blk = P // bm
    d_exp = w_down.shape[1]
    up_off = d_exp // tn

    def blk(i, nu):
        return jnp.minimum(i, nu[0] - 1)

    gateup_spec = pltpu.PrefetchScalarGridSpec(
        num_scalar_prefetch=2,
        grid=(d_exp // tn, n_blk),
        in_specs=[
            pl.BlockSpec((bm, D_MODEL), lambda j, i, be, nu: (blk(i, nu), 0)),
            pl.BlockSpec((1, D_MODEL, tn), lambda j, i, be, nu: (be[blk(i, nu)], 0, j)),
            pl.BlockSpec((1, D_MODEL, tn), lambda j, i, be, nu: (be[blk(i, nu)], 0, up_off + j)),
            pl.BlockSpec((1, 1, tn), lambda j, i, be, nu: (be[blk(i, nu)], 0, j)),
            pl.BlockSpec((1, 1, tn), lambda j, i, be, nu: (be[blk(i, nu)], 0, up_off + j)),
        ],
        out_specs=pl.BlockSpec((bm, tn), lambda j, i, be, nu: (i, j)),
        scratch_shapes=[pltpu.VMEM((D_MODEL, tn), BF16), pltpu.VMEM((D_MODEL, tn), BF16)],
    )
    hmid = pl.pallas_call(
        _gateup_kernel,
        grid_spec=gateup_spec,
        out_shape=jax.ShapeDtypeStruct((P, d_exp), BF16),
        compiler_params=_cparams(("arbitrary", "arbitrary"), 56),
        name="expert_gate_up",
    )(blk_expert, n_used, xs, w_gate_up, w_gate_up, b_gate_up, b_gate_up)

    down_spec = pltpu.PrefetchScalarGridSpec(
        num_scalar_prefetch=2,
        grid=(D_MODEL // tn, n_blk),
        in_specs=[
            pl.BlockSpec((bm, d_exp), lambda j, i, be, nu: (blk(i, nu), 0)),
            pl.BlockSpec((1, d_exp, tn), lambda j, i, be, nu: (be[blk(i, nu)], 0, j)),
            pl.BlockSpec((1, 1, tn), lambda j, i, be, nu: (be[blk(i, nu)], 0, j)),
        ],
        out_specs=pl.BlockSpec((bm, tn), lambda j, i, be, nu: (i, j)),
        scratch_shapes=[pltpu.VMEM((d_exp, tn), BF16)],
    )
    return pl.pallas_call(
        _down_kernel,
        grid_spec=down_spec,
        out_shape=jax.ShapeDtypeStruct((P, D_MODEL), F32),
        compiler_params=_cparams(("arbitrary", "arbitrary"), 48),
        name="expert_down",
    )(blk_expert, n_used, hmid, w_down, b_down)


def _combine_kernel(start_ref, e_ref, r_ref, h_ref, g_ref, ys_ref, o_ref, buf, sem):
    tb = COMBINE_TB

    def row_copy(t, k):
        a = t * TOP_K + k
        slot = start_ref[e_ref[a]] + r_ref[a]
        return pltpu.make_async_copy(ys_ref.at[pl.ds(slot, 1), :], buf.at[k, pl.ds(t, 1), :], sem)

    def issue(t, carry):
        for k in range(TOP_K):
            row_copy(t, k).start()
        return carry

    def drain(t, carry):
        for k in range(TOP_K):
            row_copy(t, k).wait()
        return carry

    lax.fori_loop(0, tb, issue, 0)
    lax.fori_loop(0, tb, drain, 0)
    g = g_ref[...]
    acc = h_ref[...]
    for k in range(TOP_K):
        acc = acc + g[:, k:k + 1] * buf[k]
    o_ref[...] = acc


def _combine(pad_start, e_flat, r_flat, h1, gates, ys):
    T = h1.shape[0]
    tb = COMBINE_TB
    grid_spec = pltpu.PrefetchScalarGridSpec(
        num_scalar_prefetch=1,
        grid=(T // tb,),
        in_specs=[
            pl.BlockSpec((tb * TOP_K,), lambda i, s: (i,), memory_space=pltpu.SMEM),
            pl.BlockSpec((tb * TOP_K,), lambda i, s: (i,), memory_space=pltpu.SMEM),
            pl.BlockSpec((tb, D_MODEL), lambda i, s: (i, 0)),
            pl.BlockSpec((tb, LANES), lambda i, s: (i, 0)),
            pl.BlockSpec(memory_space=pl.ANY),
        ],
        out_specs=pl.BlockSpec((tb, D_MODEL), lambda i, s: (i, 0)),
        scratch_shapes=[pltpu.VMEM((TOP_K, tb, D_MODEL), F32), pltpu.SemaphoreType.DMA(())],
    )
    return pl.pallas_call(
        _combine_kernel,
        grid_spec=grid_spec,
        out_shape=jax.ShapeDtypeStruct((T, D_MODEL), F32),
        compiler_params=_cparams(("arbitrary",), 32),
        name="combine",
    )(pad_start, e_flat, r_flat, h1, gates, ys)


def _pad_lanes(a, width=LANES):
    return jnp.pad(a, ((0, 0), (0, width - a.shape[1])))


def _layer(h, mem, positions, norm_mix, w_in, b_forget, fox_q_norm, fox_k_norm, swa_q_norm, swa_k_norm,
           swa_sinks, xmem_q_norm, xmem_k_norm, norm_mem, w_mem_kv, w_up_fox, w_up_swa, w_up_xmem, w_out,
           norm_ffn, w_router, b_router, w_gate_up, b_gate_up, w_down, b_down):
    B, S, D = h.shape
    M = mem.shape[1]
    T = B * S
    x2 = h.reshape(T, D)

    attn_end = 3 * FOX_W + FOX_HEADS
    gate_start = attn_end + SWA_QW + 2 * SWA_KVW + XMEM_W
    w_attn = jnp.concatenate([w_in[:, :3 * FOX_W], w_in[:, attn_end:gate_start]], axis=1).astype(BF16)
    w_gates = w_in[:, gate_start:].astype(BF16)
    wf = _pad_lanes(w_in[:, 3 * FOX_W:attn_end])
    wf_hi, wf_lo = _split2(wf)
    bf = _pad_lanes(b_forget.reshape(1, FOX_HEADS))

    def tile_gain(g, reps):
        return jnp.tile(g.reshape(1, -1), (1, reps))

    xn, c = _prenorm(x2, norm_mix.reshape(1, D), wf_hi, wf_lo, bf, S)
    proj = _matmul(xn, w_attn, F32, False, "in_proj_attn")
    gates = _matmul(xn, w_gates, BF16, True, "in_proj_gates")

    qa, ka, vb = _fox_prep(proj, c, tile_gain(fox_q_norm, FOX_HEADS), tile_gain(fox_k_norm, FOX_HEADS))
    o_fox = _fox_attn(qa, ka, vb, B, S)

    o_swa = _swa_attn(proj, positions.reshape(T, 1), swa_sinks.astype(F32),
                      _pad_lanes(swa_q_norm.reshape(1, HEAD_DIM)), _pad_lanes(swa_k_norm.reshape(1, HEAD_DIM)),
                      B, S)

    mk, mv = _mem_kv(mem.reshape(B * M, D), norm_mem.reshape(1, D), w_mem_kv.astype(BF16),
                     xmem_k_norm.reshape(1, XMEM_HEAD_DIM))
    o_x = _mem_attn(proj, mk, mv, xmem_q_norm.reshape(1, XMEM_HEAD_DIM), S, M)

    wr = _pad_lanes(w_router)
    wr_hi, wr_lo = _split2(wr)
    h1, hn, route_i, route_g, counts = _merge(
        o_fox, o_swa, o_x, gates, x2, w_up_fox.astype(BF16), w_up_swa.astype(BF16), w_up_xmem.astype(BF16),
        w_out.astype(BF16), norm_ffn.reshape(1, D), wr_hi, wr_lo, _pad_lanes(b_router.reshape(1, N_EXPERTS)))

    bm = MOE_BM
    cnt = counts[0, :N_EXPERTS].astype(jnp.int32)
    padded = ((cnt + bm - 1) // bm) * bm
    pad_end = jnp.cumsum(padded)
    pad_start = (pad_end - padded).astype(jnp.int32)
    P = T * TOP_K + N_EXPERTS * bm
    n_blk = P // bm
    blk_expert = jnp.clip(jnp.searchsorted(pad_end, jnp.arange(n_blk, dtype=jnp.int32) * bm, side='right'),
                          0, N_EXPERTS - 1).astype(jnp.int32)
    n_used = (pad_end[-1:] // bm).astype(jnp.int32)
    e_flat = route_i[:, :TOP_K].reshape(T * TOP_K)
    r_flat = route_i[:, TOP_K:2 * TOP_K].reshape(T * TOP_K)

    sub = D // LANES
    xs = _dispatch(pad_start, e_flat, r_flat, hn.reshape(T, sub, LANES), jnp.zeros((P, sub, LANES), BF16))
    ys = _experts(blk_expert, n_used, xs.reshape(P, D), w_gate_up, b_gate_up.reshape(N_EXPERTS, 1, -1),
                  w_down, b_down.reshape(N_EXPERTS, 1, -1))
    out = _combine(pad_start, e_flat, r_flat, h1, route_g, ys)
    return out.reshape(B, S, D)


def kernel(x, mem, positions, norm_mix, w_in, b_forget, fox_q_norm, fox_k_norm, swa_q_norm, swa_k_norm, swa_sinks, xmem_q_norm, xmem_k_norm, norm_mem, w_mem_kv, w_up_fox, w_up_swa, w_up_xmem, w_out, norm_ffn, w_router, b_router, w_gate_up, b_gate_up, w_down, b_down):
    h = x
    for layer in range(norm_mix.shape[0]):
        h = _layer(
            h, mem, positions, norm_mix[layer], w_in[layer], b_forget[layer],
            fox_q_norm[layer], fox_k_norm[layer], swa_q_norm[layer], swa_k_norm[layer],
            swa_sinks[layer], xmem_q_norm[layer], xmem_k_norm[layer], norm_mem[layer],
            w_mem_kv[layer], w_up_fox[layer], w_up_swa[layer], w_up_xmem[layer],
            w_out[layer], norm_ffn[layer], w_router[layer], b_router[layer],
            w_gate_up[layer], b_gate_up[layer], w_down[layer], b_down[layer])
    return h
```

```python
import functools

import numpy as np
import jax
import jax.numpy as jnp
from jax import lax
from jax.experimental import pallas as pl
from jax.experimental.pallas import tpu as pltpu

F32 = jnp.float32
BF16 = jnp.bfloat16

D_MODEL = 2048
HEAD_DIM = 64
FOX_HEADS = 12
SWA_HEADS = 12
SWA_KV_HEADS = 4
SWA_GROUP = SWA_HEADS // SWA_KV_HEADS
WINDOW = 128
XMEM_HEADS = 4
XMEM_HEAD_DIM = 128
N_EXPERTS = 32
TOP_K = 4
SWIGLU_LIMIT = 7.0
SWIGLU_ALPHA = 1.702
ROPE_THETA = 10000.0
NORM_EPS = 1e-6

FOX_W = FOX_HEADS * HEAD_DIM
SWA_QW = SWA_HEADS * HEAD_DIM
SWA_KVW = SWA_KV_HEADS * HEAD_DIM
XMEM_W = XMEM_HEADS * XMEM_HEAD_DIM
FOX_PAIRS = FOX_HEADS // 2

LANES = 128
NEG_BIG = -1e30
LOG2E = 1.4426950408889634
MIB = 1024 * 1024

PRE_TM = 512
MM_TM = 1024
MM_TN = 1024
PREP_TM = 512
FOX_TQ = 512
FOX_TK = 512
MEM_TQ = 512
MERGE_TM = 256
MOE_BM = 512
MOE_TN = 1024
DISPATCH_TB = 256
COMBINE_TB = 128


def _cparams(semantics, vmem_mib):
    return pltpu.CompilerParams(dimension_semantics=semantics, vmem_limit_bytes=vmem_mib * MIB)


def _dot(a, b):
    return jnp.dot(a, b, preferred_element_type=F32)


def _dot_nt(a, b):
    return lax.dot_general(a, b, (((1,), (1,)), ((), ())), preferred_element_type=F32)


def _split2(x):
    hi = x.astype(BF16)
    lo = (x - hi.astype(F32)).astype(BF16)
    return hi, lo


def _split3(x):
    hi = x.astype(BF16)
    r = x - hi.astype(F32)
    mid = r.astype(BF16)
    lo = (r - mid.astype(F32)).astype(BF16)
    return hi, mid, lo


def _prenorm_kernel(x_ref, g_ref, wf_hi_ref, wf_lo_ref, bf_ref, xn_ref, c_ref, carry_ref, *, blocks_per_seq):
    i = pl.program_id(0)

    @pl.when(i % blocks_per_seq == 0)
    def _():
        carry_ref[...] = jnp.zeros_like(carry_ref)

    x = x_ref[...]
    tm = x.shape[0]
    ms = jnp.mean(x * x, axis=-1, keepdims=True)
    xn = x * lax.rsqrt(ms + NORM_EPS) * g_ref[...]
    xn_hi, xn_lo = _split2(xn)
    xn_ref[...] = xn_hi
    wf_hi = wf_hi_ref[...]
    z = _dot(xn_hi, wf_hi) + _dot(xn_lo, wf_hi) + _dot(xn_hi, wf_lo_ref[...]) + bf_ref[...]
    logf = jnp.minimum(z, 0.0) - jnp.log1p(jnp.exp(-jnp.abs(z)))
    lane = lax.broadcasted_iota(jnp.int32, logf.shape, 1)
    logf = jnp.where(lane < FOX_HEADS, logf, 0.0)
    hi, mid, lo = _split3(logf)
    r = lax.broadcasted_iota(jnp.int32, (tm, tm), 0)
    cc = lax.broadcasted_iota(jnp.int32, (tm, tm), 1)
    tri = jnp.where(cc <= r, 1.0, 0.0).astype(BF16)
    c = _dot(tri, hi) + _dot(tri, mid) + _dot(tri, lo) + carry_ref[...]
    c_ref[...] = c
    carry_ref[...] = c[tm - 1:tm, :]


def _prenorm(x2, gain, wf_hi, wf_lo, bf, seq):
    T = x2.shape[0]
    tm = PRE_TM
    return pl.pallas_call(
        functools.partial(_prenorm_kernel, blocks_per_seq=seq // tm),
        grid=(T // tm,),
        in_specs=[
            pl.BlockSpec((tm, D_MODEL), lambda i: (i, 0)),
            pl.BlockSpec((1, D_MODEL), lambda i: (0, 0)),
            pl.BlockSpec((D_MODEL, LANES), lambda i: (0, 0)),
            pl.BlockSpec((D_MODEL, LANES), lambda i: (0, 0)),
            pl.BlockSpec((1, LANES), lambda i: (0, 0)),
        ],
        out_specs=[
            pl.BlockSpec((tm, D_MODEL), lambda i: (i, 0)),
            pl.BlockSpec((tm, LANES), lambda i: (i, 0)),
        ],
        out_shape=[
            jax.ShapeDtypeStruct((T, D_MODEL), BF16),
            jax.ShapeDtypeStruct((T, LANES), F32),
        ],
        scratch_shapes=[pltpu.VMEM((1, LANES), F32)],
        compiler_params=_cparams(("arbitrary",), 40),
        name="prenorm",
    )(x2, gain, wf_hi, wf_lo, bf)


def _mm_kernel(x_ref, w_ref, o_ref, *, sigmoid):
    acc = _dot(x_ref[...], w_ref[...])
    if sigmoid:
        acc = jax.nn.sigmoid(acc)
    o_ref[...] = acc.astype(o_ref.dtype)


def _matmul(x, w, out_dtype, sigmoid, name):
    M, K = x.shape
    N = w.shape[1]
    tm, tn = MM_TM, MM_TN
    return pl.pallas_call(
        functools.partial(_mm_kernel, sigmoid=sigmoid),
        grid=(M // tm, N // tn),
        in_specs=[
            pl.BlockSpec((tm, K), lambda i, j: (i, 0)),
            pl.BlockSpec((K, tn), lambda i, j: (0, j)),
        ],
        out_specs=pl.BlockSpec((tm, tn), lambda i, j: (i, j)),
        out_shape=jax.ShapeDtypeStruct((M, N), out_dtype),
        compiler_params=_cparams(("parallel", "parallel"), 48),
        name=name,
    )(x, w)


def _foxprep_kernel(q_ref, k_ref, v_ref, c_ref, gq_ref, gk_ref, bd_ref, selq_ref, selk_ref,
                    onesq_ref, onesk_ref, qa_ref, ka_ref, vb_ref):
    bd = bd_ref[...]

    def headnorm(x, g):
        hi, lo = _split2(x * x)
        ssq = _dot(hi, bd) + _dot(lo, bd)
        return x * lax.rsqrt(ssq * (1.0 / HEAD_DIM) + NORM_EPS) * g

    qn = headnorm(q_ref[...], gq_ref[...]) * (HEAD_DIM ** -0.5 * LOG2E)
    kn = headnorm(k_ref[...], gk_ref[...])
    vb_ref[...] = v_ref[...].astype(BF16)
    hi, mid, lo = _split3(c_ref[...] * LOG2E)
    c3 = jnp.concatenate([hi, mid, lo], axis=1)
    lane = lax.broadcasted_iota(jnp.int32, (qn.shape[0], LANES), 1)
    for p in range(FOX_PAIRS):
        qp = qn[:, p * LANES:(p + 1) * LANES]
        for j in range(2):
            h = 2 * p + j
            keep = (lane < HEAD_DIM) if j == 0 else (lane >= HEAD_DIM)
            qa_ref[h, :, 0:LANES] = jnp.where(keep, qp, 0.0).astype(BF16)
            qa_ref[h, :, LANES:2 * LANES] = (_dot(c3, selq_ref[h]) + onesq_ref[h]).astype(BF16)
        ka_ref[p, :, 0:LANES] = kn[:, p * LANES:(p + 1) * LANES].astype(BF16)
        ka_ref[p, :, LANES:2 * LANES] = (_dot(c3, selk_ref[p]) + onesk_ref[...]).astype(BF16)


def _fox_tables():
    selq = np.zeros((FOX_HEADS, 3 * LANES, LANES), np.float32)
    selk = np.zeros((FOX_PAIRS, 3 * LANES, LANES), np.float32)
    onesq = np.zeros((FOX_HEADS, 1, LANES), np.float32)
    onesk = np.zeros((1, LANES), np.float32)
    for h in range(FOX_HEADS):
        p, j = divmod(h, 2)
        for piece in range(3):
            selq[h, piece * LANES + h, 6 * j + piece] = 1.0
            selk[p, piece * LANES + h, 6 * j + 3 + piece] = -1.0
            onesq[h, 0, 6 * j + 3 + piece] = 1.0
            onesk[0, 6 * j + piece] = 1.0
    bd = np.kron(np.eye(FOX_HEADS, dtype=np.float32), np.ones((HEAD_DIM, HEAD_DIM), np.float32))
    return (jnp.asarray(bd, BF16), jnp.asarray(selq, BF16), jnp.asarray(selk, BF16),
            jnp.asarray(onesq, F32), jnp.asarray(onesk, F32))


def _fox_prep(proj, c, gq, gk):
    T = proj.shape[0]
    tm = PREP_TM
    bd, selq, selk, onesq, onesk = _fox_tables()
    const2 = lambda i: (0, 0)
    const3 = lambda i: (0, 0, 0)
    return pl.pallas_call(
        _foxprep_kernel,
        grid=(T // tm,),
        in_specs=[
            pl.BlockSpec((tm, FOX_W), lambda i: (i, 0)),
            pl.BlockSpec((tm, FOX_W), lambda i: (i, 1)),
            pl.BlockSpec((tm, FOX_W), lambda i: (i, 2)),
            pl.BlockSpec((tm, LANES), lambda i: (i, 0)),
            pl.BlockSpec((1, FOX_W), const2),
            pl.BlockSpec((1, FOX_W), const2),
            pl.BlockSpec((FOX_W, FOX_W), const2),
            pl.BlockSpec((FOX_HEADS, 3 * LANES, LANES), const3),
            pl.BlockSpec((FOX_PAIRS, 3 * LANES, LANES), const3),
            pl.BlockSpec((FOX_HEADS, 1, LANES), const3),
            pl.BlockSpec((1, LANES), const2),
        ],
        out_specs=[
            pl.BlockSpec((FOX_HEADS, tm, 2 * LANES), lambda i: (0, i, 0)),
            pl.BlockSpec((FOX_PAIRS, tm, 2 * LANES), lambda i: (0, i, 0)),
            pl.BlockSpec((tm, FOX_W), lambda i: (i, 0)),
        ],
        out_shape=[
            jax.ShapeDtypeStruct((FOX_HEADS, T, 2 * LANES), BF16),
            jax.ShapeDtypeStruct((FOX_PAIRS, T, 2 * LANES), BF16),
            jax.ShapeDtypeStruct((T, FOX_W), BF16),
        ],
        compiler_params=_cparams(("parallel",), 48),
        name="fox_prep",
    )(proj, proj, proj, c, gq, gk, bd, selq, selk, onesq, onesk)


def _fox_attn_kernel(q_ref, k_ref, vt_ref, o_ref, m_ref, acc_ref):
    qi = pl.program_id(2)
    tk = FOX_TK
    m_ref[...] = jnp.full(m_ref.shape, NEG_BIG, F32)
    acc_ref[...] = jnp.zeros(acc_ref.shape, F32)

    def step(kb, masked):
        off = pl.multiple_of(kb * tk, tk)
        k = k_ref[pl.ds(off, tk), :]
        scores = [_dot_nt(k, q_ref[j]) for j in range(2)]
        for j in range(2):
            s = scores[j]
            if masked:
                key = lax.broadcasted_iota(jnp.int32, s.shape, 0)
                qry = lax.broadcasted_iota(jnp.int32, s.shape, 1)
                s = jnp.where(key <= qry, s, NEG_BIG)
            m_old = m_ref[j]
            m_new = jnp.maximum(m_old, jnp.max(s, axis=0, keepdims=True))
            alpha = jnp.exp2(m_old - m_new)
            p = jnp.exp2(s - m_new)
            acc_ref[j] = alpha * acc_ref[j] + _dot(vt_ref[j, kb], p.astype(BF16))
            m_ref[j] = m_new

    def body(kb, carry):
        step(kb, False)
        return carry

    lax.fori_loop(0, qi, body, 0)
    step(qi, True)
    outs = [acc_ref[j, 0:HEAD_DIM, :] / acc_ref[j, HEAD_DIM:HEAD_DIM + 1, :] for j in range(2)]
    o_ref[...] = jnp.concatenate(outs, axis=0).T.astype(o_ref.dtype)


def _fox_attn(qa, ka, vb, batch, seq):
    T = vb.shape[0]
    tq, tk = FOX_TQ, FOX_TK
    assert tq == tk
    nq = seq // tq
    nk = seq // tk
    qa4 = qa.reshape(FOX_PAIRS, 2, T, 2 * LANES)
    vt = vb.reshape(batch, nk, tk, FOX_PAIRS, 2, HEAD_DIM).transpose(0, 3, 4, 1, 5, 2)
    vt = jnp.concatenate([vt, jnp.ones(vt.shape[:4] + (1, tk), BF16),
                          jnp.zeros(vt.shape[:4] + (LANES - HEAD_DIM - 1, tk), BF16)], axis=4)
    return pl.pallas_call(
        _fox_attn_kernel,
        grid=(batch, FOX_PAIRS, nq),
        in_specs=[
            pl.BlockSpec((None, 2, tq, 2 * LANES), lambda b, p, q: (p, 0, b * nq + q, 0)),
            pl.BlockSpec((None, seq, 2 * LANES), lambda b, p, q: (p, b, 0)),
            pl.BlockSpec((None, None, 2, nk, LANES, tk), lambda b, p, q: (b, p, 0, 0, 0, 0)),
        ],
        out_specs=pl.BlockSpec((tq, LANES), lambda b, p, q: (b * nq + q, p)),
        out_shape=jax.ShapeDtypeStruct((T, FOX_W), BF16),
        scratch_shapes=[
            pltpu.VMEM((2, 1, tq), F32),
            pltpu.VMEM((2, LANES, tq), F32),
        ],
        compiler_params=_cparams(("parallel", "parallel", "arbitrary"), 48),
        name="fox_attn",
    )(qa4, ka, vt)


def _swa_kernel(sinks_ref, q_ref, k_ref, v_ref, pos_ref, freq_ref, sign_ref, gq_ref, gk_ref,
                o_ref, kprev_ref, vprev_ref):
    n = pl.program_id(1)
    W = WINDOW

    @pl.when(n == 0)
    def _():
        kprev_ref[...] = jnp.zeros_like(kprev_ref)
        vprev_ref[...] = jnp.zeros_like(vprev_ref)

    lane = lax.broadcasted_iota(jnp.int32, (W, LANES), 1)
    low = lane < HEAD_DIM
    ang = pos_ref[...].astype(F32) * freq_ref[...]
    cos_t = jnp.where(low, jnp.cos(ang), 0.0)
    sin_t = jnp.sin(ang) * sign_ref[...]

    def head_slice(x, h):
        blk = x[:, (h // 2) * LANES:(h // 2 + 1) * LANES]
        if h % 2 == 1:
            blk = pltpu.roll(blk, HEAD_DIM, axis=1)
        return jnp.where(low, blk, 0.0)

    def norm_rope(x, g):
        ms = jnp.sum(x * x, axis=-1, keepdims=True) * (1.0 / HEAD_DIM)
        xn = x * lax.rsqrt(ms + NORM_EPS) * g
        half = HEAD_DIM // 2
        swapped = jnp.where(lane < half, pltpu.roll(xn, LANES - half, axis=1), pltpu.roll(xn, half, axis=1))
        return xn * cos_t + swapped * sin_t

    row = lax.broadcasted_iota(jnp.int32, (W, W), 0)
    col = lax.broadcasted_iota(jnp.int32, (W, W), 1)
    cur_ok = col <= row
    prev_ok = (col > row) & (n > 0)

    q_all = q_ref[...]
    k_all = k_ref[...]
    v_all = v_ref[...]
    outs = []
    for g in range(SWA_KV_HEADS):
        k_cur = norm_rope(head_slice(k_all, g), gk_ref[...]).astype(BF16)
        v_cur = head_slice(v_all, g).astype(BF16)
        k_prev = kprev_ref[g]
        v_prev = vprev_ref[g]
        for r in range(SWA_GROUP):
            h = g * SWA_GROUP + r
            qh = (norm_rope(head_slice(q_all, h), gq_ref[...]) * (HEAD_DIM ** -0.5)).astype(BF16)
            s_cur = jnp.where(cur_ok, _dot_nt(qh, k_cur), NEG_BIG)
            s_prev = jnp.where(prev_ok, _dot_nt(qh, k_prev), NEG_BIG)
            sink = sinks_ref[h]
            m = jnp.maximum(jnp.maximum(jnp.max(s_cur, axis=-1, keepdims=True),
                                        jnp.max(s_prev, axis=-1, keepdims=True)), sink)
            p_cur = jnp.exp(s_cur - m)
            p_prev = jnp.exp(s_prev - m)
            denom = (jnp.sum(p_cur, axis=-1, keepdims=True) + jnp.sum(p_prev, axis=-1, keepdims=True)
                     + jnp.exp(sink - m))
            o = (_dot(p_cur.astype(BF16), v_cur) + _dot(p_prev.astype(BF16), v_prev)) / denom
            outs.append(o)
        kprev_ref[g] = k_cur
        vprev_ref[g] = v_cur
    for pr in range(SWA_HEADS // 2):
        o_ref[:, pr * LANES:(pr + 1) * LANES] = (
            outs[2 * pr] + pltpu.roll(outs[2 * pr + 1], HEAD_DIM, axis=1)).astype(o_ref.dtype)


def _swa_attn(proj, pos2, sinks, gq, gk, batch, seq):
    T = proj.shape[0]
    W = WINDOW
    nb = seq // W
    half = HEAD_DIM // 2
    inv_freq = np.power(ROPE_THETA, -np.arange(0, HEAD_DIM, 2, dtype=np.float32) / HEAD_DIM).astype(np.float32)
    freq = np.zeros((1, LANES), np.float32)
    freq[0, :half] = inv_freq
    freq[0, half:HEAD_DIM] = inv_freq
    sign = np.zeros((1, LANES), np.float32)
    sign[0, :half] = -1.0
    sign[0, half:HEAD_DIM] = 1.0
    q_col = (3 * FOX_W) // SWA_QW
    k_col = (3 * FOX_W + SWA_QW) // SWA_KVW
    const2 = lambda b, n, s: (0, 0)
    grid_spec = pltpu.PrefetchScalarGridSpec(
        num_scalar_prefetch=1,
        grid=(batch, nb),
        in_specs=[
            pl.BlockSpec((W, SWA_QW), lambda b, n, s: (b * nb + n, q_col)),
            pl.BlockSpec((W, SWA_KVW), lambda b, n, s: (b * nb + n, k_col)),
            pl.BlockSpec((W, SWA_KVW), lambda b, n, s: (b * nb + n, k_col + 1)),
            pl.BlockSpec((W, 1), lambda b, n, s: (b * nb + n, 0)),
            pl.BlockSpec((1, LANES), const2),
            pl.BlockSpec((1, LANES), const2),
            pl.BlockSpec((1, LANES), const2),
            pl.BlockSpec((1, LANES), const2),
        ],
        out_specs=pl.BlockSpec((W, SWA_QW), lambda b, n, s: (b * nb + n, 0)),
        scratch_shapes=[
            pltpu.VMEM((SWA_KV_HEADS, W, LANES), BF16),
            pltpu.VMEM((SWA_KV_HEADS, W, LANES), BF16),
        ],
    )
    return pl.pallas_call(
        _swa_kernel,
        grid_spec=grid_spec,
        out_shape=jax.ShapeDtypeStruct((T, SWA_QW), BF16),
        compiler_params=_cparams(("parallel", "arbitrary"), 32),
        name="swa_attn",
    )(sinks, proj, proj, proj, pos2, jnp.asarray(freq), jnp.asarray(sign), gq, gk)


def _memkv_kernel(mem_ref, g_ref, w_ref, gk_ref, k_ref, v_ref):
    x = mem_ref[...]
    ms = jnp.mean(x * x, axis=-1, keepdims=True)
    mn = (x * lax.rsqrt(ms + NORM_EPS) * g_ref[...]).astype(BF16)
    kv = _dot(mn, w_ref[...])
    for h in range(XMEM_HEADS):
        kh = kv[:, h * LANES:(h + 1) * LANES]
        ms_h = jnp.mean(kh * kh, axis=-1, keepdims=True)
        k_ref[:, h * LANES:(h + 1) * LANES] = (kh * lax.rsqrt(ms_h + NORM_EPS) * gk_ref[...]).astype(BF16)
    v_ref[...] = kv[:, XMEM_W:].astype(BF16)


def _mem_kv(mem2, gain, w, gk):
    R = mem2.shape[0]
    tm = 256
    return pl.pallas_call(
        _memkv_kernel,
        grid=(R // tm,),
        in_specs=[
            pl.BlockSpec((tm, D_MODEL), lambda i: (i, 0)),
            pl.BlockSpec((1, D_MODEL), lambda i: (0, 0)),
            pl.BlockSpec((D_MODEL, 2 * XMEM_W), lambda i: (0, 0)),
            pl.BlockSpec((1, XMEM_HEAD_DIM), lambda i: (0, 0)),
        ],
        out_specs=[
            pl.BlockSpec((tm, XMEM_W), lambda i: (i, 0)),
            pl.BlockSpec((tm, XMEM_W), lambda i: (i, 0)),
        ],
        out_shape=[
            jax.ShapeDtypeStruct((R, XMEM_W), BF16),
            jax.ShapeDtypeStruct((R, XMEM_W), BF16),
        ],
        compiler_params=_cparams(("parallel",), 32),
        name="mem_kv",
    )(mem2, gain, w, gk)


def _memattn_kernel(q_ref, k_ref, v_ref, gq_ref, o_ref):
    q = q_ref[...]
    for h in range(XMEM_HEADS):
        sl = slice(h * LANES, (h + 1) * LANES)
        qh = q[:, sl]
        ms = jnp.mean(qh * qh, axis=-1, keepdims=True)
        qn = (qh * lax.rsqrt(ms + NORM_EPS) * gq_ref[...] * (XMEM_HEAD_DIM ** -0.5)).astype(BF16)
        s = _dot_nt(qn, k_ref[:, sl])
        m = jnp.max(s, axis=-1, keepdims=True)
        p = jnp.exp(s - m)
        l = jnp.sum(p, axis=-1, keepdims=True)
        o_ref[:, sl] = (_dot(p.astype(BF16), v_ref[:, sl]) / l).astype(o_ref.dtype)


def _mem_attn(proj, mk, mv, gq, seq, n_mem):
    T = proj.shape[0]
    tq = MEM_TQ
    per_seq = seq // tq
    q_col = (3 * FOX_W + SWA_QW + 2 * SWA_KVW) // XMEM_W
    return pl.pallas_call(
        _memattn_kernel,
        grid=(T // tq,),
        in_specs=[
            pl.BlockSpec((tq, XMEM_W), lambda i: (i, q_col)),
            pl.BlockSpec((n_mem, XMEM_W), lambda i: (i // per_seq, 0)),
            pl.BlockSpec((n_mem, XMEM_W), lambda i: (i // per_seq, 0)),
            pl.BlockSpec((1, XMEM_HEAD_DIM), lambda i: (0, 0)),
        ],
        out_specs=pl.BlockSpec((tq, XMEM_W), lambda i: (i, 0)),
        out_shape=jax.ShapeDtypeStruct((T, XMEM_W), BF16),
        compiler_params=_cparams(("parallel",), 32),
        name="mem_attn",
    )(proj, mk, mv, gq)


def _merge_kernel(of_ref, os_ref, ox_ref, g0_ref, g1_ref, g2_ref, x_ref, wf_ref, ws_ref, wx_ref, wo_ref,
                  gn_ref, wr_hi_ref, wr_lo_ref, br_ref,
                  h_ref, hn_ref, ri_ref, rg_ref, cnt_ref, carry_ref):
    i = pl.program_id(0)

    @pl.when(i == 0)
    def _():
        carry_ref[...] = jnp.zeros_like(carry_ref)

    merged = (g0_ref[...].astype(F32) * _dot(of_ref[...], wf_ref[...])
              + g1_ref[...].astype(F32) * _dot(os_ref[...], ws_ref[...])
              + g2_ref[...].astype(F32) * _dot(ox_ref[...], wx_ref[...]))
    h = x_ref[...] + _dot(merged.astype(BF16), wo_ref[...])
    h_ref[...] = h
    ms = jnp.mean(h * h, axis=-1, keepdims=True)
    hn = h * lax.rsqrt(ms + NORM_EPS) * gn_ref[...]
    hn_hi, hn_lo = _split2(hn)
    hn_ref[...] = hn
    wr_hi = wr_hi_ref[...]
    logits = _dot(hn_hi, wr_hi) + _dot(hn_lo, wr_hi) + _dot(hn_hi, wr_lo_ref[...]) + br_ref[...]
    tm = logits.shape[0]
    lane = lax.broadcasted_iota(jnp.int32, (tm, LANES), 1).astype(F32)
    work = jnp.where(lane < N_EXPERTS, logits, NEG_BIG)
    vals, idxs = [], []
    for _ in range(TOP_K):
        mx = jnp.max(work, axis=-1, keepdims=True)
        ix = jnp.min(jnp.where(work == mx, lane, float(LANES)), axis=-1, keepdims=True)
        vals.append(mx)
        idxs.append(ix)
        work = jnp.where(lane == ix, NEG_BIG, work)
    es = [jnp.exp(v - vals[0]) for v in vals]
    den = es[0] + es[1] + es[2] + es[3]
    onehot = jnp.zeros((tm, LANES), F32)
    for ix in idxs:
        onehot = onehot + jnp.where(lane == ix, 1.0, 0.0)
    r = lax.broadcasted_iota(jnp.int32, (tm, tm), 0)
    cc = lax.broadcasted_iota(jnp.int32, (tm, tm), 1)
    tri = jnp.where(cc < r, 1.0, 0.0).astype(BF16)
    before = _dot(tri, onehot.astype(BF16)) + carry_ref[...]
    ri = jnp.zeros((tm, LANES), jnp.int32)
    rg = jnp.zeros((tm, LANES), F32)
    for k in range(TOP_K):
        rank = jnp.sum(jnp.where(lane == idxs[k], before, 0.0), axis=-1, keepdims=True)
        ri = jnp.where(lane == k, idxs[k].astype(jnp.int32), ri)
        ri = jnp.where(lane == TOP_K + k, rank.astype(jnp.int32), ri)
        rg = jnp.where(lane == k, es[k] / den, rg)
    ri_ref[...] = ri
    rg_ref[...] = rg
    total = carry_ref[...] + jnp.sum(onehot, axis=0, keepdims=True)
    carry_ref[...] = total
    cnt_ref[...] = total


def _merge(o_fox, o_swa, o_x, gates, x2, wf, ws, wx, wo, gn, wr_hi, wr_lo, br):
    T = x2.shape[0]
    tm = MERGE_TM
    row = lambda i: (i, 0)
    const = lambda i: (0, 0)
    resident = functools.partial(pl.BlockSpec, index_map=const, pipeline_mode=pl.Buffered(1))
    return pl.pallas_call(
        _merge_kernel,
        grid=(T // tm,),
        in_specs=[
            pl.BlockSpec((tm, FOX_W), row),
            pl.BlockSpec((tm, SWA_QW), row),
            pl.BlockSpec((tm, XMEM_W), row),
            pl.BlockSpec((tm, D_MODEL), lambda i: (i, 0)),
            pl.BlockSpec((tm, D_MODEL), lambda i: (i, 1)),
            pl.BlockSpec((tm, D_MODEL), lambda i: (i, 2)),
            pl.BlockSpec((tm, D_MODEL), row),
            resident((FOX_W, D_MODEL)),
            resident((SWA_QW, D_MODEL)),
            resident((XMEM_W, D_MODEL)),
            resident((D_MODEL, D_MODEL)),
            resident((1, D_MODEL)),
            resident((D_MODEL, LANES)),
            resident((D_MODEL, LANES)),
            resident((1, LANES)),
        ],
        out_specs=[
            pl.BlockSpec((tm, D_MODEL), row),
            pl.BlockSpec((tm, D_MODEL), row),
            pl.BlockSpec((tm, LANES), row),
            pl.BlockSpec((tm, LANES), row),
            pl.BlockSpec((1, LANES), const),
        ],
        out_shape=[
            jax.ShapeDtypeStruct((T, D_MODEL), F32),
            jax.ShapeDtypeStruct((T, D_MODEL), F32),
            jax.ShapeDtypeStruct((T, LANES), jnp.int32),
            jax.ShapeDtypeStruct((T, LANES), F32),
            jax.ShapeDtypeStruct((1, LANES), F32),
        ],
        scratch_shapes=[pltpu.VMEM((1, LANES), F32)],
        compiler_params=_cparams(("arbitrary",), 56),
        name="merge_router",
    )(o_fox, o_swa, o_x, gates, gates, gates, x2, wf, ws, wx, wo, gn, wr_hi, wr_lo, br)


def _dispatch_kernel(start_ref, cnt_ref, e_ref, r_ref, hn_ref, xs_ref, zero_ref, sem):
    i = pl.program_id(0)
    tb = DISPATCH_TB

    def row_copy(src_ref, t, slot):
        return pltpu.make_async_copy(src_ref.at[pl.ds(t, 1), :], xs_ref.at[pl.ds(slot, 1), :], sem)

    @pl.when(i == 0)
    def _():
        zero_ref[...] = jnp.zeros_like(zero_ref)

        def per_expert(e, carry):
            base = start_ref[e]
            n = cnt_ref[e]
            end = ((n + MOE_BM - 1) // MOE_BM) * MOE_BM

            def fill(r, c):
                row_copy(zero_ref, 0, base + r).start()
                return c

            def drain(r, c):
                row_copy(zero_ref, 0, base + r).wait()
                return c

            lax.fori_loop(n, end, fill, 0)
            lax.fori_loop(n, end, drain, 0)
            return carry

        lax.fori_loop(0, N_EXPERTS, per_expert, 0)

        last = N_EXPERTS - 1
        used = start_ref[last] + ((cnt_ref[last] + MOE_BM - 1) // MOE_BM) * MOE_BM
        zr = zero_ref.shape[0]

        def tail_copy(r):
            return pltpu.make_async_copy(zero_ref, xs_ref.at[pl.ds(pl.multiple_of(used + r * zr, zr), zr), :], sem)

        def tail_fill(r, c):
            tail_copy(r).start()
            return c

        def tail_drain(r, c):
            tail_copy(r).wait()
            return c

        n_tail = (xs_ref.shape[0] - used) // zr
        lax.fori_loop(0, n_tail, tail_fill, 0)
        lax.fori_loop(0, n_tail, tail_drain, 0)

    def token_copy(t, k):
        a = t * TOP_K + k
        return row_copy(hn_ref, t, start_ref[e_ref[a]] + r_ref[a])

    def issue(t, carry):
        for k in range(TOP_K):
            token_copy(t, k).start()
        return carry

    def drain(t, carry):
        for k in range(TOP_K):
            token_copy(t, k).wait()
        return carry

    lax.fori_loop(0, tb, issue, 0)
    lax.fori_loop(0, tb, drain, 0)


def _dispatch(pad_start, cnt, e_flat, r_flat, hn, n_slots):
    T, D = hn.shape
    tb = DISPATCH_TB
    grid_spec = pltpu.PrefetchScalarGridSpec(
        num_scalar_prefetch=2,
        grid=(T // tb,),
        in_specs=[
            pl.BlockSpec((tb * TOP_K,), lambda i, s, c: (i,), memory_space=pltpu.SMEM),
            pl.BlockSpec((tb * TOP_K,), lambda i, s, c: (i,), memory_space=pltpu.SMEM),
            pl.BlockSpec((tb, D), lambda i, s, c: (i, 0)),
        ],
        out_specs=pl.BlockSpec(memory_space=pl.ANY),
        scratch_shapes=[pltpu.VMEM((MOE_BM // 2, D), F32), pltpu.SemaphoreType.DMA(())],
    )
    return pl.pallas_call(
        _dispatch_kernel,
        grid_spec=grid_spec,
        out_shape=jax.ShapeDtypeStruct((n_slots, D), F32),
        compiler_params=_cparams(("arbitrary",), 32),
        name="dispatch",
    )(pad_start, cnt, e_flat, r_flat, hn)


def _expert_changed(be_ref, i):
    return (i == 0) | (be_ref[i] != be_ref[jnp.maximum(i - 1, 0)])


def _gateup_kernel(be_ref, nu_ref, xs_ref, wg_ref, wu_ref, bg_ref, bu_ref, o_ref, wg_s, wu_s):
    i = pl.program_id(1)

    @pl.when(i < nu_ref[0])
    def _():
        @pl.when(_expert_changed(be_ref, i))
        def _():
            wg_s[...] = wg_ref[0].astype(BF16)
            wu_s[...] = wu_ref[0].astype(BF16)

        x = xs_ref[...].astype(BF16)
        gate = _dot(x, wg_s[...]) + bg_ref[0]
        up = _dot(x, wu_s[...]) + bu_ref[0]
        gate = jnp.minimum(gate, SWIGLU_LIMIT)
        up = jnp.clip(up, -SWIGLU_LIMIT, SWIGLU_LIMIT)
        glu = gate * jax.nn.sigmoid(gate * SWIGLU_ALPHA)
        o_ref[...] = ((up + 1.0) * glu).astype(o_ref.dtype)

    @pl.when(i >= nu_ref[0])
    def _():
        o_ref[...] = jnp.zeros_like(o_ref)


def _down_kernel(be_ref, nu_ref, h_ref, w_ref, b_ref, o_ref, w_s):
    i = pl.program_id(1)

    @pl.when(i < nu_ref[0])
    def _():
        @pl.when(_expert_changed(be_ref, i))
        def _():
            w_s[...] = w_ref[0].astype(BF16)

        o_ref[...] = _dot(h_ref[...], w_s[...]) + b_ref[0]

    @pl.when(i >= nu_ref[0])
    def _():
        o_ref[...] = jnp.zeros_like(o_ref)


def _experts(blk_expert, n_used, xs, w_gate_up, b_gate_up, w_down, b_down):
    P = xs.shape[0]
    bm, tn = MOE_BM, MOE_TN
    n_blk = P // bm
    d_exp = w_down.shape[1]
    up_off = d_exp // tn

    def blk(i, nu):
        return jnp.minimum(i, nu[0] - 1)

    gateup_spec = pltpu.PrefetchScalarGridSpec(
        num_scalar_prefetch=2,
        grid=(d_exp // tn, n_blk),
        in_specs=[
            pl.BlockSpec((bm, D_MODEL), lambda j, i, be, nu: (blk(i, nu), 0)),
            pl.BlockSpec((1, D_MODEL, tn), lambda j, i, be, nu: (be[blk(i, nu)], 0, j)),
            pl.BlockSpec((1, D_MODEL, tn), lambda j, i, be, nu: (be[blk(i, nu)], 0, up_off + j)),
            pl.BlockSpec((1, 1, tn), lambda j, i, be, nu: (be[blk(i, nu)], 0, j)),
            pl.BlockSpec((1, 1, tn), lambda j, i, be, nu: (be[blk(i, nu)], 0, up_off + j)),
        ],
        out_specs=pl.BlockSpec((bm, tn), lambda j, i, be, nu: (i, j)),
        scratch_shapes=[pltpu.VMEM((D_MODEL, tn), BF16), pltpu.VMEM((D_MODEL, tn), BF16)],
    )
    hmid = pl.pallas_call(
        _gateup_kernel,
        grid_spec=gateup_spec,
        out_shape=jax.ShapeDtypeStruct((P, d_exp), BF16),
        compiler_params=_cparams(("arbitrary", "arbitrary"), 56),
        name="expert_gate_up",
    )(blk_expert, n_used, xs, w_gate_up, w_gate_up, b_gate_up, b_gate_up)

    down_spec = pltpu.PrefetchScalarGridSpec(
        num_scalar_prefetch=2,
        grid=(D_MODEL // tn, n_blk),
        in_specs=[
            pl.BlockSpec((bm, d_exp), lambda j, i, be, nu: (blk(i, nu), 0)),
            pl.BlockSpec((1, d_exp, tn), lambda j, i, be, nu: (be[blk(i, nu)], 0, j)),
            pl.BlockSpec((1, 1, tn), lambda j, i, be, nu: (be[blk(i, nu)], 0, j)),
        ],
        out_specs=pl.BlockSpec((bm, tn), lambda j, i, be, nu: (i, j)),
        scratch_shapes=[pltpu.VMEM((d_exp, tn), BF16)],
    )
    return pl.pallas_call(
        _down_kernel,
        grid_spec=down_spec,
        out_shape=jax.ShapeDtypeStruct((P, D_MODEL), F32),
        compiler_params=_cparams(("arbitrary", "arbitrary"), 48),
        name="expert_down",
    )(blk_expert, n_used, hmid, w_down, b_down)


def _combine_kernel(start_ref, e_ref, r_ref, h_ref, g_ref, ys_ref, o_ref, buf, sem):
    tb = COMBINE_TB

    def row_copy(t, k):
        a = t * TOP_K + k
        slot = start_ref[e_ref[a]] + r_ref[a]
        return pltpu.make_async_copy(ys_ref.at[pl.ds(slot, 1), :], buf.at[k, pl.ds(t, 1), :], sem)

    def issue(t, carry):
        for k in range(TOP_K):
            row_copy(t, k).start()
        return carry

    def drain(t, carry):
        for k in range(TOP_K):
            row_copy(t, k).wait()
        return carry

    lax.fori_loop(0, tb, issue, 0)
    lax.fori_loop(0, tb, drain, 0)
    g = g_ref[...]
    acc = h_ref[...]
    for k in range(TOP_K):
        acc = acc + g[:, k:k + 1] * buf[k]
    o_ref[...] = acc


def _combine(pad_start, e_flat, r_flat, h1, gates, ys):
    T = h1.shape[0]
    tb = COMBINE_TB
    grid_spec = pltpu.PrefetchScalarGridSpec(
        num_scalar_prefetch=1,
        grid=(T // tb,),
        in_specs=[
            pl.BlockSpec((tb * TOP_K,), lambda i, s: (i,), memory_space=pltpu.SMEM),
            pl.BlockSpec((tb * TOP_K,), lambda i, s: (i,), memory_space=pltpu.SMEM),
            pl.BlockSpec((tb, D_MODEL), lambda i, s: (i, 0)),
            pl.BlockSpec((tb, LANES), lambda i, s: (i, 0)),
            pl.BlockSpec(memory_space=pl.ANY),
        ],
        out_specs=pl.BlockSpec((tb, D_MODEL), lambda i, s: (i, 0)),
        scratch_shapes=[pltpu.VMEM((TOP_K, tb, D_MODEL), F32), pltpu.SemaphoreType.DMA(())],
    )
    return pl.pallas_call(
        _combine_kernel,
        grid_spec=grid_spec,
        out_shape=jax.ShapeDtypeStruct((T, D_MODEL), F32),
        compiler_params=_cparams(("arbitrary",), 32),
        name="combine",
    )(pad_start, e_flat, r_flat, h1, gates, ys)


def _pad_lanes(a, width=LANES):
    return jnp.pad(a, ((0, 0), (0, width - a.shape[1])))


def _layer(h, mem, positions, norm_mix, w_in, b_forget, fox_q_norm, fox_k_norm, swa_q_norm, swa_k_norm,
           swa_sinks, xmem_q_norm, xmem_k_norm, norm_mem, w_mem_kv, w_up_fox, w_up_swa, w_up_xmem, w_out,
           norm_ffn, w_router, b_router, w_gate_up, b_gate_up, w_down, b_down):
    B, S, D = h.shape
    M = mem.shape[1]
    T = B * S
    x2 = h.reshape(T, D)

    attn_end = 3 * FOX_W + FOX_HEADS
    gate_start = attn_end + SWA_QW + 2 * SWA_KVW + XMEM_W
    w_attn = jnp.concatenate([w_in[:, :3 * FOX_W], w_in[:, attn_end:gate_start]], axis=1).astype(BF16)
    w_gates = w_in[:, gate_start:].astype(BF16)
    wf = _pad_lanes(w_in[:, 3 * FOX_W:attn_end])
    wf_hi, wf_lo = _split2(wf)
    bf = _pad_lanes(b_forget.reshape(1, FOX_HEADS))

    def tile_gain(g, reps):
        return jnp.tile(g.reshape(1, -1), (1, reps))

    xn, c = _prenorm(x2, norm_mix.reshape(1, D), wf_hi, wf_lo, bf, S)
    proj = _matmul(xn, w_attn, F32, False, "in_proj_attn")
    gates = _matmul(xn, w_gates, BF16, True, "in_proj_gates")

    qa, ka, vb = _fox_prep(proj, c, tile_gain(fox_q_norm, FOX_HEADS), tile_gain(fox_k_norm, FOX_HEADS))
    o_fox = _fox_attn(qa, ka, vb, B, S)

    o_swa = _swa_attn(proj, positions.reshape(T, 1), swa_sinks.astype(F32),
                      _pad_lanes(swa_q_norm.reshape(1, HEAD_DIM)), _pad_lanes(swa_k_norm.reshape(1, HEAD_DIM)),
                      B, S)

    mk, mv = _mem_kv(mem.reshape(B * M, D), norm_mem.reshape(1, D), w_mem_kv.astype(BF16),
                     xmem_k_norm.reshape(1, XMEM_HEAD_DIM))
    o_x = _mem_attn(proj, mk, mv, xmem_q_norm.reshape(1, XMEM_HEAD_DIM), S, M)

    wr = _pad_lanes(w_router)
    wr_hi, wr_lo = _split2(wr)
    h1, hn, route_i, route_g, counts = _merge(
        o_fox, o_swa, o_x, gates, x2, w_up_fox.astype(BF16), w_up_swa.astype(BF16), w_up_xmem.astype(BF16),
        w_out.astype(BF16), norm_ffn.reshape(1, D), wr_hi, wr_lo, _pad_lanes(b_router.reshape(1, N_EXPERTS)))

    bm = MOE_BM
    cnt = counts[0, :N_EXPERTS].astype(jnp.int32)
    padded = ((cnt + bm - 1) // bm) * bm
    pad_end = jnp.cumsum(padded)
    pad_start = (pad_end - padded).astype(jnp.int32)
    P = T * TOP_K + N_EXPERTS * bm
    n_blk = P // bm
    blk_first = jnp.arange(n_blk, dtype=jnp.int32) * bm
    blk_expert = jnp.minimum(jnp.sum((pad_end[None, :] <= blk_first[:, None]).astype(jnp.int32), axis=1),
                             N_EXPERTS - 1)
    n_used = (pad_end[-1:] // bm).astype(jnp.int32)
    e_flat = route_i[:, :TOP_K].reshape(T * TOP_K)
    r_flat = route_i[:, TOP_K:2 * TOP_K].reshape(T * TOP_K)

    xs = _dispatch(pad_start, cnt, e_flat, r_flat, hn, P)
    ys = _experts(blk_expert, n_used, xs, w_gate_up, b_gate_up.reshape(N_EXPERTS, 1, -1),
                  w_down, b_down.reshape(N_EXPERTS, 1, -1))
    out = _combine(pad_start, e_flat, r_flat, h1, route_g, ys)
    return out.reshape(B, S, D)


def kernel(x, mem, positions, norm_mix, w_in, b_forget, fox_q_norm, fox_k_norm, swa_q_norm, swa_k_norm, swa_sinks, xmem_q_norm, xmem_k_norm, norm_mem, w_mem_kv, w_up_fox, w_up_swa, w_up_xmem, w_out, norm_ffn, w_router, b_router, w_gate_up, b_gate_up, w_down, b_down):
    h = x
    for layer in range(norm_mix.shape[0]):
        h = _layer(
            h, mem, positions, norm_mix[layer], w_in[layer], b_forget[layer],
            fox_q_norm[layer], fox_k_norm[layer], swa_q_norm[layer], swa_k_norm[layer],
            swa_sinks[layer], xmem_q_norm[layer], xmem_k_norm[layer], norm_mem[layer],
            w_mem_kv[layer], w_up_fox[layer], w_up_swa[layer], w_up_xmem[layer],
            w_out[layer], norm_ffn[layer], w_router[layer], b_router[layer],
            w_gate_up[layer], b_gate_up[layer], w_down[layer], b_down[layer])
    return h
```

```python
import functools

import numpy as np
import jax
import jax.numpy as jnp
from jax import lax
from jax.experimental import pallas as pl
from jax.experimental.pallas import tpu as pltpu

F32 = jnp.float32
BF16 = jnp.bfloat16

D_MODEL = 2048
HEAD_DIM = 64
FOX_HEADS = 12
SWA_HEADS = 12
SWA_KV_HEADS = 4
SWA_GROUP = SWA_HEADS // SWA_KV_HEADS
WINDOW = 128
XMEM_HEADS = 4
XMEM_HEAD_DIM = 128
N_EXPERTS = 32
TOP_K = 4
SWIGLU_LIMIT = 7.0
SWIGLU_ALPHA = 1.702
ROPE_THETA = 10000.0
NORM_EPS = 1e-6

FOX_W = FOX_HEADS * HEAD_DIM
SWA_QW = SWA_HEADS * HEAD_DIM
SWA_KVW = SWA_KV_HEADS * HEAD_DIM
XMEM_W = XMEM_HEADS * XMEM_HEAD_DIM
FOX_PAIRS = FOX_HEADS // 2

LANES = 128
NEG_BIG = -1e30
LOG2E = 1.4426950408889634
MIB = 1024 * 1024

PRE_TM = 512
MM_TM = 1024
MM_TN = 1024
PREP_TM = 512
FOX_TQ = 512
FOX_TK = 512
MEM_TQ = 512
MERGE_TM = 256
MOE_BM = 512
MOE_TN = 1024
DISPATCH_TB = 256
COMBINE_TB = 128


def _cparams(semantics, vmem_mib):
    return pltpu.CompilerParams(dimension_semantics=semantics, vmem_limit_bytes=vmem_mib * MIB)


def _dot(a, b):
    return jnp.dot(a, b, preferred_element_type=F32)


def _dot_nt(a, b):
    return lax.dot_general(a, b, (((1,), (1,)), ((), ())), preferred_element_type=F32)


def _split2(x):
    hi = x.astype(BF16)
    lo = (x - hi.astype(F32)).astype(BF16)
    return hi, lo


def _split3(x):
    hi = x.astype(BF16)
    r = x - hi.astype(F32)
    mid = r.astype(BF16)
    lo = (r - mid.astype(F32)).astype(BF16)
    return hi, mid, lo


def _prenorm_kernel(x_ref, g_ref, wf_hi_ref, wf_lo_ref, bf_ref, xn_ref, c_ref, carry_ref, *, blocks_per_seq):
    i = pl.program_id(0)

    @pl.when(i % blocks_per_seq == 0)
    def _():
        carry_ref[...] = jnp.zeros_like(carry_ref)

    x = x_ref[...]
    tm = x.shape[0]
    ms = jnp.mean(x * x, axis=-1, keepdims=True)
    xn = x * lax.rsqrt(ms + NORM_EPS) * g_ref[...]
    xn_hi, xn_lo = _split2(xn)
    xn_ref[...] = xn_hi
    wf_hi = wf_hi_ref[...]
    z = _dot(xn_hi, wf_hi) + _dot(xn_lo, wf_hi) + _dot(xn_hi, wf_lo_ref[...]) + bf_ref[...]
    logf = jnp.minimum(z, 0.0) - jnp.log1p(jnp.exp(-jnp.abs(z)))
    lane = lax.broadcasted_iota(jnp.int32, logf.shape, 1)
    logf = jnp.where(lane < FOX_HEADS, logf, 0.0)
    hi, mid, lo = _split3(logf)
    r = lax.broadcasted_iota(jnp.int32, (tm, tm), 0)
    cc = lax.broadcasted_iota(jnp.int32, (tm, tm), 1)
    tri = jnp.where(cc <= r, 1.0, 0.0).astype(BF16)
    c = _dot(tri, hi) + _dot(tri, mid) + _dot(tri, lo) + carry_ref[...]
    c_ref[...] = c
    carry_ref[...] = c[tm - 1:tm, :]


def _prenorm(x2, gain, wf_hi, wf_lo, bf, seq):
    T = x2.shape[0]
    tm = PRE_TM
    return pl.pallas_call(
        functools.partial(_prenorm_kernel, blocks_per_seq=seq // tm),
        grid=(T // tm,),
        in_specs=[
            pl.BlockSpec((tm, D_MODEL), lambda i: (i, 0)),
            pl.BlockSpec((1, D_MODEL), lambda i: (0, 0)),
            pl.BlockSpec((D_MODEL, LANES), lambda i: (0, 0)),
            pl.BlockSpec((D_MODEL, LANES), lambda i: (0, 0)),
            pl.BlockSpec((1, LANES), lambda i: (0, 0)),
        ],
        out_specs=[
            pl.BlockSpec((tm, D_MODEL), lambda i: (i, 0)),
            pl.BlockSpec((tm, LANES), lambda i: (i, 0)),
        ],
        out_shape=[
            jax.ShapeDtypeStruct((T, D_MODEL), BF16),
            jax.ShapeDtypeStruct((T, LANES), F32),
        ],
        scratch_shapes=[pltpu.VMEM((1, LANES), F32)],
        compiler_params=_cparams(("arbitrary",), 40),
        name="prenorm",
    )(x2, gain, wf_hi, wf_lo, bf)


def _mm_kernel(x_ref, w_ref, o_ref, *, sigmoid):
    acc = _dot(x_ref[...], w_ref[...])
    if sigmoid:
        acc = jax.nn.sigmoid(acc)
    o_ref[...] = acc.astype(o_ref.dtype)


def _matmul(x, w, out_dtype, sigmoid, name):
    M, K = x.shape
    N = w.shape[1]
    tm, tn = MM_TM, MM_TN
    return pl.pallas_call(
        functools.partial(_mm_kernel, sigmoid=sigmoid),
        grid=(M // tm, N // tn),
        in_specs=[
            pl.BlockSpec((tm, K), lambda i, j: (i, 0)),
            pl.BlockSpec((K, tn), lambda i, j: (0, j)),
        ],
        out_specs=pl.BlockSpec((tm, tn), lambda i, j: (i, j)),
        out_shape=jax.ShapeDtypeStruct((M, N), out_dtype),
        compiler_params=_cparams(("parallel", "parallel"), 48),
        name=name,
    )(x, w)


def _foxprep_kernel(q_ref, k_ref, v_ref, c_ref, gq_ref, gk_ref, bd_ref, selq_ref, selk_ref,
                    onesq_ref, onesk_ref, qa_ref, ka_ref, vt_ref):
    bd = bd_ref[...]

    def headnorm(x, g):
        hi, lo = _split2(x * x)
        ssq = _dot(hi, bd) + _dot(lo, bd)
        return x * lax.rsqrt(ssq * (1.0 / HEAD_DIM) + NORM_EPS) * g

    qn = headnorm(q_ref[...], gq_ref[...]) * (HEAD_DIM ** -0.5 * LOG2E)
    kn = headnorm(k_ref[...], gk_ref[...])
    v = v_ref[...]
    hi, mid, lo = _split3(c_ref[...] * LOG2E)
    c3 = jnp.concatenate([hi, mid, lo], axis=1)
    lane = lax.broadcasted_iota(jnp.int32, (qn.shape[0], LANES), 1)
    ones_col = jnp.where(lane == HEAD_DIM, 1.0, 0.0)
    for p in range(FOX_PAIRS):
        qp = qn[:, p * LANES:(p + 1) * LANES]
        vp = v[:, p * LANES:(p + 1) * LANES]
        for j in range(2):
            h = 2 * p + j
            keep = (lane < HEAD_DIM) if j == 0 else (lane >= HEAD_DIM)
            qa_ref[h, :, 0:LANES] = jnp.where(keep, qp, 0.0).astype(BF16)
            qa_ref[h, :, LANES:2 * LANES] = (_dot(c3, selq_ref[h]) + onesq_ref[h]).astype(BF16)
            vh = vp if j == 0 else pltpu.roll(vp, HEAD_DIM, axis=1)
            vt_ref[h, 0] = jnp.where(lane < HEAD_DIM, vh, ones_col).T.astype(BF16)
        ka_ref[p, :, 0:LANES] = kn[:, p * LANES:(p + 1) * LANES].astype(BF16)
        ka_ref[p, :, LANES:2 * LANES] = (_dot(c3, selk_ref[p]) + onesk_ref[...]).astype(BF16)


def _fox_tables():
    selq = np.zeros((FOX_HEADS, 3 * LANES, LANES), np.float32)
    selk = np.zeros((FOX_PAIRS, 3 * LANES, LANES), np.float32)
    onesq = np.zeros((FOX_HEADS, 1, LANES), np.float32)
    onesk = np.zeros((1, LANES), np.float32)
    for h in range(FOX_HEADS):
        p, j = divmod(h, 2)
        for piece in range(3):
            selq[h, piece * LANES + h, 6 * j + piece] = 1.0
            selk[p, piece * LANES + h, 6 * j + 3 + piece] = -1.0
            onesq[h, 0, 6 * j + 3 + piece] = 1.0
            onesk[0, 6 * j + piece] = 1.0
    bd = np.kron(np.eye(FOX_HEADS, dtype=np.float32), np.ones((HEAD_DIM, HEAD_DIM), np.float32))
    return (jnp.asarray(bd, BF16), jnp.asarray(selq, BF16), jnp.asarray(selk, BF16),
            jnp.asarray(onesq, F32), jnp.asarray(onesk, F32))


def _fox_prep(proj, c, gq, gk):
    T = proj.shape[0]
    tm = PREP_TM
    bd, selq, selk, onesq, onesk = _fox_tables()
    const2 = lambda i: (0, 0)
    const3 = lambda i: (0, 0, 0)
    return pl.pallas_call(
        _foxprep_kernel,
        grid=(T // tm,),
        in_specs=[
            pl.BlockSpec((tm, FOX_W), lambda i: (i, 0)),
            pl.BlockSpec((tm, FOX_W), lambda i: (i, 1)),
            pl.BlockSpec((tm, FOX_W), lambda i: (i, 2)),
            pl.BlockSpec((tm, LANES), lambda i: (i, 0)),
            pl.BlockSpec((1, FOX_W), const2),
            pl.BlockSpec((1, FOX_W), const2),
            pl.BlockSpec((FOX_W, FOX_W), const2),
            pl.BlockSpec((FOX_HEADS, 3 * LANES, LANES), const3),
            pl.BlockSpec((FOX_PAIRS, 3 * LANES, LANES), const3),
            pl.BlockSpec((FOX_HEADS, 1, LANES), const3),
            pl.BlockSpec((1, LANES), const2),
        ],
        out_specs=[
            pl.BlockSpec((FOX_HEADS, tm, 2 * LANES), lambda i: (0, i, 0)),
            pl.BlockSpec((FOX_PAIRS, tm, 2 * LANES), lambda i: (0, i, 0)),
            pl.BlockSpec((FOX_HEADS, 1, LANES, tm), lambda i: (0, i, 0, 0)),
        ],
        out_shape=[
            jax.ShapeDtypeStruct((FOX_HEADS, T, 2 * LANES), BF16),
            jax.ShapeDtypeStruct((FOX_PAIRS, T, 2 * LANES), BF16),
            jax.ShapeDtypeStruct((FOX_HEADS, T // tm, LANES, tm), BF16),
        ],
        compiler_params=_cparams(("parallel",), 48),
        name="fox_prep",
    )(proj, proj, proj, c, gq, gk, bd, selq, selk, onesq, onesk)


def _fox_attn_kernel(q_ref, k_ref, vt_ref, o_ref, m_ref, acc_ref, sa_ref, sb_ref):
    qi = pl.program_id(2)
    tk = FOX_TK
    m_ref[...] = jnp.full(m_ref.shape, NEG_BIG, F32)
    acc_ref[...] = jnp.zeros(acc_ref.shape, F32)

    def scores(kb, s_ref):
        off = pl.multiple_of(kb * tk, tk)
        k = k_ref[pl.ds(off, tk), :]
        for j in range(2):
            s_ref[j] = _dot_nt(k, q_ref[j])

    def softmax_pv(kb, s_ref, masked):
        for j in range(2):
            s = s_ref[j]
            if masked:
                key = lax.broadcasted_iota(jnp.int32, s.shape, 0)
                qry = lax.broadcasted_iota(jnp.int32, s.shape, 1)
                s = jnp.where(key <= qry, s, NEG_BIG)
            m_old = m_ref[j]
            m_new = jnp.maximum(m_old, jnp.max(s, axis=0, keepdims=True))
            alpha = jnp.exp2(m_old - m_new)
            p = jnp.exp2(s - m_new)
            acc_ref[j] = alpha * acc_ref[j] + _dot(vt_ref[j, kb], p.astype(BF16))
            m_ref[j] = m_new

    scores(0, sa_ref)

    def two_blocks(i, carry):
        scores(2 * i + 1, sb_ref)
        softmax_pv(2 * i, sa_ref, False)
        scores(2 * i + 2, sa_ref)
        softmax_pv(2 * i + 1, sb_ref, False)
        return carry

    lax.fori_loop(0, qi // 2, two_blocks, 0)

    @pl.when(qi % 2 == 0)
    def _():
        softmax_pv(qi, sa_ref, True)

    @pl.when(qi % 2 == 1)
    def _():
        scores(qi, sb_ref)
        softmax_pv(qi - 1, sa_ref, False)
        softmax_pv(qi, sb_ref, True)

    outs = [acc_ref[j, 0:HEAD_DIM, :] / acc_ref[j, HEAD_DIM:HEAD_DIM + 1, :] for j in range(2)]
    o_ref[...] = jnp.concatenate(outs, axis=0).T.astype(o_ref.dtype)


def _fox_attn(qa, ka, vt, batch, seq):
    T = qa.shape[1]
    tq, tk = FOX_TQ, FOX_TK
    assert tq == tk == PREP_TM
    nq = seq // tq
    nk = seq // tk
    qa4 = qa.reshape(FOX_PAIRS, 2, T, 2 * LANES)
    vt5 = vt.reshape(FOX_PAIRS, 2, batch * nk, LANES, tk)
    return pl.pallas_call(
        _fox_attn_kernel,
        grid=(batch, FOX_PAIRS, nq),
        in_specs=[
            pl.BlockSpec((None, 2, tq, 2 * LANES), lambda b, p, q: (p, 0, b * nq + q, 0)),
            pl.BlockSpec((None, seq, 2 * LANES), lambda b, p, q: (p, b, 0)),
            pl.BlockSpec((None, 2, nk, LANES, tk), lambda b, p, q: (p, 0, b, 0, 0)),
        ],
        out_specs=pl.BlockSpec((tq, LANES), lambda b, p, q: (b * nq + q, p)),
        out_shape=jax.ShapeDtypeStruct((T, FOX_W), BF16),
        scratch_shapes=[
            pltpu.VMEM((2, 1, tq), F32),
            pltpu.VMEM((2, LANES, tq), F32),
            pltpu.VMEM((2, tk, tq), F32),
            pltpu.VMEM((2, tk, tq), F32),
        ],
        compiler_params=_cparams(("parallel", "parallel", "arbitrary"), 48),
        name="fox_attn",
    )(qa4, ka, vt5)


SWA_HEAD_ORDER = (0, 3, 1, 4, 2, 5, 6, 9, 7, 10, 8, 11)
SWA_KV_TILES = SWA_KV_HEADS // 2
SWA_Q_TILES_PER_KV_TILE = SWA_HEADS // 2 // SWA_KV_TILES


def _swa_kernel(sinks_ref, q_ref, k_ref, v_ref, pos_ref, freq_ref, sign_ref, gq_ref, gk_ref, bdq_ref, bdk_ref,
                o_ref, kprev_ref, vtprev_ref):
    n = pl.program_id(1)
    W = WINDOW
    half = HEAD_DIM // 2

    @pl.when(n == 0)
    def _():
        kprev_ref[...] = jnp.zeros_like(kprev_ref)
        vtprev_ref[...] = jnp.zeros_like(vtprev_ref)

    lane = lax.broadcasted_iota(jnp.int32, (W, LANES), 1)
    ang = pos_ref[...].astype(F32) * freq_ref[...]
    cos1 = jnp.cos(ang)
    sin1 = jnp.sin(ang) * sign_ref[...]
    first_half1 = (lane & half) == 0

    def norm_rope(x, g, bd):
        reps = x.shape[1] // LANES
        hi, lo = _split2(x * x)
        ssq = _dot(hi, bd) + _dot(lo, bd)
        xn = x * lax.rsqrt(ssq * (1.0 / HEAD_DIM) + NORM_EPS) * g
        first_half = jnp.tile(first_half1, (1, reps))
        swapped = jnp.where(first_half, pltpu.roll(xn, x.shape[1] - half, axis=1), pltpu.roll(xn, half, axis=1))
        return xn * jnp.tile(cos1, (1, reps)) + swapped * jnp.tile(sin1, (1, reps))

    q = norm_rope(q_ref[...], gq_ref[...], bdq_ref[...]) * (HEAD_DIM ** -0.5 * LOG2E)
    k = norm_rope(k_ref[...], gk_ref[...], bdk_ref[...]).astype(BF16)
    v = v_ref[...]

    key = lax.broadcasted_iota(jnp.int32, (2 * W, 2 * W), 0)
    qry = lax.broadcasted_iota(jnp.int32, (2 * W, 2 * W), 1) & (W - 1)
    visible = ((key < W) & (key > qry) & (n > 0)) | ((key >= W) & (key - W <= qry))
    low_q = lax.broadcasted_iota(jnp.int32, (1, 2 * W), 1) < W

    for kt in range(SWA_KV_TILES):
        k_cur = k[:, kt * LANES:(kt + 1) * LANES]
        vt_cur = v[:, kt * LANES:(kt + 1) * LANES].T.astype(BF16)
        keys = jnp.concatenate([kprev_ref[kt], k_cur], axis=0)
        vt = jnp.concatenate([vtprev_ref[kt], vt_cur], axis=1)
        for r in range(SWA_Q_TILES_PER_KV_TILE):
            t = kt * SWA_Q_TILES_PER_KV_TILE + r
            qt = q[:, t * LANES:(t + 1) * LANES]
            qs = jnp.concatenate([jnp.where(lane < HEAD_DIM, qt, 0.0), jnp.where(lane >= HEAD_DIM, qt, 0.0)],
                                 axis=0).astype(BF16)
            s = jnp.where(visible, _dot_nt(keys, qs), NEG_BIG)
            sink = jnp.where(low_q, sinks_ref[2 * t], sinks_ref[2 * t + 1]) * LOG2E
            m = jnp.maximum(jnp.max(s, axis=0, keepdims=True), sink)
            p = jnp.exp2(s - m)
            denom = jnp.sum(p, axis=0, keepdims=True) + jnp.exp2(sink - m)
            ot = _dot(vt, p.astype(BF16)) / denom
            pair = jnp.concatenate([ot[0:HEAD_DIM, 0:W], ot[HEAD_DIM:LANES, W:2 * W]], axis=0)
            o_ref[:, t * LANES:(t + 1) * LANES] = pair.T.astype(o_ref.dtype)
        kprev_ref[kt] = k_cur
        vtprev_ref[kt] = vt_cur


def _swa_attn(proj, pos2, sinks, gq, gk, batch, seq):
    T = proj.shape[0]
    W = WINDOW
    assert W == LANES
    nb = seq // W
    half = HEAD_DIM // 2
    inv_freq = np.power(ROPE_THETA, -np.arange(0, HEAD_DIM, 2, dtype=np.float32) / HEAD_DIM).astype(np.float32)
    freq = np.tile(inv_freq, LANES // half).reshape(1, LANES)
    sign = np.tile(np.concatenate([-np.ones(half, np.float32), np.ones(half, np.float32)]),
                   LANES // HEAD_DIM).reshape(1, LANES)
    head_ones = np.ones((HEAD_DIM, HEAD_DIM), np.float32)
    bdq = jnp.asarray(np.kron(np.eye(SWA_HEADS, dtype=np.float32), head_ones), BF16)
    bdk = jnp.asarray(np.kron(np.eye(SWA_KV_HEADS, dtype=np.float32), head_ones), BF16)
    q_col = (3 * FOX_W) // SWA_QW
    k_col = (3 * FOX_W + SWA_QW) // SWA_KVW
    const2 = lambda b, n, s: (0, 0)
    grid_spec = pltpu.PrefetchScalarGridSpec(
        num_scalar_prefetch=1,
        grid=(batch, nb),
        in_specs=[
            pl.BlockSpec((W, SWA_QW), lambda b, n, s: (b * nb + n, q_col)),
            pl.BlockSpec((W, SWA_KVW), lambda b, n, s: (b * nb + n, k_col)),
            pl.BlockSpec((W, SWA_KVW), lambda b, n, s: (b * nb + n, k_col + 1)),
            pl.BlockSpec((W, 1), lambda b, n, s: (b * nb + n, 0)),
            pl.BlockSpec((1, LANES), const2),
            pl.BlockSpec((1, LANES), const2),
            pl.BlockSpec((1, SWA_QW), const2),
            pl.BlockSpec((1, SWA_KVW), const2),
            pl.BlockSpec((SWA_QW, SWA_QW), const2),
            pl.BlockSpec((SWA_KVW, SWA_KVW), const2),
        ],
        out_specs=pl.BlockSpec((W, SWA_QW), lambda b, n, s: (b * nb + n, 0)),
        scratch_shapes=[
            pltpu.VMEM((SWA_KV_TILES, W, LANES), BF16),
            pltpu.VMEM((SWA_KV_TILES, LANES, W), BF16),
        ],
    )
    return pl.pallas_call(
        _swa_kernel,
        grid_spec=grid_spec,
        out_shape=jax.ShapeDtypeStruct((T, SWA_QW), BF16),
        compiler_params=_cparams(("parallel", "arbitrary"), 32),
        name="swa_attn",
    )(sinks, proj, proj, proj, pos2, jnp.asarray(freq), jnp.asarray(sign), gq, gk, bdq, bdk)


def _memkv_kernel(mem_ref, g_ref, w_ref, gk_ref, k_ref, v_ref):
    x = mem_ref[...]
    ms = jnp.mean(x * x, axis=-1, keepdims=True)
    mn = (x * lax.rsqrt(ms + NORM_EPS) * g_ref[...]).astype(BF16)
    kv = _dot(mn, w_ref[...])
    for h in range(XMEM_HEADS):
        kh = kv[:, h * LANES:(h + 1) * LANES]
        ms_h = jnp.mean(kh * kh, axis=-1, keepdims=True)
        k_ref[:, h * LANES:(h + 1) * LANES] = (kh * lax.rsqrt(ms_h + NORM_EPS) * gk_ref[...]).astype(BF16)
    v_ref[...] = kv[:, XMEM_W:].astype(BF16)


def _mem_kv(mem2, gain, w, gk):
    R = mem2.shape[0]
    tm = 256
    return pl.pallas_call(
        _memkv_kernel,
        grid=(R // tm,),
        in_specs=[
            pl.BlockSpec((tm, D_MODEL), lambda i: (i, 0)),
            pl.BlockSpec((1, D_MODEL), lambda i: (0, 0)),
            pl.BlockSpec((D_MODEL, 2 * XMEM_W), lambda i: (0, 0)),
            pl.BlockSpec((1, XMEM_HEAD_DIM), lambda i: (0, 0)),
        ],
        out_specs=[
            pl.BlockSpec((tm, XMEM_W), lambda i: (i, 0)),
            pl.BlockSpec((tm, XMEM_W), lambda i: (i, 0)),
        ],
        out_shape=[
            jax.ShapeDtypeStruct((R, XMEM_W), BF16),
            jax.ShapeDtypeStruct((R, XMEM_W), BF16),
        ],
        compiler_params=_cparams(("parallel",), 32),
        name="mem_kv",
    )(mem2, gain, w, gk)


def _memattn_kernel(q_ref, k_ref, v_ref, gq_ref, o_ref):
    q = q_ref[...]
    for h in range(XMEM_HEADS):
        sl = slice(h * LANES, (h + 1) * LANES)
        qh = q[:, sl]
        ms = jnp.mean(qh * qh, axis=-1, keepdims=True)
        qn = (qh * lax.rsqrt(ms + NORM_EPS) * gq_ref[...] * (XMEM_HEAD_DIM ** -0.5)).astype(BF16)
        s = _dot_nt(qn, k_ref[:, sl])
        m = jnp.max(s, axis=-1, keepdims=True)
        p = jnp.exp(s - m)
        l = jnp.sum(p, axis=-1, keepdims=True)
        o_ref[:, sl] = (_dot(p.astype(BF16), v_ref[:, sl]) / l).astype(o_ref.dtype)


def _mem_attn(proj, mk, mv, gq, seq, n_mem):
    T = proj.shape[0]
    tq = MEM_TQ
    per_seq = seq // tq
    q_col = (3 * FOX_W + SWA_QW + 2 * SWA_KVW) // XMEM_W
    return pl.pallas_call(
        _memattn_kernel,
        grid=(T // tq,),
        in_specs=[
            pl.BlockSpec((tq, XMEM_W), lambda i: (i, q_col)),
            pl.BlockSpec((n_mem, XMEM_W), lambda i: (i // per_seq, 0)),
            pl.BlockSpec((n_mem, XMEM_W), lambda i: (i // per_seq, 0)),
            pl.BlockSpec((1, XMEM_HEAD_DIM), lambda i: (0, 0)),
        ],
        out_specs=pl.BlockSpec((tq, XMEM_W), lambda i: (i, 0)),
        out_shape=jax.ShapeDtypeStruct((T, XMEM_W), BF16),
        compiler_params=_cparams(("parallel",), 32),
        name="mem_attn",
    )(proj, mk, mv, gq)


def _merge_kernel(of_ref, os_ref, ox_ref, g0_ref, g1_ref, g2_ref, x_ref, wf_ref, ws_ref, wx_ref, wo_ref,
                  gn_ref, wr_hi_ref, wr_lo_ref, br_ref,
                  h_ref, hn_ref, ri_ref, rg_ref, cnt_ref, carry_ref):
    i = pl.program_id(0)

    @pl.when(i == 0)
    def _():
        carry_ref[...] = jnp.zeros_like(carry_ref)

    merged = (g0_ref[...].astype(F32) * _dot(of_ref[...], wf_ref[...])
              + g1_ref[...].astype(F32) * _dot(os_ref[...], ws_ref[...])
              + g2_ref[...].astype(F32) * _dot(ox_ref[...], wx_ref[...]))
    h = x_ref[...] + _dot(merged.astype(BF16), wo_ref[...])
    h_ref[...] = h
    ms = jnp.mean(h * h, axis=-1, keepdims=True)
    hn = h * lax.rsqrt(ms + NORM_EPS) * gn_ref[...]
    hn_hi, hn_lo = _split2(hn)
    hn_ref[...] = hn
    wr_hi = wr_hi_ref[...]
    logits = _dot(hn_hi, wr_hi) + _dot(hn_lo, wr_hi) + _dot(hn_hi, wr_lo_ref[...]) + br_ref[...]
    tm = logits.shape[0]
    lane = lax.broadcasted_iota(jnp.int32, (tm, LANES), 1).astype(F32)
    work = jnp.where(lane < N_EXPERTS, logits, NEG_BIG)
    vals, idxs = [], []
    for _ in range(TOP_K):
        mx = jnp.max(work, axis=-1, keepdims=True)
        ix = jnp.min(jnp.where(work == mx, lane, float(LANES)), axis=-1, keepdims=True)
        vals.append(mx)
        idxs.append(ix)
        work = jnp.where(lane == ix, NEG_BIG, work)
    es = [jnp.exp(v - vals[0]) for v in vals]
    den = es[0] + es[1] + es[2] + es[3]
    onehot = jnp.zeros((tm, LANES), F32)
    for ix in idxs:
        onehot = onehot + jnp.where(lane == ix, 1.0, 0.0)
    r = lax.broadcasted_iota(jnp.int32, (tm, tm), 0)
    cc = lax.broadcasted_iota(jnp.int32, (tm, tm), 1)
    tri = jnp.where(cc < r, 1.0, 0.0).astype(BF16)
    before = _dot(tri, onehot.astype(BF16)) + carry_ref[...]
    ri = jnp.zeros((tm, LANES), jnp.int32)
    rg = jnp.zeros((tm, LANES), F32)
    for k in range(TOP_K):
        rank = jnp.sum(jnp.where(lane == idxs[k], before, 0.0), axis=-1, keepdims=True)
        ri = jnp.where(lane == k, idxs[k].astype(jnp.int32), ri)
        ri = jnp.where(lane == TOP_K + k, rank.astype(jnp.int32), ri)
        rg = jnp.where(lane == k, es[k] / den, rg)
    ri_ref[...] = ri
    rg_ref[...] = rg
    total = carry_ref[...] + jnp.sum(onehot, axis=0, keepdims=True)
    carry_ref[...] = total
    cnt_ref[...] = total


def _merge(o_fox, o_swa, o_x, gates, x2, wf, ws, wx, wo, gn, wr_hi, wr_lo, br):
    T = x2.shape[0]
    tm = MERGE_TM
    row = lambda i: (i, 0)
    const = lambda i: (0, 0)
    resident = functools.partial(pl.BlockSpec, index_map=const, pipeline_mode=pl.Buffered(1))
    return pl.pallas_call(
        _merge_kernel,
        grid=(T // tm,),
        in_specs=[
            pl.BlockSpec((tm, FOX_W), row),
            pl.BlockSpec((tm, SWA_QW), row),
            pl.BlockSpec((tm, XMEM_W), row),
            pl.BlockSpec((tm, D_MODEL), lambda i: (i, 0)),
            pl.BlockSpec((tm, D_MODEL), lambda i: (i, 1)),
            pl.BlockSpec((tm, D_MODEL), lambda i: (i, 2)),
            pl.BlockSpec((tm, D_MODEL), row),
            resident((FOX_W, D_MODEL)),
            resident((SWA_QW, D_MODEL)),
            resident((XMEM_W, D_MODEL)),
            resident((D_MODEL, D_MODEL)),
            resident((1, D_MODEL)),
            resident((D_MODEL, LANES)),
            resident((D_MODEL, LANES)),
            resident((1, LANES)),
        ],
        out_specs=[
            pl.BlockSpec((tm, D_MODEL), row),
            pl.BlockSpec((tm, D_MODEL), row),
            pl.BlockSpec((tm, LANES), row),
            pl.BlockSpec((tm, LANES), row),
            pl.BlockSpec((1, LANES), const),
        ],
        out_shape=[
            jax.ShapeDtypeStruct((T, D_MODEL), F32),
            jax.ShapeDtypeStruct((T, D_MODEL), F32),
            jax.ShapeDtypeStruct((T, LANES), jnp.int32),
            jax.ShapeDtypeStruct((T, LANES), F32),
            jax.ShapeDtypeStruct((1, LANES), F32),
        ],
        scratch_shapes=[pltpu.VMEM((1, LANES), F32)],
        compiler_params=_cparams(("arbitrary",), 56),
        name="merge_router",
    )(o_fox, o_swa, o_x, gates, gates, gates, x2, wf, ws, wx, wo, gn, wr_hi, wr_lo, br)


def _dispatch_kernel(start_ref, cnt_ref, e_ref, r_ref, hn_ref, xs_ref, zero_ref, sem):
    i = pl.program_id(0)
    tb = DISPATCH_TB

    def row_copy(src_ref, t, slot):
        return pltpu.make_async_copy(src_ref.at[pl.ds(t, 1), :], xs_ref.at[pl.ds(slot, 1), :], sem)

    @pl.when(i == 0)
    def _():
        zero_ref[...] = jnp.zeros_like(zero_ref)

        def per_expert(e, carry):
            base = start_ref[e]
            n = cnt_ref[e]
            end = ((n + MOE_BM - 1) // MOE_BM) * MOE_BM

            def fill(r, c):
                row_copy(zero_ref, 0, base + r).start()
                return c

            def drain(r, c):
                row_copy(zero_ref, 0, base + r).wait()
                return c

            lax.fori_loop(n, end, fill, 0)
            lax.fori_loop(n, end, drain, 0)
            return carry

        lax.fori_loop(0, N_EXPERTS, per_expert, 0)

        last = N_EXPERTS - 1
        used = start_ref[last] + ((cnt_ref[last] + MOE_BM - 1) // MOE_BM) * MOE_BM
        zr = zero_ref.shape[0]

        def tail_copy(r):
            return pltpu.make_async_copy(zero_ref, xs_ref.at[pl.ds(pl.multiple_of(used + r * zr, zr), zr), :], sem)

        def tail_fill(r, c):
            tail_copy(r).start()
            return c

        def tail_drain(r, c):
            tail_copy(r).wait()
            return c

        n_tail = (xs_ref.shape[0] - used) // zr
        lax.fori_loop(0, n_tail, tail_fill, 0)
        lax.fori_loop(0, n_tail, tail_drain, 0)

    def token_copy(t, k):
        a = t * TOP_K + k
        return row_copy(hn_ref, t, start_ref[e_ref[a]] + r_ref[a])

    def issue(t, carry):
        for k in range(TOP_K):
            token_copy(t, k).start()
        return carry

    def drain(t, carry):
        for k in range(TOP_K):
            token_copy(t, k).wait()
        return carry

    lax.fori_loop(0, tb, issue, 0)
    lax.fori_loop(0, tb, drain, 0)


def _dispatch(pad_start, cnt, e_flat, r_flat, hn, n_slots):
    T, D = hn.shape
    tb = DISPATCH_TB
    grid_spec = pltpu.PrefetchScalarGridSpec(
        num_scalar_prefetch=2,
        grid=(T // tb,),
        in_specs=[
            pl.BlockSpec((tb * TOP_K,), lambda i, s, c: (i,), memory_space=pltpu.SMEM),
            pl.BlockSpec((tb * TOP_K,), lambda i, s, c: (i,), memory_space=pltpu.SMEM),
            pl.BlockSpec((tb, D), lambda i, s, c: (i, 0)),
        ],
        out_specs=pl.BlockSpec(memory_space=pl.ANY),
        scratch_shapes=[pltpu.VMEM((MOE_BM // 2, D), F32), pltpu.SemaphoreType.DMA(())],
    )
    return pl.pallas_call(
        _dispatch_kernel,
        grid_spec=grid_spec,
        out_shape=jax.ShapeDtypeStruct((n_slots, D), F32),
        compiler_params=_cparams(("arbitrary",), 32),
        name="dispatch",
    )(pad_start, cnt, e_flat, r_flat, hn)


def _expert_changed(be_ref, i):
    return (i == 0) | (be_ref[i] != be_ref[jnp.maximum(i - 1, 0)])


def _gateup_kernel(be_ref, nu_ref, xs_ref, wg_ref, wu_ref, bg_ref, bu_ref, o_ref, wg_s, wu_s):
    i = pl.program_id(1)

    @pl.when(i < nu_ref[0])
    def _():
        @pl.when(_expert_changed(be_ref, i))
        def _():
            wg_s[...] = wg_ref[0].astype(BF16)
            wu_s[...] = wu_ref[0].astype(BF16)

        x = xs_ref[...].astype(BF16)
        gate = _dot(x, wg_s[...]) + bg_ref[0]
        up = _dot(x, wu_s[...]) + bu_ref[0]
        gate = jnp.minimum(gate, SWIGLU_LIMIT)
        up = jnp.clip(up, -SWIGLU_LIMIT, SWIGLU_LIMIT)
        glu = gate * jax.nn.sigmoid(gate * SWIGLU_ALPHA)
        o_ref[...] = ((up + 1.0) * glu).astype(o_ref.dtype)

    @pl.when(i >= nu_ref[0])
    def _():
        o_ref[...] = jnp.zeros_like(o_ref)


def _down_kernel(be_ref, nu_ref, h_ref, w_ref, b_ref, o_ref, w_s):
    i = pl.program_id(1)

    @pl.when(i < nu_ref[0])
    def _():
        @pl.when(_expert_changed(be_ref, i))
        def _():
            w_s[...] = w_ref[0].astype(BF16)

        o_ref[...] = _dot(h_ref[...], w_s[...]) + b_ref[0]

    @pl.when(i >= nu_ref[0])
    def _():
        o_ref[...] = jnp.zeros_like(o_ref)


def _experts(blk_expert, n_used, xs, w_gate_up, b_gate_up, w_down, b_down):
    P = xs.shape[0]
    bm, tn = MOE_BM, MOE_TN
    n_blk = P // bm
    d_exp = w_down.shape[1]
    up_off = d_exp // tn

    def blk(i, nu):
        return jnp.minimum(i, nu[0] - 1)

    gateup_spec = pltpu.PrefetchScalarGridSpec(
        num_scalar_prefetch=2,
        grid=(d_exp // tn, n_blk),
        in_specs=[
            pl.BlockSpec((bm, D_MODEL), lambda j, i, be, nu: (blk(i, nu), 0)),
            pl.BlockSpec((1, D_MODEL, tn), lambda j, i, be, nu: (be[blk(i, nu)], 0, j)),
            pl.BlockSpec((1, D_MODEL, tn), lambda j, i, be, nu: (be[blk(i, nu)], 0, up_off + j)),
            pl.BlockSpec((1, 1, tn), lambda j, i, be, nu: (be[blk(i, nu)], 0, j)),
            pl.BlockSpec((1, 1, tn), lambda j, i, be, nu: (be[blk(i, nu)], 0, up_off + j)),
        ],
        out_specs=pl.BlockSpec((bm, tn), lambda j, i, be, nu: (i, j)),
        scratch_shapes=[pltpu.VMEM((D_MODEL, tn), BF16), pltpu.VMEM((D_MODEL, tn), BF16)],
    )
    hmid = pl.pallas_call(
        _gateup_kernel,
        grid_spec=gateup_spec,
        out_shape=jax.ShapeDtypeStruct((P, d_exp), BF16),
        compiler_params=_cparams(("arbitrary", "arbitrary"), 56),
        name="expert_gate_up",
    )(blk_expert, n_used, xs, w_gate_up, w_gate_up, b_gate_up, b_gate_up)

    down_spec = pltpu.PrefetchScalarGridSpec(
        num_scalar_prefetch=2,
        grid=(D_MODEL // tn, n_blk),
        in_specs=[
            pl.BlockSpec((bm, d_exp), lambda j, i, be, nu: (blk(i, nu), 0)),
            pl.BlockSpec((1, d_exp, tn), lambda j, i, be, nu: (be[blk(i, nu)], 0, j)),
            pl.BlockSpec((1, 1, tn), lambda j, i, be, nu: (be[blk(i, nu)], 0, j)),
        ],
        out_specs=pl.BlockSpec((bm, tn), lambda j, i, be, nu: (i, j)),
        scratch_shapes=[pltpu.VMEM((d_exp, tn), BF16)],
    )
    return pl.pallas_call(
        _down_kernel,
        grid_spec=down_spec,
        out_shape=jax.ShapeDtypeStruct((P, D_MODEL), F32),
        compiler_params=_cparams(("arbitrary", "arbitrary"), 48),
        name="expert_down",
    )(blk_expert, n_used, hmid, w_down, b_down)


def _combine_kernel(start_ref, e_ref, r_ref, h_ref, g_ref, ys_ref, o_ref, buf, sem):
    tb = COMBINE_TB

    def row_copy(t, k):
        a = t * TOP_K + k
        slot = start_ref[e_ref[a]] + r_ref[a]
        return pltpu.make_async_copy(ys_ref.at[pl.ds(slot, 1), :], buf.at[k, pl.ds(t, 1), :], sem)

    def issue(t, carry):
        for k in range(TOP_K):
            row_copy(t, k).start()
        return carry

    def drain(t, carry):
        for k in range(TOP_K):
            row_copy(t, k).wait()
        return carry

    lax.fori_loop(0, tb, issue, 0)
    lax.fori_loop(0, tb, drain, 0)
    g = g_ref[...]
    acc = h_ref[...]
    for k in range(TOP_K):
        acc = acc + g[:, k:k + 1] * buf[k]
    o_ref[...] = acc


def _combine(pad_start, e_flat, r_flat, h1, gates, ys):
    T = h1.shape[0]
    tb = COMBINE_TB
    grid_spec = pltpu.PrefetchScalarGridSpec(
        num_scalar_prefetch=1,
        grid=(T // tb,),
        in_specs=[
            pl.BlockSpec((tb * TOP_K,), lambda i, s: (i,), memory_space=pltpu.SMEM),
            pl.BlockSpec((tb * TOP_K,), lambda i, s: (i,), memory_space=pltpu.SMEM),
            pl.BlockSpec((tb, D_MODEL), lambda i, s: (i, 0)),
            pl.BlockSpec((tb, LANES), lambda i, s: (i, 0)),
            pl.BlockSpec(memory_space=pl.ANY),
        ],
        out_specs=pl.BlockSpec((tb, D_MODEL), lambda i, s: (i, 0)),
        scratch_shapes=[pltpu.VMEM((TOP_K, tb, D_MODEL), F32), pltpu.SemaphoreType.DMA(())],
    )
    return pl.pallas_call(
        _combine_kernel,
        grid_spec=grid_spec,
        out_shape=jax.ShapeDtypeStruct((T, D_MODEL), F32),
        compiler_params=_cparams(("arbitrary",), 32),
        name="combine",
    )(pad_start, e_flat, r_flat, h1, gates, ys)


def _pad_lanes(a, width=LANES):
    return jnp.pad(a, ((0, 0), (0, width - a.shape[1])))


def _layer(h, mem, positions, norm_mix, w_in, b_forget, fox_q_norm, fox_k_norm, swa_q_norm, swa_k_norm,
           swa_sinks, xmem_q_norm, xmem_k_norm, norm_mem, w_mem_kv, w_up_fox, w_up_swa, w_up_xmem, w_out,
           norm_ffn, w_router, b_router, w_gate_up, b_gate_up, w_down, b_down):
    B, S, D = h.shape
    M = mem.shape[1]
    T = B * S
    x2 = h.reshape(T, D)

    attn_end = 3 * FOX_W + FOX_HEADS
    gate_start = attn_end + SWA_QW + 2 * SWA_KVW + XMEM_W
    head_order = np.asarray(SWA_HEAD_ORDER)
    w_sq = w_in[:, attn_end:attn_end + SWA_QW].reshape(D, SWA_HEADS, HEAD_DIM)[:, head_order].reshape(D, SWA_QW)
    w_attn = jnp.concatenate([w_in[:, :3 * FOX_W], w_sq, w_in[:, attn_end + SWA_QW:gate_start]],
                             axis=1).astype(BF16)
    w_up_swa = w_up_swa.reshape(SWA_HEADS, HEAD_DIM, D)[head_order].reshape(SWA_QW, D)
    swa_sinks = swa_sinks[head_order]
    w_gates = w_in[:, gate_start:].astype(BF16)
    wf = _pad_lanes(w_in[:, 3 * FOX_W:attn_end])
    wf_hi, wf_lo = _split2(wf)
    bf = _pad_lanes(b_forget.reshape(1, FOX_HEADS))

    def tile_gain(g, reps):
        return jnp.tile(g.reshape(1, -1), (1, reps))

    xn, c = _prenorm(x2, norm_mix.reshape(1, D), wf_hi, wf_lo, bf, S)
    proj = _matmul(xn, w_attn, F32, False, "in_proj_attn")
    gates = _matmul(xn, w_gates, BF16, True, "in_proj_gates")

    qa, ka, vb = _fox_prep(proj, c, tile_gain(fox_q_norm, FOX_HEADS), tile_gain(fox_k_norm, FOX_HEADS))
    o_fox = _fox_attn(qa, ka, vb, B, S)

    o_swa = _swa_attn(proj, positions.reshape(T, 1), swa_sinks.astype(F32),
                      tile_gain(swa_q_norm, SWA_HEADS), tile_gain(swa_k_norm, SWA_KV_HEADS), B, S)

    mk, mv = _mem_kv(mem.reshape(B * M, D), norm_mem.reshape(1, D), w_mem_kv.astype(BF16),
                     xmem_k_norm.reshape(1, XMEM_HEAD_DIM))
    o_x = _mem_attn(proj, mk, mv, xmem_q_norm.reshape(1, XMEM_HEAD_DIM), S, M)

    wr = _pad_lanes(w_router)
    wr_hi, wr_lo = _split2(wr)
    h1, hn, route_i, route_g, counts = _merge(
        o_fox, o_swa, o_x, gates, x2, w_up_fox.astype(BF16), w_up_swa.astype(BF16), w_up_xmem.astype(BF16),
        w_out.astype(BF16), norm_ffn.reshape(1, D), wr_hi, wr_lo, _pad_lanes(b_router.reshape(1, N_EXPERTS)))

    bm = MOE_BM
    cnt = counts[0, :N_EXPERTS].astype(jnp.int32)
    padded = ((cnt + bm - 1) // bm) * bm
    pad_end = jnp.cumsum(padded)
    pad_start = (pad_end - padded).astype(jnp.int32)
    P = T * TOP_K + N_EXPERTS * bm
    n_blk = P // bm
    blk_first = jnp.arange(n_blk, dtype=jnp.int32) * bm
    blk_expert = jnp.minimum(jnp.sum((pad_end[None, :] <= blk_first[:, None]).astype(jnp.int32), axis=1),
                             N_EXPERTS - 1)
    n_used = (pad_end[-1:] // bm).astype(jnp.int32)
    e_flat = route_i[:, :TOP_K].reshape(T * TOP_K)
    r_flat = route_i[:, TOP_K:2 * TOP_K].reshape(T * TOP_K)

    xs = _dispatch(pad_start, cnt, e_flat, r_flat, hn, P)
    ys = _experts(blk_expert, n_used, xs, w_gate_up, b_gate_up.reshape(N_EXPERTS, 1, -1),
                  w_down, b_down.reshape(N_EXPERTS, 1, -1))
    out = _combine(pad_start, e_flat, r_flat, h1, route_g, ys)
    return out.reshape(B, S, D)


def kernel(x, mem, positions, norm_mix, w_in, b_forget, fox_q_norm, fox_k_norm, swa_q_norm, swa_k_norm, swa_sinks, xmem_q_norm, xmem_k_norm, norm_mem, w_mem_kv, w_up_fox, w_up_swa, w_up_xmem, w_out, norm_ffn, w_router, b_router, w_gate_up, b_gate_up, w_down, b_down):
    h = x
    for layer in range(norm_mix.shape[0]):
        h = _layer(
            h, mem, positions, norm_mix[layer], w_in[layer], b_forget[layer],
            fox_q_norm[layer], fox_k_norm[layer], swa_q_norm[layer], swa_k_norm[layer],
            swa_sinks[layer], xmem_q_norm[layer], xmem_k_norm[layer], norm_mem[layer],
            w_mem_kv[layer], w_up_fox[layer], w_up_swa[layer], w_up_xmem[layer],
            w_out[layer], norm_ffn[layer], w_router[layer], b_router[layer],
            w_gate_up[layer], b_gate_up[layer], w_down[layer], b_down[layer])
    return h
```

```python
import functools

import numpy as np
import jax
import jax.numpy as jnp
from jax import lax
from jax.experimental import pallas as pl
from jax.experimental.pallas import tpu as pltpu

F32 = jnp.float32
BF16 = jnp.bfloat16

D_MODEL = 2048
HEAD_DIM = 64
FOX_HEADS = 12
SWA_HEADS = 12
SWA_KV_HEADS = 4
SWA_GROUP = SWA_HEADS // SWA_KV_HEADS
WINDOW = 128
XMEM_HEADS = 4
XMEM_HEAD_DIM = 128
N_EXPERTS = 32
TOP_K = 4
SWIGLU_LIMIT = 7.0
SWIGLU_ALPHA = 1.702
ROPE_THETA = 10000.0
NORM_EPS = 1e-6

FOX_W = FOX_HEADS * HEAD_DIM
SWA_QW = SWA_HEADS * HEAD_DIM
SWA_KVW = SWA_KV_HEADS * HEAD_DIM
XMEM_W = XMEM_HEADS * XMEM_HEAD_DIM
FOX_PAIRS = FOX_HEADS // 2

LANES = 128
NEG_BIG = -1e30
LOG2E = 1.4426950408889634
MIB = 1024 * 1024

PRE_TM = 512
MM_TM = 1024
MM_TN = 1024
PREP_TM = 512
FOX_TQ = 512
FOX_TK = 512
MEM_TQ = 512
MERGE_TM = 256
MOE_BM = 512
MOE_TN = 1024
DISPATCH_TB = 256
COMBINE_TB = 128


def _cparams(semantics, vmem_mib):
    return pltpu.CompilerParams(dimension_semantics=semantics, vmem_limit_bytes=vmem_mib * MIB)


def _dot(a, b):
    return jnp.dot(a, b, preferred_element_type=F32)


def _dot_nt(a, b):
    return lax.dot_general(a, b, (((1,), (1,)), ((), ())), preferred_element_type=F32)


def _split2(x):
    hi = x.astype(BF16)
    lo = (x - hi.astype(F32)).astype(BF16)
    return hi, lo


def _split3(x):
    hi = x.astype(BF16)
    r = x - hi.astype(F32)
    mid = r.astype(BF16)
    lo = (r - mid.astype(F32)).astype(BF16)
    return hi, mid, lo


def _prenorm_kernel(x_ref, g_ref, wf_hi_ref, wf_lo_ref, bf_ref, xn_ref, c_ref, carry_ref, *, blocks_per_seq):
    i = pl.program_id(0)

    @pl.when(i % blocks_per_seq == 0)
    def _():
        carry_ref[...] = jnp.zeros_like(carry_ref)

    x = x_ref[...]
    tm = x.shape[0]
    ms = jnp.mean(x * x, axis=-1, keepdims=True)
    xn = x * lax.rsqrt(ms + NORM_EPS) * g_ref[...]
    xn_hi, xn_lo = _split2(xn)
    xn_ref[...] = xn_hi
    wf_hi = wf_hi_ref[...]
    z = _dot(xn_hi, wf_hi) + _dot(xn_lo, wf_hi) + _dot(xn_hi, wf_lo_ref[...]) + bf_ref[...]
    logf = jnp.minimum(z, 0.0) - jnp.log1p(jnp.exp(-jnp.abs(z)))
    lane = lax.broadcasted_iota(jnp.int32, logf.shape, 1)
    logf = jnp.where(lane < FOX_HEADS, logf, 0.0)
    hi, mid, lo = _split3(logf)
    r = lax.broadcasted_iota(jnp.int32, (tm, tm), 0)
    cc = lax.broadcasted_iota(jnp.int32, (tm, tm), 1)
    tri = jnp.where(cc <= r, 1.0, 0.0).astype(BF16)
    c = _dot(tri, hi) + _dot(tri, mid) + _dot(tri, lo) + carry_ref[...]
    c_ref[...] = c
    carry_ref[...] = c[tm - 1:tm, :]


def _prenorm(x2, gain, wf_hi, wf_lo, bf, seq):
    T = x2.shape[0]
    tm = PRE_TM
    return pl.pallas_call(
        functools.partial(_prenorm_kernel, blocks_per_seq=seq // tm),
        grid=(T // tm,),
        in_specs=[
            pl.BlockSpec((tm, D_MODEL), lambda i: (i, 0)),
            pl.BlockSpec((1, D_MODEL), lambda i: (0, 0)),
            pl.BlockSpec((D_MODEL, LANES), lambda i: (0, 0)),
            pl.BlockSpec((D_MODEL, LANES), lambda i: (0, 0)),
            pl.BlockSpec((1, LANES), lambda i: (0, 0)),
        ],
        out_specs=[
            pl.BlockSpec((tm, D_MODEL), lambda i: (i, 0)),
            pl.BlockSpec((tm, LANES), lambda i: (i, 0)),
        ],
        out_shape=[
            jax.ShapeDtypeStruct((T, D_MODEL), BF16),
            jax.ShapeDtypeStruct((T, LANES), F32),
        ],
        scratch_shapes=[pltpu.VMEM((1, LANES), F32)],
        compiler_params=_cparams(("arbitrary",), 40),
        name="prenorm",
    )(x2, gain, wf_hi, wf_lo, bf)


def _mm_kernel(x_ref, w_ref, o_ref, *, sigmoid):
    acc = _dot(x_ref[...], w_ref[...])
    if sigmoid:
        acc = jax.nn.sigmoid(acc)
    o_ref[...] = acc.astype(o_ref.dtype)


def _matmul(x, w, out_dtype, sigmoid, name):
    M, K = x.shape
    N = w.shape[1]
    tm, tn = MM_TM, MM_TN
    return pl.pallas_call(
        functools.partial(_mm_kernel, sigmoid=sigmoid),
        grid=(M // tm, N // tn),
        in_specs=[
            pl.BlockSpec((tm, K), lambda i, j: (i, 0)),
            pl.BlockSpec((K, tn), lambda i, j: (0, j)),
        ],
        out_specs=pl.BlockSpec((tm, tn), lambda i, j: (i, j)),
        out_shape=jax.ShapeDtypeStruct((M, N), out_dtype),
        compiler_params=_cparams(("parallel", "parallel"), 48),
        name=name,
    )(x, w)


def _transpose_cast_kernel(w_ref, o_ref):
    o_ref[...] = w_ref[...].T.astype(o_ref.dtype)


def _transpose_cast(w_t):
    N, K = w_t.shape
    tn = 512
    return pl.pallas_call(
        _transpose_cast_kernel,
        grid=(N // tn,),
        in_specs=[pl.BlockSpec((tn, K), lambda i: (i, 0))],
        out_specs=pl.BlockSpec((K, tn), lambda i: (0, i)),
        out_shape=jax.ShapeDtypeStruct((K, N), BF16),
        compiler_params=_cparams(("parallel",), 32),
        name="weight_transpose_cast",
    )(w_t)


def _foxprep_kernel(q_ref, k_ref, v_ref, c_ref, gq_ref, gk_ref, bd_ref, selq_ref, selk_ref,
                    onesq_ref, onesk_ref, qa_ref, ka_ref, vt_ref):
    bd = bd_ref[...]

    def headnorm(x, g):
        hi, lo = _split2(x * x)
        ssq = _dot(hi, bd) + _dot(lo, bd)
        return x * lax.rsqrt(ssq * (1.0 / HEAD_DIM) + NORM_EPS) * g

    qn = headnorm(q_ref[...], gq_ref[...]) * (HEAD_DIM ** -0.5 * LOG2E)
    kn = headnorm(k_ref[...], gk_ref[...])
    v = v_ref[...]
    hi, mid, lo = _split3(c_ref[...] * LOG2E)
    c3 = jnp.concatenate([hi, mid, lo], axis=1)
    lane = lax.broadcasted_iota(jnp.int32, (qn.shape[0], LANES), 1)
    ones_col = jnp.where(lane == HEAD_DIM, 1.0, 0.0)
    for p in range(FOX_PAIRS):
        qp = qn[:, p * LANES:(p + 1) * LANES]
        vp = v[:, p * LANES:(p + 1) * LANES]
        for j in range(2):
            h = 2 * p + j
            keep = (lane < HEAD_DIM) if j == 0 else (lane >= HEAD_DIM)
            qa_ref[h, :, 0:LANES] = jnp.where(keep, qp, 0.0).astype(BF16)
            qa_ref[h, :, LANES:2 * LANES] = (_dot(c3, selq_ref[h]) + onesq_ref[h]).astype(BF16)
            vh = vp if j == 0 else pltpu.roll(vp, HEAD_DIM, axis=1)
            vt_ref[h, 0] = jnp.where(lane < HEAD_DIM, vh, ones_col).T.astype(BF16)
        ka_ref[p, :, 0:LANES] = kn[:, p * LANES:(p + 1) * LANES].astype(BF16)
        ka_ref[p, :, LANES:2 * LANES] = (_dot(c3, selk_ref[p]) + onesk_ref[...]).astype(BF16)


def _fox_tables():
    selq = np.zeros((FOX_HEADS, 3 * LANES, LANES), np.float32)
    selk = np.zeros((FOX_PAIRS, 3 * LANES, LANES), np.float32)
    onesq = np.zeros((FOX_HEADS, 1, LANES), np.float32)
    onesk = np.zeros((1, LANES), np.float32)
    for h in range(FOX_HEADS):
        p, j = divmod(h, 2)
        for piece in range(3):
            selq[h, piece * LANES + h, 6 * j + piece] = 1.0
            selk[p, piece * LANES + h, 6 * j + 3 + piece] = -1.0
            onesq[h, 0, 6 * j + 3 + piece] = 1.0
            onesk[0, 6 * j + piece] = 1.0
    bd = np.kron(np.eye(FOX_HEADS, dtype=np.float32), np.ones((HEAD_DIM, HEAD_DIM), np.float32))
    return (jnp.asarray(bd, BF16), jnp.asarray(selq, BF16), jnp.asarray(selk, BF16),
            jnp.asarray(onesq, F32), jnp.asarray(onesk, F32))


def _fox_prep(proj, c, gq, gk):
    T = proj.shape[0]
    tm = PREP_TM
    bd, selq, selk, onesq, onesk = _fox_tables()
    const2 = lambda i: (0, 0)
    const3 = lambda i: (0, 0, 0)
    return pl.pallas_call(
        _foxprep_kernel,
        grid=(T // tm,),
        in_specs=[
            pl.BlockSpec((tm, FOX_W), lambda i: (i, 0)),
            pl.BlockSpec((tm, FOX_W), lambda i: (i, 1)),
            pl.BlockSpec((tm, FOX_W), lambda i: (i, 2)),
            pl.BlockSpec((tm, LANES), lambda i: (i, 0)),
            pl.BlockSpec((1, FOX_W), const2),
            pl.BlockSpec((1, FOX_W), const2),
            pl.BlockSpec((FOX_W, FOX_W), const2),
            pl.BlockSpec((FOX_HEADS, 3 * LANES, LANES), const3),
            pl.BlockSpec((FOX_PAIRS, 3 * LANES, LANES), const3),
            pl.BlockSpec((FOX_HEADS, 1, LANES), const3),
            pl.BlockSpec((1, LANES), const2),
        ],
        out_specs=[
            pl.BlockSpec((FOX_HEADS, tm, 2 * LANES), lambda i: (0, i, 0)),
            pl.BlockSpec((FOX_PAIRS, tm, 2 * LANES), lambda i: (0, i, 0)),
            pl.BlockSpec((FOX_HEADS, 1, LANES, tm), lambda i: (0, i, 0, 0)),
        ],
        out_shape=[
            jax.ShapeDtypeStruct((FOX_HEADS, T, 2 * LANES), BF16),
            jax.ShapeDtypeStruct((FOX_PAIRS, T, 2 * LANES), BF16),
            jax.ShapeDtypeStruct((FOX_HEADS, T // tm, LANES, tm), BF16),
        ],
        compiler_params=_cparams(("parallel",), 48),
        name="fox_prep",
    )(proj, proj, proj, c, gq, gk, bd, selq, selk, onesq, onesk)


def _fox_attn_kernel(q_ref, k_ref, vt_ref, o_ref, m_ref, acc_ref, sa_ref, sb_ref):
    qi = pl.program_id(2)
    tk = FOX_TK
    m_ref[...] = jnp.full(m_ref.shape, NEG_BIG, F32)
    acc_ref[...] = jnp.zeros(acc_ref.shape, F32)

    def scores(kb, s_ref):
        off = pl.multiple_of(kb * tk, tk)
        k = k_ref[pl.ds(off, tk), :]
        for j in range(2):
            s_ref[j] = _dot_nt(k, q_ref[j])

    def softmax_pv(kb, s_ref, masked):
        for j in range(2):
            s = s_ref[j]
            if masked:
                key = lax.broadcasted_iota(jnp.int32, s.shape, 0)
                qry = lax.broadcasted_iota(jnp.int32, s.shape, 1)
                s = jnp.where(key <= qry, s, NEG_BIG)
            m_old = m_ref[j]
            m_new = jnp.maximum(m_old, jnp.max(s, axis=0, keepdims=True))
            alpha = jnp.exp2(m_old - m_new)
            p = jnp.exp2(s - m_new)
            acc_ref[j] = alpha * acc_ref[j] + _dot(vt_ref[j, kb], p.astype(BF16))
            m_ref[j] = m_new

    scores(0, sa_ref)

    def two_blocks(i, carry):
        scores(2 * i + 1, sb_ref)
        softmax_pv(2 * i, sa_ref, False)
        scores(2 * i + 2, sa_ref)
        softmax_pv(2 * i + 1, sb_ref, False)
        return carry

    lax.fori_loop(0, qi // 2, two_blocks, 0)

    @pl.when(qi % 2 == 0)
    def _():
        softmax_pv(qi, sa_ref, True)

    @pl.when(qi % 2 == 1)
    def _():
        scores(qi, sb_ref)
        softmax_pv(qi - 1, sa_ref, False)
        softmax_pv(qi, sb_ref, True)

    outs = [acc_ref[j, 0:HEAD_DIM, :] / acc_ref[j, HEAD_DIM:HEAD_DIM + 1, :] for j in range(2)]
    o_ref[...] = jnp.concatenate(outs, axis=0).T.astype(o_ref.dtype)


def _fox_attn(qa, ka, vt, batch, seq):
    T = qa.shape[1]
    tq, tk = FOX_TQ, FOX_TK
    assert tq == tk == PREP_TM
    nq = seq // tq
    nk = seq // tk
    qa4 = qa.reshape(FOX_PAIRS, 2, T, 2 * LANES)
    vt5 = vt.reshape(FOX_PAIRS, 2, batch * nk, LANES, tk)
    return pl.pallas_call(
        _fox_attn_kernel,
        grid=(batch, FOX_PAIRS, nq),
        in_specs=[
            pl.BlockSpec((None, 2, tq, 2 * LANES), lambda b, p, q: (p, 0, b * nq + q, 0)),
            pl.BlockSpec((None, seq, 2 * LANES), lambda b, p, q: (p, b, 0)),
            pl.BlockSpec((None, 2, nk, LANES, tk), lambda b, p, q: (p, 0, b, 0, 0)),
        ],
        out_specs=pl.BlockSpec((tq, LANES), lambda b, p, q: (b * nq + q, p)),
        out_shape=jax.ShapeDtypeStruct((T, FOX_W), BF16),
        scratch_shapes=[
            pltpu.VMEM((2, 1, tq), F32),
            pltpu.VMEM((2, LANES, tq), F32),
            pltpu.VMEM((2, tk, tq), F32),
            pltpu.VMEM((2, tk, tq), F32),
        ],
        compiler_params=_cparams(("parallel", "parallel", "arbitrary"), 48),
        name="fox_attn",
    )(qa4, ka, vt5)


SWA_HEAD_ORDER = (0, 3, 1, 4, 2, 5, 6, 9, 7, 10, 8, 11)
SWA_KV_TILES = SWA_KV_HEADS // 2
SWA_Q_TILES_PER_KV_TILE = SWA_HEADS // 2 // SWA_KV_TILES


def _swa_kernel(sinks_ref, q_ref, k_ref, v_ref, pos_ref, freq_ref, sign_ref, gq_ref, gk_ref, bdq_ref, bdk_ref,
                o_ref, kprev_ref, vtprev_ref):
    n = pl.program_id(1)
    W = WINDOW
    half = HEAD_DIM // 2

    @pl.when(n == 0)
    def _():
        kprev_ref[...] = jnp.zeros_like(kprev_ref)
        vtprev_ref[...] = jnp.zeros_like(vtprev_ref)

    lane = lax.broadcasted_iota(jnp.int32, (W, LANES), 1)
    ang = pos_ref[...].astype(F32) * freq_ref[...]
    cos1 = jnp.cos(ang)
    sin1 = jnp.sin(ang) * sign_ref[...]
    first_half1 = (lane & half) == 0

    def norm_rope(x, g, bd):
        reps = x.shape[1] // LANES
        hi, lo = _split2(x * x)
        ssq = _dot(hi, bd) + _dot(lo, bd)
        xn = x * lax.rsqrt(ssq * (1.0 / HEAD_DIM) + NORM_EPS) * g
        first_half = jnp.tile(first_half1, (1, reps))
        swapped = jnp.where(first_half, pltpu.roll(xn, x.shape[1] - half, axis=1), pltpu.roll(xn, half, axis=1))
        return xn * jnp.tile(cos1, (1, reps)) + swapped * jnp.tile(sin1, (1, reps))

    q = norm_rope(q_ref[...], gq_ref[...], bdq_ref[...]) * (HEAD_DIM ** -0.5 * LOG2E)
    k = norm_rope(k_ref[...], gk_ref[...], bdk_ref[...]).astype(BF16)
    v = v_ref[...]

    key = lax.broadcasted_iota(jnp.int32, (2 * W, 2 * W), 0)
    qry = lax.broadcasted_iota(jnp.int32, (2 * W, 2 * W), 1) & (W - 1)
    visible = ((key < W) & (key > qry) & (n > 0)) | ((key >= W) & (key - W <= qry))
    low_q = lax.broadcasted_iota(jnp.int32, (1, 2 * W), 1) < W

    for kt in range(SWA_KV_TILES):
        k_cur = k[:, kt * LANES:(kt + 1) * LANES]
        vt_cur = v[:, kt * LANES:(kt + 1) * LANES].T.astype(BF16)
        keys = jnp.concatenate([kprev_ref[kt], k_cur], axis=0)
        vt = jnp.concatenate([vtprev_ref[kt], vt_cur], axis=1)
        for r in range(SWA_Q_TILES_PER_KV_TILE):
            t = kt * SWA_Q_TILES_PER_KV_TILE + r
            qt = q[:, t * LANES:(t + 1) * LANES]
            qs = jnp.concatenate([jnp.where(lane < HEAD_DIM, qt, 0.0), jnp.where(lane >= HEAD_DIM, qt, 0.0)],
                                 axis=0).astype(BF16)
            s = jnp.where(visible, _dot_nt(keys, qs), NEG_BIG)
            sink = jnp.where(low_q, sinks_ref[2 * t], sinks_ref[2 * t + 1]) * LOG2E
            m = jnp.maximum(jnp.max(s, axis=0, keepdims=True), sink)
            p = jnp.exp2(s - m)
            denom = jnp.sum(p, axis=0, keepdims=True) + jnp.exp2(sink - m)
            ot = _dot(vt, p.astype(BF16)) / denom
            pair = jnp.concatenate([ot[0:HEAD_DIM, 0:W], ot[HEAD_DIM:LANES, W:2 * W]], axis=0)
            o_ref[:, t * LANES:(t + 1) * LANES] = pair.T.astype(o_ref.dtype)
        kprev_ref[kt] = k_cur
        vtprev_ref[kt] = vt_cur


def _swa_attn(proj, pos2, sinks, gq, gk, batch, seq):
    T = proj.shape[0]
    W = WINDOW
    assert W == LANES
    nb = seq // W
    half = HEAD_DIM // 2
    inv_freq = np.power(ROPE_THETA, -np.arange(0, HEAD_DIM, 2, dtype=np.float32) / HEAD_DIM).astype(np.float32)
    freq = np.tile(inv_freq, LANES // half).reshape(1, LANES)
    sign = np.tile(np.concatenate([-np.ones(half, np.float32), np.ones(half, np.float32)]),
                   LANES // HEAD_DIM).reshape(1, LANES)
    head_ones = np.ones((HEAD_DIM, HEAD_DIM), np.float32)
    bdq = jnp.asarray(np.kron(np.eye(SWA_HEADS, dtype=np.float32), head_ones), BF16)
    bdk = jnp.asarray(np.kron(np.eye(SWA_KV_HEADS, dtype=np.float32), head_ones), BF16)
    q_col = (3 * FOX_W) // SWA_QW
    k_col = (3 * FOX_W + SWA_QW) // SWA_KVW
    const2 = lambda b, n, s: (0, 0)
    grid_spec = pltpu.PrefetchScalarGridSpec(
        num_scalar_prefetch=1,
        grid=(batch, nb),
        in_specs=[
            pl.BlockSpec((W, SWA_QW), lambda b, n, s: (b * nb + n, q_col)),
            pl.BlockSpec((W, SWA_KVW), lambda b, n, s: (b * nb + n, k_col)),
            pl.BlockSpec((W, SWA_KVW), lambda b, n, s: (b * nb + n, k_col + 1)),
            pl.BlockSpec((W, 1), lambda b, n, s: (b * nb + n, 0)),
            pl.BlockSpec((1, LANES), const2),
            pl.BlockSpec((1, LANES), const2),
            pl.BlockSpec((1, SWA_QW), const2),
            pl.BlockSpec((1, SWA_KVW), const2),
            pl.BlockSpec((SWA_QW, SWA_QW), const2),
            pl.BlockSpec((SWA_KVW, SWA_KVW), const2),
        ],
        out_specs=pl.BlockSpec((W, SWA_QW), lambda b, n, s: (b * nb + n, 0)),
        scratch_shapes=[
            pltpu.VMEM((SWA_KV_TILES, W, LANES), BF16),
            pltpu.VMEM((SWA_KV_TILES, LANES, W), BF16),
        ],
    )
    return pl.pallas_call(
        _swa_kernel,
        grid_spec=grid_spec,
        out_shape=jax.ShapeDtypeStruct((T, SWA_QW), BF16),
        compiler_params=_cparams(("parallel", "arbitrary"), 32),
        name="swa_attn",
    )(sinks, proj, proj, proj, pos2, jnp.asarray(freq), jnp.asarray(sign), gq, gk, bdq, bdk)


def _memkv_kernel(mem_ref, g_ref, w_ref, gk_ref, k_ref, v_ref):
    x = mem_ref[...]
    ms = jnp.mean(x * x, axis=-1, keepdims=True)
    mn = (x * lax.rsqrt(ms + NORM_EPS) * g_ref[...]).astype(BF16)
    kv = _dot(mn, w_ref[...])
    for h in range(XMEM_HEADS):
        kh = kv[:, h * LANES:(h + 1) * LANES]
        ms_h = jnp.mean(kh * kh, axis=-1, keepdims=True)
        k_ref[:, h * LANES:(h + 1) * LANES] = (kh * lax.rsqrt(ms_h + NORM_EPS) * gk_ref[...]).astype(BF16)
    v_ref[...] = kv[:, XMEM_W:].astype(BF16)


def _mem_kv(mem2, gain, w, gk):
    R = mem2.shape[0]
    tm = 256
    return pl.pallas_call(
        _memkv_kernel,
        grid=(R // tm,),
        in_specs=[
            pl.BlockSpec((tm, D_MODEL), lambda i: (i, 0)),
            pl.BlockSpec((1, D_MODEL), lambda i: (0, 0)),
            pl.BlockSpec((D_MODEL, 2 * XMEM_W), lambda i: (0, 0)),
            pl.BlockSpec((1, XMEM_HEAD_DIM), lambda i: (0, 0)),
        ],
        out_specs=[
            pl.BlockSpec((tm, XMEM_W), lambda i: (i, 0)),
            pl.BlockSpec((tm, XMEM_W), lambda i: (i, 0)),
        ],
        out_shape=[
            jax.ShapeDtypeStruct((R, XMEM_W), BF16),
            jax.ShapeDtypeStruct((R, XMEM_W), BF16),
        ],
        compiler_params=_cparams(("parallel",), 32),
        name="mem_kv",
    )(mem2, gain, w, gk)


def _memattn_kernel(q_ref, k_ref, v_ref, gq_ref, o_ref):
    q = q_ref[...]
    for h in range(XMEM_HEADS):
        sl = slice(h * LANES, (h + 1) * LANES)
        qh = q[:, sl]
        ms = jnp.mean(qh * qh, axis=-1, keepdims=True)
        qn = (qh * lax.rsqrt(ms + NORM_EPS) * gq_ref[...] * (XMEM_HEAD_DIM ** -0.5)).astype(BF16)
        s = _dot_nt(qn, k_ref[:, sl])
        m = jnp.max(s, axis=-1, keepdims=True)
        p = jnp.exp(s - m)
        l = jnp.sum(p, axis=-1, keepdims=True)
        o_ref[:, sl] = (_dot(p.astype(BF16), v_ref[:, sl]) / l).astype(o_ref.dtype)


def _mem_attn(proj, mk, mv, gq, seq, n_mem):
    T = proj.shape[0]
    tq = MEM_TQ
    per_seq = seq // tq
    q_col = (3 * FOX_W + SWA_QW + 2 * SWA_KVW) // XMEM_W
    return pl.pallas_call(
        _memattn_kernel,
        grid=(T // tq,),
        in_specs=[
            pl.BlockSpec((tq, XMEM_W), lambda i: (i, q_col)),
            pl.BlockSpec((n_mem, XMEM_W), lambda i: (i // per_seq, 0)),
            pl.BlockSpec((n_mem, XMEM_W), lambda i: (i // per_seq, 0)),
            pl.BlockSpec((1, XMEM_HEAD_DIM), lambda i: (0, 0)),
        ],
        out_specs=pl.BlockSpec((tq, XMEM_W), lambda i: (i, 0)),
        out_shape=jax.ShapeDtypeStruct((T, XMEM_W), BF16),
        compiler_params=_cparams(("parallel",), 32),
        name="mem_attn",
    )(proj, mk, mv, gq)


def _merge_kernel(of_ref, os_ref, ox_ref, g0_ref, g1_ref, g2_ref, x_ref, wf_ref, ws_ref, wx_ref, wo_ref,
                  gn_ref, wr_hi_ref, wr_lo_ref, br_ref,
                  h_ref, hn_ref, ri_ref, rg_ref, cnt_ref, carry_ref):
    i = pl.program_id(0)

    @pl.when(i == 0)
    def _():
        carry_ref[...] = jnp.zeros_like(carry_ref)

    merged = (g0_ref[...].astype(F32) * _dot(of_ref[...], wf_ref[...])
              + g1_ref[...].astype(F32) * _dot(os_ref[...], ws_ref[...])
              + g2_ref[...].astype(F32) * _dot(ox_ref[...], wx_ref[...]))
    h = x_ref[...] + _dot(merged.astype(BF16), wo_ref[...])
    h_ref[...] = h
    ms = jnp.mean(h * h, axis=-1, keepdims=True)
    hn = h * lax.rsqrt(ms + NORM_EPS) * gn_ref[...]
    hn_hi, hn_lo = _split2(hn)
    hn_ref[...] = hn
    wr_hi = wr_hi_ref[...]
    logits = _dot(hn_hi, wr_hi) + _dot(hn_lo, wr_hi) + _dot(hn_hi, wr_lo_ref[...]) + br_ref[...]
    tm = logits.shape[0]
    lane = lax.broadcasted_iota(jnp.int32, (tm, LANES), 1).astype(F32)
    work = jnp.where(lane < N_EXPERTS, logits, NEG_BIG)
    vals, idxs = [], []
    for _ in range(TOP_K):
        mx = jnp.max(work, axis=-1, keepdims=True)
        ix = jnp.min(jnp.where(work == mx, lane, float(LANES)), axis=-1, keepdims=True)
        vals.append(mx)
        idxs.append(ix)
        work = jnp.where(lane == ix, NEG_BIG, work)
    es = [jnp.exp(v - vals[0]) for v in vals]
    den = es[0] + es[1] + es[2] + es[3]
    onehot = jnp.zeros((tm, LANES), F32)
    for ix in idxs:
        onehot = onehot + jnp.where(lane == ix, 1.0, 0.0)
    r = lax.broadcasted_iota(jnp.int32, (tm, tm), 0)
    cc = lax.broadcasted_iota(jnp.int32, (tm, tm), 1)
    tri = jnp.where(cc < r, 1.0, 0.0).astype(BF16)
    before = _dot(tri, onehot.astype(BF16)) + carry_ref[...]
    ri = jnp.zeros((tm, LANES), jnp.int32)
    rg = jnp.zeros((tm, LANES), F32)
    for k in range(TOP_K):
        rank = jnp.sum(jnp.where(lane == idxs[k], before, 0.0), axis=-1, keepdims=True)
        ri = jnp.where(lane == k, idxs[k].astype(jnp.int32), ri)
        ri = jnp.where(lane == TOP_K + k, rank.astype(jnp.int32), ri)
        rg = jnp.where(lane == k, es[k] / den, rg)
    ri_ref[...] = ri
    rg_ref[...] = rg
    total = carry_ref[...] + jnp.sum(onehot, axis=0, keepdims=True)
    carry_ref[...] = total
    cnt_ref[...] = total


def _merge(o_fox, o_swa, o_x, gates, x2, wf, ws, wx, wo, gn, wr_hi, wr_lo, br):
    T = x2.shape[0]
    tm = MERGE_TM
    row = lambda i: (i, 0)
    const = lambda i: (0, 0)
    resident = functools.partial(pl.BlockSpec, index_map=const, pipeline_mode=pl.Buffered(1))
    return pl.pallas_call(
        _merge_kernel,
        grid=(T // tm,),
        in_specs=[
            pl.BlockSpec((tm, FOX_W), row),
            pl.BlockSpec((tm, SWA_QW), row),
            pl.BlockSpec((tm, XMEM_W), row),
            pl.BlockSpec((tm, D_MODEL), lambda i: (i, 0)),
            pl.BlockSpec((tm, D_MODEL), lambda i: (i, 1)),
            pl.BlockSpec((tm, D_MODEL), lambda i: (i, 2)),
            pl.BlockSpec((tm, D_MODEL), row),
            resident((FOX_W, D_MODEL)),
            resident((SWA_QW, D_MODEL)),
            resident((XMEM_W, D_MODEL)),
            resident((D_MODEL, D_MODEL)),
            resident((1, D_MODEL)),
            resident((D_MODEL, LANES)),
            resident((D_MODEL, LANES)),
            resident((1, LANES)),
        ],
        out_specs=[
            pl.BlockSpec((tm, D_MODEL), row),
            pl.BlockSpec((tm, D_MODEL), row),
            pl.BlockSpec((tm, LANES), row),
            pl.BlockSpec((tm, LANES), row),
            pl.BlockSpec((1, LANES), const),
        ],
        out_shape=[
            jax.ShapeDtypeStruct((T, D_MODEL), F32),
            jax.ShapeDtypeStruct((T, D_MODEL), F32),
            jax.ShapeDtypeStruct((T, LANES), jnp.int32),
            jax.ShapeDtypeStruct((T, LANES), F32),
            jax.ShapeDtypeStruct((1, LANES), F32),
        ],
        scratch_shapes=[pltpu.VMEM((1, LANES), F32)],
        compiler_params=_cparams(("arbitrary",), 56),
        name="merge_router",
    )(o_fox, o_swa, o_x, gates, gates, gates, x2, wf, ws, wx, wo, gn, wr_hi, wr_lo, br)


def _dispatch_kernel(start_ref, cnt_ref, e_ref, r_ref, hn_ref, xs_ref, zero_ref, buf_ref, sem, in_sem, out_sem):
    i = pl.program_id(0)
    tb = DISPATCH_TB

    def row_copy(src_ref, t, dst_row):
        return pltpu.make_async_copy(src_ref.at[pl.ds(t, 1), :], xs_ref.at[pl.ds(dst_row, 1), :], sem)

    @pl.when(i == 0)
    def _():
        zero_ref[...] = jnp.zeros_like(zero_ref)

        def per_expert(e, carry):
            base = start_ref[e]
            n = cnt_ref[e]
            end = ((n + MOE_BM - 1) // MOE_BM) * MOE_BM

            def fill(r, c):
                row_copy(zero_ref, 0, base + r).start()
                return c

            def drain(r, c):
                row_copy(zero_ref, 0, base + r).wait()
                return c

            lax.fori_loop(n, end, fill, 0)
            lax.fori_loop(n, end, drain, 0)
            return carry

        lax.fori_loop(0, N_EXPERTS, per_expert, 0)

        last = N_EXPERTS - 1
        used = start_ref[last] + ((cnt_ref[last] + MOE_BM - 1) // MOE_BM) * MOE_BM
        zr = zero_ref.shape[0]

        def tail_copy(r):
            return pltpu.make_async_copy(zero_ref, xs_ref.at[pl.ds(pl.multiple_of(used + r * zr, zr), zr), :], sem)

        def tail_fill(r, c):
            tail_copy(r).start()
            return c

        def tail_drain(r, c):
            tail_copy(r).wait()
            return c

        n_tail = (xs_ref.shape[0] - used) // zr
        lax.fori_loop(0, n_tail, tail_fill, 0)
        lax.fori_loop(0, n_tail, tail_drain, 0)

    n_steps = pl.num_programs(0)
    slot = i % 2

    def block_load(step, dst_slot):
        return pltpu.make_async_copy(hn_ref.at[pl.ds(pl.multiple_of(step * tb, tb), tb), :], buf_ref.at[dst_slot],
                                     in_sem.at[dst_slot])

    def scatter_copy(src_slot, t, dst_row):
        return pltpu.make_async_copy(buf_ref.at[src_slot, pl.ds(t, 1), :], xs_ref.at[pl.ds(dst_row, 1), :],
                                     out_sem.at[src_slot])

    def drain_scatters(src_slot):
        def drain(t, carry):
            for k in range(TOP_K):
                scatter_copy(src_slot, t, 0).wait()
            return carry

        lax.fori_loop(0, tb, drain, 0)

    @pl.when(i == 0)
    def _():
        block_load(0, 0).start()

    block_load(i, slot).wait()

    def issue(t, carry):
        for k in range(TOP_K):
            a = t * TOP_K + k
            scatter_copy(slot, t, start_ref[e_ref[a]] + r_ref[a]).start()
        return carry

    lax.fori_loop(0, tb, issue, 0)

    @pl.when(i > 0)
    def _():
        drain_scatters(1 - slot)

    @pl.when(i + 1 < n_steps)
    def _():
        block_load(i + 1, 1 - slot).start()

    @pl.when(i + 1 == n_steps)
    def _():
        drain_scatters(slot)


def _dispatch(pad_start, cnt, e_flat, r_flat, hn, n_slots):
    T, D = hn.shape
    tb = DISPATCH_TB
    grid_spec = pltpu.PrefetchScalarGridSpec(
        num_scalar_prefetch=2,
        grid=(T // tb,),
        in_specs=[
            pl.BlockSpec((tb * TOP_K,), lambda i, s, c: (i,), memory_space=pltpu.SMEM),
            pl.BlockSpec((tb * TOP_K,), lambda i, s, c: (i,), memory_space=pltpu.SMEM),
            pl.BlockSpec(memory_space=pl.ANY),
        ],
        out_specs=pl.BlockSpec(memory_space=pl.ANY),
        scratch_shapes=[pltpu.VMEM((MOE_BM // 2, D), F32), pltpu.VMEM((2, tb, D), F32),
                        pltpu.SemaphoreType.DMA(()), pltpu.SemaphoreType.DMA((2,)), pltpu.SemaphoreType.DMA((2,))],
    )
    return pl.pallas_call(
        _dispatch_kernel,
        grid_spec=grid_spec,
        out_shape=jax.ShapeDtypeStruct((n_slots, D), F32),
        compiler_params=_cparams(("arbitrary",), 32),
        name="dispatch",
    )(pad_start, cnt, e_flat, r_flat, hn)


def _expert_changed(be_ref, i):
    return (i == 0) | (be_ref[i] != be_ref[jnp.maximum(i - 1, 0)])


def _gateup_kernel(be_ref, nu_ref, xs_ref, wg_ref, wu_ref, bg_ref, bu_ref, o_ref, wg_s, wu_s):
    i = pl.program_id(1)

    @pl.when(i < nu_ref[0])
    def _():
        @pl.when(_expert_changed(be_ref, i))
        def _():
            wg_s[...] = wg_ref[0].astype(BF16)
            wu_s[...] = wu_ref[0].astype(BF16)

        x = xs_ref[...].astype(BF16)
        gate = _dot(x, wg_s[...]) + bg_ref[0]
        up = _dot(x, wu_s[...]) + bu_ref[0]
        gate = jnp.minimum(gate, SWIGLU_LIMIT)
        up = jnp.clip(up, -SWIGLU_LIMIT, SWIGLU_LIMIT)
        glu = gate * jax.nn.sigmoid(gate * SWIGLU_ALPHA)
        o_ref[...] = ((up + 1.0) * glu).astype(o_ref.dtype)

    @pl.when(i >= nu_ref[0])
    def _():
        o_ref[...] = jnp.zeros_like(o_ref)


def _down_kernel(be_ref, nu_ref, h_ref, w_ref, b_ref, o_ref, w_s):
    i = pl.program_id(1)

    @pl.when(i < nu_ref[0])
    def _():
        @pl.when(_expert_changed(be_ref, i))
        def _():
            w_s[...] = w_ref[0].astype(BF16)

        o_ref[...] = _dot(h_ref[...], w_s[...]) + b_ref[0]

    @pl.when(i >= nu_ref[0])
    def _():
        o_ref[...] = jnp.zeros_like(o_ref)


def _experts(blk_expert, n_used, xs, w_gate_up, b_gate_up, w_down, b_down):
    P = xs.shape[0]
    bm, tn = MOE_BM, MOE_TN
    n_blk = P // bm
    d_exp = w_down.shape[1]
    up_off = d_exp // tn

    def blk(i, nu):
        return jnp.minimum(i, nu[0] - 1)

    gateup_spec = pltpu.PrefetchScalarGridSpec(
        num_scalar_prefetch=2,
        grid=(d_exp // tn, n_blk),
        in_specs=[
            pl.BlockSpec((bm, D_MODEL), lambda j, i, be, nu: (blk(i, nu), 0)),
            pl.BlockSpec((1, D_MODEL, tn), lambda j, i, be, nu: (be[blk(i, nu)], 0, j)),
            pl.BlockSpec((1, D_MODEL, tn), lambda j, i, be, nu: (be[blk(i, nu)], 0, up_off + j)),
            pl.BlockSpec((1, 1, tn), lambda j, i, be, nu: (be[blk(i, nu)], 0, j)),
            pl.BlockSpec((1, 1, tn), lambda j, i, be, nu: (be[blk(i, nu)], 0, up_off + j)),
        ],
        out_specs=pl.BlockSpec((bm, tn), lambda j, i, be, nu: (i, j)),
        scratch_shapes=[pltpu.VMEM((D_MODEL, tn), BF16), pltpu.VMEM((D_MODEL, tn), BF16)],
    )
    hmid = pl.pallas_call(
        _gateup_kernel,
        grid_spec=gateup_spec,
        out_shape=jax.ShapeDtypeStruct((P, d_exp), BF16),
        compiler_params=_cparams(("arbitrary", "arbitrary"), 56),
        name="expert_gate_up",
    )(blk_expert, n_used, xs, w_gate_up, w_gate_up, b_gate_up, b_gate_up)

    down_spec = pltpu.PrefetchScalarGridSpec(
        num_scalar_prefetch=2,
        grid=(D_MODEL // tn, n_blk),
        in_specs=[
            pl.BlockSpec((bm, d_exp), lambda j, i, be, nu: (blk(i, nu), 0)),
            pl.BlockSpec((1, d_exp, tn), lambda j, i, be, nu: (be[blk(i, nu)], 0, j)),
            pl.BlockSpec((1, 1, tn), lambda j, i, be, nu: (be[blk(i, nu)], 0, j)),
        ],
        out_specs=pl.BlockSpec((bm, tn), lambda j, i, be, nu: (i, j)),
        scratch_shapes=[pltpu.VMEM((d_exp, tn), BF16)],
    )
    return pl.pallas_call(
        _down_kernel,
        grid_spec=down_spec,
        out_shape=jax.ShapeDtypeStruct((P, D_MODEL), F32),
        compiler_params=_cparams(("arbitrary", "arbitrary"), 48),
        name="expert_down",
    )(blk_expert, n_used, hmid, w_down, b_down)


def _combine_kernel(start_ref, e_ref, r_ref, e_next_ref, r_next_ref, h_ref, g_ref, ys_ref, o_ref, buf, sem):
    i = pl.program_id(0)
    n_steps = pl.num_programs(0)
    tb = COMBINE_TB
    slot = i % 2

    def row_copy(dst_slot, t, k, src_row):
        return pltpu.make_async_copy(ys_ref.at[pl.ds(src_row, 1), :], buf.at[dst_slot, k, pl.ds(t, 1), :],
                                     sem.at[dst_slot])

    def request(dst_slot, e_blk, r_blk):
        def issue(t, carry):
            for k in range(TOP_K):
                a = t * TOP_K + k
                row_copy(dst_slot, t, k, start_ref[e_blk[a]] + r_blk[a]).start()
            return carry

        lax.fori_loop(0, tb, issue, 0)

    @pl.when(i == 0)
    def _():
        request(0, e_ref, r_ref)

    @pl.when(i + 1 < n_steps)
    def _():
        request(1 - slot, e_next_ref, r_next_ref)

    def drain(t, carry):
        for k in range(TOP_K):
            row_copy(slot, t, k, 0).wait()
        return carry

    lax.fori_loop(0, tb, drain, 0)
    g = g_ref[...]
    acc = h_ref[...]
    for k in range(TOP_K):
        acc = acc + g[:, k:k + 1] * buf[slot, k]
    o_ref[...] = acc


def _combine(pad_start, e_flat, r_flat, h1, gates, ys):
    T = h1.shape[0]
    tb = COMBINE_TB
    n_steps = T // tb
    idx_spec = lambda fn: pl.BlockSpec((tb * TOP_K,), fn, memory_space=pltpu.SMEM)
    this_blk = lambda i, s: (i,)
    next_blk = lambda i, s: (jnp.minimum(i + 1, n_steps - 1),)
    grid_spec = pltpu.PrefetchScalarGridSpec(
        num_scalar_prefetch=1,
        grid=(n_steps,),
        in_specs=[
            idx_spec(this_blk),
            idx_spec(this_blk),
            idx_spec(next_blk),
            idx_spec(next_blk),
            pl.BlockSpec((tb, D_MODEL), lambda i, s: (i, 0)),
            pl.BlockSpec((tb, LANES), lambda i, s: (i, 0)),
            pl.BlockSpec(memory_space=pl.ANY),
        ],
        out_specs=pl.BlockSpec((tb, D_MODEL), lambda i, s: (i, 0)),
        scratch_shapes=[pltpu.VMEM((2, TOP_K, tb, D_MODEL), F32), pltpu.SemaphoreType.DMA((2,))],
    )
    return pl.pallas_call(
        _combine_kernel,
        grid_spec=grid_spec,
        out_shape=jax.ShapeDtypeStruct((T, D_MODEL), F32),
        compiler_params=_cparams(("arbitrary",), 40),
        name="combine",
    )(pad_start, e_flat, r_flat, e_flat, r_flat, h1, gates, ys)


def _pad_lanes(a, width=LANES):
    return jnp.pad(a, ((0, 0), (0, width - a.shape[1])))


def _layer(h, mem, positions, norm_mix, w_in, b_forget, fox_q_norm, fox_k_norm, swa_q_norm, swa_k_norm,
           swa_sinks, xmem_q_norm, xmem_k_norm, norm_mem, w_mem_kv, w_up_fox, w_up_swa, w_up_xmem, w_out,
           norm_ffn, w_router, b_router, w_gate_up, b_gate_up, w_down, b_down):
    B, S, D = h.shape
    M = mem.shape[1]
    T = B * S
    x2 = h.reshape(T, D)

    attn_end = 3 * FOX_W + FOX_HEADS
    gate_start = attn_end + SWA_QW + 2 * SWA_KVW + XMEM_W
    head_order = np.asarray(SWA_HEAD_ORDER)
    w_t = jnp.transpose(w_in)
    w_sq_t = w_t[attn_end:attn_end + SWA_QW].reshape(SWA_HEADS, HEAD_DIM, D)[head_order].reshape(SWA_QW, D)
    w_attn = _transpose_cast(jnp.concatenate([w_t[:3 * FOX_W], w_sq_t, w_t[attn_end + SWA_QW:gate_start]], axis=0))
    w_gates = _transpose_cast(w_t[gate_start:])
    w_up_swa = w_up_swa.reshape(SWA_HEADS, HEAD_DIM, D)[head_order].reshape(SWA_QW, D)
    swa_sinks = swa_sinks[head_order]
    wf = _pad_lanes(jnp.transpose(w_t[3 * FOX_W:attn_end]))
    wf_hi, wf_lo = _split2(wf)
    bf = _pad_lanes(b_forget.reshape(1, FOX_HEADS))

    def tile_gain(g, reps):
        return jnp.tile(g.reshape(1, -1), (1, reps))

    xn, c = _prenorm(x2, norm_mix.reshape(1, D), wf_hi, wf_lo, bf, S)
    proj = _matmul(xn, w_attn, F32, False, "in_proj_attn")
    gates = _matmul(xn, w_gates, BF16, True, "in_proj_gates")

    qa, ka, vb = _fox_prep(proj, c, tile_gain(fox_q_norm, FOX_HEADS), tile_gain(fox_k_norm, FOX_HEADS))
    o_fox = _fox_attn(qa, ka, vb, B, S)

    o_swa = _swa_attn(proj, positions.reshape(T, 1), swa_sinks.astype(F32),
                      tile_gain(swa_q_norm, SWA_HEADS), tile_gain(swa_k_norm, SWA_KV_HEADS), B, S)

    mk, mv = _mem_kv(mem.reshape(B * M, D), norm_mem.reshape(1, D), w_mem_kv.astype(BF16),
                     xmem_k_norm.reshape(1, XMEM_HEAD_DIM))
    o_x = _mem_attn(proj, mk, mv, xmem_q_norm.reshape(1, XMEM_HEAD_DIM), S, M)

    wr = _pad_lanes(w_router)
    wr_hi, wr_lo = _split2(wr)
    h1, hn, route_i, route_g, counts = _merge(
        o_fox, o_swa, o_x, gates, x2, w_up_fox.astype(BF16), w_up_swa.astype(BF16), w_up_xmem.astype(BF16),
        w_out.astype(BF16), norm_ffn.reshape(1, D), wr_hi, wr_lo, _pad_lanes(b_router.reshape(1, N_EXPERTS)))

    bm = MOE_BM
    cnt = counts[0, :N_EXPERTS].astype(jnp.int32)
    padded = ((cnt + bm - 1) // bm) * bm
    pad_end = jnp.cumsum(padded)
    pad_start = (pad_end - padded).astype(jnp.int32)
    P = T * TOP_K + N_EXPERTS * bm
    n_blk = P // bm
    blk_first = jnp.arange(n_blk, dtype=jnp.int32) * bm
    blk_expert = jnp.minimum(jnp.sum((pad_end[None, :] <= blk_first[:, None]).astype(jnp.int32), axis=1),
                             N_EXPERTS - 1)
    n_used = (pad_end[-1:] // bm).astype(jnp.int32)
    e_flat = route_i[:, :TOP_K].reshape(T * TOP_K)
    r_flat = route_i[:, TOP_K:2 * TOP_K].reshape(T * TOP_K)

    xs = _dispatch(pad_start, cnt, e_flat, r_flat, hn, P)
    ys = _experts(blk_expert, n_used, xs, w_gate_up, b_gate_up.reshape(N_EXPERTS, 1, -1),
                  w_down, b_down.reshape(N_EXPERTS, 1, -1))
    out = _combine(pad_start, e_flat, r_flat, h1, route_g, ys)
    return out.reshape(B, S, D)


def kernel(x, mem, positions, norm_mix, w_in, b_forget, fox_q_norm, fox_k_norm, swa_q_norm, swa_k_norm, swa_sinks, xmem_q_norm, xmem_k_norm, norm_mem, w_mem_kv, w_up_fox, w_up_swa, w_up_xmem, w_out, norm_ffn, w_router, b_router, w_gate_up, b_gate_up, w_down, b_down):
    h = x
    for layer in range(norm_mix.shape[0]):
        h = _layer(
            h, mem, positions, norm_mix[layer], w_in[layer], b_forget[layer],
            fox_q_norm[layer], fox_k_norm[layer], swa_q_norm[layer], swa_k_norm[layer],
            swa_sinks[layer], xmem_q_norm[layer], xmem_k_norm[layer], norm_mem[layer],
            w_mem_kv[layer], w_up_fox[layer], w_up_swa[layer], w_up_xmem[layer],
            w_out[layer], norm_ffn[layer], w_router[layer], b_router[layer],
            w_gate_up[layer], b_gate_up[layer], w_down[layer], b_down[layer])
    return h
```

```python
import functools

import numpy as np
import jax
import jax.numpy as jnp
from jax import lax
from jax.experimental import pallas as pl
from jax.experimental.pallas import tpu as pltpu

F32 = jnp.float32
BF16 = jnp.bfloat16

D_MODEL = 2048
HEAD_DIM = 64
FOX_HEADS = 12
SWA_HEADS = 12
SWA_KV_HEADS = 4
SWA_GROUP = SWA_HEADS // SWA_KV_HEADS
WINDOW = 128
XMEM_HEADS = 4
XMEM_HEAD_DIM = 128
N_EXPERTS = 32
TOP_K = 4
SWIGLU_LIMIT = 7.0
SWIGLU_ALPHA = 1.702
ROPE_THETA = 10000.0
NORM_EPS = 1e-6

FOX_W = FOX_HEADS * HEAD_DIM
SWA_QW = SWA_HEADS * HEAD_DIM
SWA_KVW = SWA_KV_HEADS * HEAD_DIM
XMEM_W = XMEM_HEADS * XMEM_HEAD_DIM
FOX_PAIRS = FOX_HEADS // 2

LANES = 128
NEG_BIG = -1e30
LOG2E = 1.4426950408889634
MIB = 1024 * 1024

PRE_TM = 512
MM_TM = 1024
MM_TN = 1024
PREP_TM = 512
FOX_TQ = 512
FOX_TK = 512
MEM_TQ = 512
MERGE_TM = 256
MOE_BM = 512
MOE_SUB = 128
MOE_TN = 1024
MOE_DOWN_TN = 2048
DISPATCH_TB = 256
COMBINE_TB = 128


def _cparams(semantics, vmem_mib):
    return pltpu.CompilerParams(dimension_semantics=semantics, vmem_limit_bytes=vmem_mib * MIB)


def _dot(a, b):
    return jnp.dot(a, b, preferred_element_type=F32)


def _dot_nt(a, b):
    return lax.dot_general(a, b, (((1,), (1,)), ((), ())), preferred_element_type=F32)


def _split2(x):
    hi = x.astype(BF16)
    lo = (x - hi.astype(F32)).astype(BF16)
    return hi, lo


def _split3(x):
    hi = x.astype(BF16)
    r = x - hi.astype(F32)
    mid = r.astype(BF16)
    lo = (r - mid.astype(F32)).astype(BF16)
    return hi, mid, lo


def _prenorm_kernel(x_ref, g_ref, wf_hi_ref, wf_lo_ref, bf_ref, xn_ref, c_ref, carry_ref, *, blocks_per_seq):
    i = pl.program_id(0)

    @pl.when(i % blocks_per_seq == 0)
    def _():
        carry_ref[...] = jnp.zeros_like(carry_ref)

    x = x_ref[...]
    tm = x.shape[0]
    ms = jnp.mean(x * x, axis=-1, keepdims=True)
    xn = x * lax.rsqrt(ms + NORM_EPS) * g_ref[...]
    xn_hi, xn_lo = _split2(xn)
    xn_ref[...] = xn_hi
    wf_hi = wf_hi_ref[...]
    z = _dot(xn_hi, wf_hi) + _dot(xn_lo, wf_hi) + _dot(xn_hi, wf_lo_ref[...]) + bf_ref[...]
    logf = jnp.minimum(z, 0.0) - jnp.log1p(jnp.exp(-jnp.abs(z)))
    lane = lax.broadcasted_iota(jnp.int32, logf.shape, 1)
    logf = jnp.where(lane < FOX_HEADS, logf, 0.0)
    hi, mid, lo = _split3(logf)
    r = lax.broadcasted_iota(jnp.int32, (tm, tm), 0)
    cc = lax.broadcasted_iota(jnp.int32, (tm, tm), 1)
    tri = jnp.where(cc <= r, 1.0, 0.0).astype(BF16)
    c = _dot(tri, hi) + _dot(tri, mid) + _dot(tri, lo) + carry_ref[...]
    c_ref[...] = c
    carry_ref[...] = c[tm - 1:tm, :]


def _prenorm(x2, gain, wf_hi, wf_lo, bf, seq):
    T = x2.shape[0]
    tm = PRE_TM
    return pl.pallas_call(
        functools.partial(_prenorm_kernel, blocks_per_seq=seq // tm),
        grid=(T // tm,),
        in_specs=[
            pl.BlockSpec((tm, D_MODEL), lambda i: (i, 0)),
            pl.BlockSpec((1, D_MODEL), lambda i: (0, 0)),
            pl.BlockSpec((D_MODEL, LANES), lambda i: (0, 0)),
            pl.BlockSpec((D_MODEL, LANES), lambda i: (0, 0)),
            pl.BlockSpec((1, LANES), lambda i: (0, 0)),
        ],
        out_specs=[
            pl.BlockSpec((tm, D_MODEL), lambda i: (i, 0)),
            pl.BlockSpec((tm, LANES), lambda i: (i, 0)),
        ],
        out_shape=[
            jax.ShapeDtypeStruct((T, D_MODEL), BF16),
            jax.ShapeDtypeStruct((T, LANES), F32),
        ],
        scratch_shapes=[pltpu.VMEM((1, LANES), F32)],
        compiler_params=_cparams(("arbitrary",), 40),
        name="prenorm",
    )(x2, gain, wf_hi, wf_lo, bf)


def _mm_kernel(x_ref, w_ref, o_ref, *, sigmoid):
    acc = _dot(x_ref[...], w_ref[...])
    if sigmoid:
        acc = jax.nn.sigmoid(acc)
    o_ref[...] = acc.astype(o_ref.dtype)


def _matmul(x, w, out_dtype, sigmoid, name):
    M, K = x.shape
    N = w.shape[1]
    tm, tn = MM_TM, MM_TN
    return pl.pallas_call(
        functools.partial(_mm_kernel, sigmoid=sigmoid),
        grid=(M // tm, N // tn),
        in_specs=[
            pl.BlockSpec((tm, K), lambda i, j: (i, 0)),
            pl.BlockSpec((K, tn), lambda i, j: (0, j)),
        ],
        out_specs=pl.BlockSpec((tm, tn), lambda i, j: (i, j)),
        out_shape=jax.ShapeDtypeStruct((M, N), out_dtype),
        compiler_params=_cparams(("parallel", "parallel"), 48),
        name=name,
    )(x, w)


def _transpose_cast_kernel(w_ref, o_ref):
    o_ref[...] = w_ref[...].T.astype(o_ref.dtype)


def _transpose_cast(w_t):
    N, K = w_t.shape
    tn = 512
    return pl.pallas_call(
        _transpose_cast_kernel,
        grid=(N // tn,),
        in_specs=[pl.BlockSpec((tn, K), lambda i: (i, 0))],
        out_specs=pl.BlockSpec((K, tn), lambda i: (0, i)),
        out_shape=jax.ShapeDtypeStruct((K, N), BF16),
        compiler_params=_cparams(("parallel",), 32),
        name="weight_transpose_cast",
    )(w_t)


def _foxprep_kernel(q_ref, k_ref, v_ref, c_ref, gq_ref, gk_ref, bd_ref, selq_ref, selk_ref,
                    onesq_ref, onesk_ref, qa_ref, ka_ref, vt_ref):
    bd = bd_ref[...]

    def headnorm(x, g):
        hi, lo = _split2(x * x)
        ssq = _dot(hi, bd) + _dot(lo, bd)
        return x * lax.rsqrt(ssq * (1.0 / HEAD_DIM) + NORM_EPS) * g

    qn = headnorm(q_ref[...], gq_ref[...]) * (HEAD_DIM ** -0.5 * LOG2E)
    kn = headnorm(k_ref[...], gk_ref[...])
    v = v_ref[...]
    hi, mid, lo = _split3(c_ref[...] * LOG2E)
    c3 = jnp.concatenate([hi, mid, lo], axis=1)
    lane = lax.broadcasted_iota(jnp.int32, (qn.shape[0], LANES), 1)
    ones_col = jnp.where(lane == HEAD_DIM, 1.0, 0.0)
    for p in range(FOX_PAIRS):
        qp = qn[:, p * LANES:(p + 1) * LANES]
        vp = v[:, p * LANES:(p + 1) * LANES]
        for j in range(2):
            h = 2 * p + j
            keep = (lane < HEAD_DIM) if j == 0 else (lane >= HEAD_DIM)
            qa_ref[h, :, 0:LANES] = jnp.where(keep, qp, 0.0).astype(BF16)
            qa_ref[h, :, LANES:2 * LANES] = (_dot(c3, selq_ref[h]) + onesq_ref[h]).astype(BF16)
            vh = vp if j == 0 else pltpu.roll(vp, HEAD_DIM, axis=1)
            vt_ref[h, 0] = jnp.where(lane < HEAD_DIM, vh, ones_col).T.astype(BF16)
        ka_ref[p, :, 0:LANES] = kn[:, p * LANES:(p + 1) * LANES].astype(BF16)
        ka_ref[p, :, LANES:2 * LANES] = (_dot(c3, selk_ref[p]) + onesk_ref[...]).astype(BF16)


def _fox_tables():
    selq = np.zeros((FOX_HEADS, 3 * LANES, LANES), np.float32)
    selk = np.zeros((FOX_PAIRS, 3 * LANES, LANES), np.float32)
    onesq = np.zeros((FOX_HEADS, 1, LANES), np.float32)
    onesk = np.zeros((1, LANES), np.float32)
    for h in range(FOX_HEADS):
        p, j = divmod(h, 2)
        for piece in range(3):
            selq[h, piece * LANES + h, 6 * j + piece] = 1.0
            selk[p, piece * LANES + h, 6 * j + 3 + piece] = -1.0
            onesq[h, 0, 6 * j + 3 + piece] = 1.0
            onesk[0, 6 * j + piece] = 1.0
    bd = np.kron(np.eye(FOX_HEADS, dtype=np.float32), np.ones((HEAD_DIM, HEAD_DIM), np.float32))
    return (jnp.asarray(bd, BF16), jnp.asarray(selq, BF16), jnp.asarray(selk, BF16),
            jnp.asarray(onesq, F32), jnp.asarray(onesk, F32))


def _fox_prep(proj, c, gq, gk):
    T = proj.shape[0]
    tm = PREP_TM
    bd, selq, selk, onesq, onesk = _fox_tables()
    const2 = lambda i: (0, 0)
    const3 = lambda i: (0, 0, 0)
    return pl.pallas_call(
        _foxprep_kernel,
        grid=(T // tm,),
        in_specs=[
            pl.BlockSpec((tm, FOX_W), lambda i: (i, 0)),
            pl.BlockSpec((tm, FOX_W), lambda i: (i, 1)),
            pl.BlockSpec((tm, FOX_W), lambda i: (i, 2)),
            pl.BlockSpec((tm, LANES), lambda i: (i, 0)),
            pl.BlockSpec((1, FOX_W), const2),
            pl.BlockSpec((1, FOX_W), const2),
            pl.BlockSpec((FOX_W, FOX_W), const2),
            pl.BlockSpec((FOX_HEADS, 3 * LANES, LANES), const3),
            pl.BlockSpec((FOX_PAIRS, 3 * LANES, LANES), const3),
            pl.BlockSpec((FOX_HEADS, 1, LANES), const3),
            pl.BlockSpec((1, LANES), const2),
        ],
        out_specs=[
            pl.BlockSpec((FOX_HEADS, tm, 2 * LANES), lambda i: (0, i, 0)),
            pl.BlockSpec((FOX_PAIRS, tm, 2 * LANES), lambda i: (0, i, 0)),
            pl.BlockSpec((FOX_HEADS, 1, LANES, tm), lambda i: (0, i, 0, 0)),
        ],
        out_shape=[
            jax.ShapeDtypeStruct((FOX_HEADS, T, 2 * LANES), BF16),
            jax.ShapeDtypeStruct((FOX_PAIRS, T, 2 * LANES), BF16),
            jax.ShapeDtypeStruct((FOX_HEADS, T // tm, LANES, tm), BF16),
        ],
        compiler_params=_cparams(("parallel",), 48),
        name="fox_prep",
    )(proj, proj, proj, c, gq, gk, bd, selq, selk, onesq, onesk)


def _fox_attn_kernel(q_ref, k_ref, vt_ref, o_ref, m_ref, acc_ref, sa_ref, sb_ref):
    qi = pl.program_id(2)
    tk = FOX_TK
    m_ref[...] = jnp.full(m_ref.shape, NEG_BIG, F32)
    acc_ref[...] = jnp.zeros(acc_ref.shape, F32)

    def scores(kb, s_ref):
        off = pl.multiple_of(kb * tk, tk)
        k = k_ref[pl.ds(off, tk), :]
        for j in range(2):
            s_ref[j] = _dot_nt(k, q_ref[j])

    def softmax_pv(kb, s_ref, masked):
        for j in range(2):
            s = s_ref[j]
            if masked:
                key = lax.broadcasted_iota(jnp.int32, s.shape, 0)
                qry = lax.broadcasted_iota(jnp.int32, s.shape, 1)
                s = jnp.where(key <= qry, s, NEG_BIG)
            m_old = m_ref[j]
            m_new = jnp.maximum(m_old, jnp.max(s, axis=0, keepdims=True))
            alpha = jnp.exp2(m_old - m_new)
            p = jnp.exp2(s - m_new)
            acc_ref[j] = alpha * acc_ref[j] + _dot(vt_ref[j, kb], p.astype(BF16))
            m_ref[j] = m_new

    scores(0, sa_ref)

    def two_blocks(i, carry):
        scores(2 * i + 1, sb_ref)
        softmax_pv(2 * i, sa_ref, False)
        scores(2 * i + 2, sa_ref)
        softmax_pv(2 * i + 1, sb_ref, False)
        return carry

    lax.fori_loop(0, qi // 2, two_blocks, 0)

    @pl.when(qi % 2 == 0)
    def _():
        softmax_pv(qi, sa_ref, True)

    @pl.when(qi % 2 == 1)
    def _():
        scores(qi, sb_ref)
        softmax_pv(qi - 1, sa_ref, False)
        softmax_pv(qi, sb_ref, True)

    outs = [acc_ref[j, 0:HEAD_DIM, :] / acc_ref[j, HEAD_DIM:HEAD_DIM + 1, :] for j in range(2)]
    o_ref[...] = jnp.concatenate(outs, axis=0).T.astype(o_ref.dtype)


def _fox_attn(qa, ka, vt, batch, seq):
    T = qa.shape[1]
    tq, tk = FOX_TQ, FOX_TK
    assert tq == tk == PREP_TM
    nq = seq // tq
    nk = seq // tk
    qa4 = qa.reshape(FOX_PAIRS, 2, T, 2 * LANES)
    vt5 = vt.reshape(FOX_PAIRS, 2, batch * nk, LANES, tk)
    return pl.pallas_call(
        _fox_attn_kernel,
        grid=(batch, FOX_PAIRS, nq),
        in_specs=[
            pl.BlockSpec((None, 2, tq, 2 * LANES), lambda b, p, q: (p, 0, b * nq + q, 0)),
            pl.BlockSpec((None, seq, 2 * LANES), lambda b, p, q: (p, b, 0)),
            pl.BlockSpec((None, 2, nk, LANES, tk), lambda b, p, q: (p, 0, b, 0, 0)),
        ],
        out_specs=pl.BlockSpec((tq, LANES), lambda b, p, q: (b * nq + q, p)),
        out_shape=jax.ShapeDtypeStruct((T, FOX_W), BF16),
        scratch_shapes=[
            pltpu.VMEM((2, 1, tq), F32),
            pltpu.VMEM((2, LANES, tq), F32),
            pltpu.VMEM((2, tk, tq), F32),
            pltpu.VMEM((2, tk, tq), F32),
        ],
        compiler_params=_cparams(("parallel", "parallel", "arbitrary"), 48),
        name="fox_attn",
    )(qa4, ka, vt5)


SWA_HEAD_ORDER = (0, 3, 1, 4, 2, 5, 6, 9, 7, 10, 8, 11)
SWA_KV_TILES = SWA_KV_HEADS // 2
SWA_Q_TILES_PER_KV_TILE = SWA_HEADS // 2 // SWA_KV_TILES


def _swa_kernel(sinks_ref, q_ref, k_ref, v_ref, pos_ref, freq_ref, sign_ref, gq_ref, gk_ref, bdq_ref, bdk_ref,
                o_ref, kprev_ref, vtprev_ref):
    n = pl.program_id(1)
    W = WINDOW
    half = HEAD_DIM // 2

    @pl.when(n == 0)
    def _():
        kprev_ref[...] = jnp.zeros_like(kprev_ref)
        vtprev_ref[...] = jnp.zeros_like(vtprev_ref)

    lane = lax.broadcasted_iota(jnp.int32, (W, LANES), 1)
    ang = pos_ref[...].astype(F32) * freq_ref[...]
    cos1 = jnp.cos(ang)
    sin1 = jnp.sin(ang) * sign_ref[...]
    first_half1 = (lane & half) == 0

    def norm_rope(x, g, bd):
        reps = x.shape[1] // LANES
        hi, lo = _split2(x * x)
        ssq = _dot(hi, bd) + _dot(lo, bd)
        xn = x * lax.rsqrt(ssq * (1.0 / HEAD_DIM) + NORM_EPS) * g
        first_half = jnp.tile(first_half1, (1, reps))
        swapped = jnp.where(first_half, pltpu.roll(xn, x.shape[1] - half, axis=1), pltpu.roll(xn, half, axis=1))
        return xn * jnp.tile(cos1, (1, reps)) + swapped * jnp.tile(sin1, (1, reps))

    q = norm_rope(q_ref[...], gq_ref[...], bdq_ref[...]) * (HEAD_DIM ** -0.5 * LOG2E)
    k = norm_rope(k_ref[...], gk_ref[...], bdk_ref[...]).astype(BF16)
    v = v_ref[...]

    key = lax.broadcasted_iota(jnp.int32, (2 * W, 2 * W), 0)
    qry = lax.broadcasted_iota(jnp.int32, (2 * W, 2 * W), 1) & (W - 1)
    visible = ((key < W) & (key > qry) & (n > 0)) | ((key >= W) & (key - W <= qry))
    low_q = lax.broadcasted_iota(jnp.int32, (1, 2 * W), 1) < W

    for kt in range(SWA_KV_TILES):
        k_cur = k[:, kt * LANES:(kt + 1) * LANES]
        vt_cur = v[:, kt * LANES:(kt + 1) * LANES].T.astype(BF16)
        keys = jnp.concatenate([kprev_ref[kt], k_cur], axis=0)
        vt = jnp.concatenate([vtprev_ref[kt], vt_cur], axis=1)
        for r in range(SWA_Q_TILES_PER_KV_TILE):
            t = kt * SWA_Q_TILES_PER_KV_TILE + r
            qt = q[:, t * LANES:(t + 1) * LANES]
            qs = jnp.concatenate([jnp.where(lane < HEAD_DIM, qt, 0.0), jnp.where(lane >= HEAD_DIM, qt, 0.0)],
                                 axis=0).astype(BF16)
            s = jnp.where(visible, _dot_nt(keys, qs), NEG_BIG)
            sink = jnp.where(low_q, sinks_ref[2 * t], sinks_ref[2 * t + 1]) * LOG2E
            m = jnp.maximum(jnp.max(s, axis=0, keepdims=True), sink)
            p = jnp.exp2(s - m)
            denom = jnp.sum(p, axis=0, keepdims=True) + jnp.exp2(sink - m)
            ot = _dot(vt, p.astype(BF16)) / denom
            pair = jnp.concatenate([ot[0:HEAD_DIM, 0:W], ot[HEAD_DIM:LANES, W:2 * W]], axis=0)
            o_ref[:, t * LANES:(t + 1) * LANES] = pair.T.astype(o_ref.dtype)
        kprev_ref[kt] = k_cur
        vtprev_ref[kt] = vt_cur


def _swa_attn(proj, pos2, sinks, gq, gk, batch, seq):
    T = proj.shape[0]
    W = WINDOW
    assert W == LANES
    nb = seq // W
    half = HEAD_DIM // 2
    inv_freq = np.power(ROPE_THETA, -np.arange(0, HEAD_DIM, 2, dtype=np.float32) / HEAD_DIM).astype(np.float32)
    freq = np.tile(inv_freq, LANES // half).reshape(1, LANES)
    sign = np.tile(np.concatenate([-np.ones(half, np.float32), np.ones(half, np.float32)]),
                   LANES // HEAD_DIM).reshape(1, LANES)
    head_ones = np.ones((HEAD_DIM, HEAD_DIM), np.float32)
    bdq = jnp.asarray(np.kron(np.eye(SWA_HEADS, dtype=np.float32), head_ones), BF16)
    bdk = jnp.asarray(np.kron(np.eye(SWA_KV_HEADS, dtype=np.float32), head_ones), BF16)
    q_col = (3 * FOX_W) // SWA_QW
    k_col = (3 * FOX_W + SWA_QW) // SWA_KVW
    const2 = lambda b, n, s: (0, 0)
    grid_spec = pltpu.PrefetchScalarGridSpec(
        num_scalar_prefetch=1,
        grid=(batch, nb),
        in_specs=[
            pl.BlockSpec((W, SWA_QW), lambda b, n, s: (b * nb + n, q_col)),
            pl.BlockSpec((W, SWA_KVW), lambda b, n, s: (b * nb + n, k_col)),
            pl.BlockSpec((W, SWA_KVW), lambda b, n, s: (b * nb + n, k_col + 1)),
            pl.BlockSpec((W, 1), lambda b, n, s: (b * nb + n, 0)),
            pl.BlockSpec((1, LANES), const2),
            pl.BlockSpec((1, LANES), const2),
            pl.BlockSpec((1, SWA_QW), const2),
            pl.BlockSpec((1, SWA_KVW), const2),
            pl.BlockSpec((SWA_QW, SWA_QW), const2),
            pl.BlockSpec((SWA_KVW, SWA_KVW), const2),
        ],
        out_specs=pl.BlockSpec((W, SWA_QW), lambda b, n, s: (b * nb + n, 0)),
        scratch_shapes=[
            pltpu.VMEM((SWA_KV_TILES, W, LANES), BF16),
            pltpu.VMEM((SWA_KV_TILES, LANES, W), BF16),
        ],
    )
    return pl.pallas_call(
        _swa_kernel,
        grid_spec=grid_spec,
        out_shape=jax.ShapeDtypeStruct((T, SWA_QW), BF16),
        compiler_params=_cparams(("parallel", "arbitrary"), 32),
        name="swa_attn",
    )(sinks, proj, proj, proj, pos2, jnp.asarray(freq), jnp.asarray(sign), gq, gk, bdq, bdk)


def _memkv_kernel(mem_ref, g_ref, w_ref, gk_ref, k_ref, v_ref):
    x = mem_ref[...]
    ms = jnp.mean(x * x, axis=-1, keepdims=True)
    mn = (x * lax.rsqrt(ms + NORM_EPS) * g_ref[...]).astype(BF16)
    kv = _dot(mn, w_ref[...])
    for h in range(XMEM_HEADS):
        kh = kv[:, h * LANES:(h + 1) * LANES]
        ms_h = jnp.mean(kh * kh, axis=-1, keepdims=True)
        k_ref[:, h * LANES:(h + 1) * LANES] = (kh * lax.rsqrt(ms_h + NORM_EPS) * gk_ref[...]).astype(BF16)
    v_ref[...] = kv[:, XMEM_W:].astype(BF16)


def _mem_kv(mem2, gain, w, gk):
    R = mem2.shape[0]
    tm = 256
    return pl.pallas_call(
        _memkv_kernel,
        grid=(R // tm,),
        in_specs=[
            pl.BlockSpec((tm, D_MODEL), lambda i: (i, 0)),
            pl.BlockSpec((1, D_MODEL), lambda i: (0, 0)),
            pl.BlockSpec((D_MODEL, 2 * XMEM_W), lambda i: (0, 0)),
            pl.BlockSpec((1, XMEM_HEAD_DIM), lambda i: (0, 0)),
        ],
        out_specs=[
            pl.BlockSpec((tm, XMEM_W), lambda i: (i, 0)),
            pl.BlockSpec((tm, XMEM_W), lambda i: (i, 0)),
        ],
        out_shape=[
            jax.ShapeDtypeStruct((R, XMEM_W), BF16),
            jax.ShapeDtypeStruct((R, XMEM_W), BF16),
        ],
        compiler_params=_cparams(("parallel",), 32),
        name="mem_kv",
    )(mem2, gain, w, gk)


def _memattn_kernel(q_ref, k_ref, v_ref, gq_ref, o_ref):
    q = q_ref[...]
    for h in range(XMEM_HEADS):
        sl = slice(h * LANES, (h + 1) * LANES)
        qh = q[:, sl]
        ms = jnp.mean(qh * qh, axis=-1, keepdims=True)
        qn = (qh * lax.rsqrt(ms + NORM_EPS) * gq_ref[...] * (XMEM_HEAD_DIM ** -0.5)).astype(BF16)
        s = _dot_nt(qn, k_ref[:, sl])
        m = jnp.max(s, axis=-1, keepdims=True)
        p = jnp.exp(s - m)
        l = jnp.sum(p, axis=-1, keepdims=True)
        o_ref[:, sl] = (_dot(p.astype(BF16), v_ref[:, sl]) / l).astype(o_ref.dtype)


def _mem_attn(proj, mk, mv, gq, seq, n_mem):
    T = proj.shape[0]
    tq = MEM_TQ
    per_seq = seq // tq
    q_col = (3 * FOX_W + SWA_QW + 2 * SWA_KVW) // XMEM_W
    return pl.pallas_call(
        _memattn_kernel,
        grid=(T // tq,),
        in_specs=[
            pl.BlockSpec((tq, XMEM_W), lambda i: (i, q_col)),
            pl.BlockSpec((n_mem, XMEM_W), lambda i: (i // per_seq, 0)),
            pl.BlockSpec((n_mem, XMEM_W), lambda i: (i // per_seq, 0)),
            pl.BlockSpec((1, XMEM_HEAD_DIM), lambda i: (0, 0)),
        ],
        out_specs=pl.BlockSpec((tq, XMEM_W), lambda i: (i, 0)),
        out_shape=jax.ShapeDtypeStruct((T, XMEM_W), BF16),
        compiler_params=_cparams(("parallel",), 32),
        name="mem_attn",
    )(proj, mk, mv, gq)


def _merge_kernel(of_ref, os_ref, ox_ref, g0_ref, g1_ref, g2_ref, x_ref, wf_ref, ws_ref, wx_ref, wo_ref,
                  gn_ref, wr_hi_ref, wr_lo_ref, br_ref,
                  h_ref, hn_ref, ri_ref, rg_ref, cnt_ref, carry_ref):
    i = pl.program_id(0)

    @pl.when(i == 0)
    def _():
        carry_ref[...] = jnp.zeros_like(carry_ref)

    merged = (g0_ref[...].astype(F32) * _dot(of_ref[...], wf_ref[...])
              + g1_ref[...].astype(F32) * _dot(os_ref[...], ws_ref[...])
              + g2_ref[...].astype(F32) * _dot(ox_ref[...], wx_ref[...]))
    h = x_ref[...] + _dot(merged.astype(BF16), wo_ref[...])
    h_ref[...] = h
    ms = jnp.mean(h * h, axis=-1, keepdims=True)
    hn = h * lax.rsqrt(ms + NORM_EPS) * gn_ref[...]
    hn_hi, hn_lo = _split2(hn)
    hn_ref[...] = hn
    wr_hi = wr_hi_ref[...]
    logits = _dot(hn_hi, wr_hi) + _dot(hn_lo, wr_hi) + _dot(hn_hi, wr_lo_ref[...]) + br_ref[...]
    tm = logits.shape[0]
    lane = lax.broadcasted_iota(jnp.int32, (tm, LANES), 1).astype(F32)
    work = jnp.where(lane < N_EXPERTS, logits, NEG_BIG)
    vals, idxs = [], []
    for _ in range(TOP_K):
        mx = jnp.max(work, axis=-1, keepdims=True)
        ix = jnp.min(jnp.where(work == mx, lane, float(LANES)), axis=-1, keepdims=True)
        vals.append(mx)
        idxs.append(ix)
        work = jnp.where(lane == ix, NEG_BIG, work)
    es = [jnp.exp(v - vals[0]) for v in vals]
    den = es[0] + es[1] + es[2] + es[3]
    onehot = jnp.zeros((tm, LANES), F32)
    for ix in idxs:
        onehot = onehot + jnp.where(lane == ix, 1.0, 0.0)
    r = lax.broadcasted_iota(jnp.int32, (tm, tm), 0)
    cc = lax.broadcasted_iota(jnp.int32, (tm, tm), 1)
    tri = jnp.where(cc < r, 1.0, 0.0).astype(BF16)
    before = _dot(tri, onehot.astype(BF16)) + carry_ref[...]
    ri = jnp.zeros((tm, LANES), jnp.int32)
    rg = jnp.zeros((tm, LANES), F32)
    for k in range(TOP_K):
        rank = jnp.sum(jnp.where(lane == idxs[k], before, 0.0), axis=-1, keepdims=True)
        ri = jnp.where(lane == k, idxs[k].astype(jnp.int32), ri)
        ri = jnp.where(lane == TOP_K + k, rank.astype(jnp.int32), ri)
        rg = jnp.where(lane == k, es[k] / den, rg)
    ri_ref[...] = ri
    rg_ref[...] = rg
    total = carry_ref[...] + jnp.sum(onehot, axis=0, keepdims=True)
    carry_ref[...] = total
    cnt_ref[...] = total


def _merge(o_fox, o_swa, o_x, gates, x2, wf, ws, wx, wo, gn, wr_hi, wr_lo, br):
    T = x2.shape[0]
    tm = MERGE_TM
    row = lambda i: (i, 0)
    const = lambda i: (0, 0)
    resident = functools.partial(pl.BlockSpec, index_map=const, pipeline_mode=pl.Buffered(1))
    return pl.pallas_call(
        _merge_kernel,
        grid=(T // tm,),
        in_specs=[
            pl.BlockSpec((tm, FOX_W), row),
            pl.BlockSpec((tm, SWA_QW), row),
            pl.BlockSpec((tm, XMEM_W), row),
            pl.BlockSpec((tm, D_MODEL), lambda i: (i, 0)),
            pl.BlockSpec((tm, D_MODEL), lambda i: (i, 1)),
            pl.BlockSpec((tm, D_MODEL), lambda i: (i, 2)),
            pl.BlockSpec((tm, D_MODEL), row),
            resident((FOX_W, D_MODEL)),
            resident((SWA_QW, D_MODEL)),
            resident((XMEM_W, D_MODEL)),
            resident((D_MODEL, D_MODEL)),
            resident((1, D_MODEL)),
            resident((D_MODEL, LANES)),
            resident((D_MODEL, LANES)),
            resident((1, LANES)),
        ],
        out_specs=[
            pl.BlockSpec((tm, D_MODEL), row),
            pl.BlockSpec((tm, D_MODEL), row),
            pl.BlockSpec((tm, LANES), row),
            pl.BlockSpec((tm, LANES), row),
            pl.BlockSpec((1, LANES), const),
        ],
        out_shape=[
            jax.ShapeDtypeStruct((T, D_MODEL), F32),
            jax.ShapeDtypeStruct((T, D_MODEL), F32),
            jax.ShapeDtypeStruct((T, LANES), jnp.int32),
            jax.ShapeDtypeStruct((T, LANES), F32),
            jax.ShapeDtypeStruct((1, LANES), F32),
        ],
        scratch_shapes=[pltpu.VMEM((1, LANES), F32)],
        compiler_params=_cparams(("arbitrary",), 56),
        name="merge_router",
    )(o_fox, o_swa, o_x, gates, gates, gates, x2, wf, ws, wx, wo, gn, wr_hi, wr_lo, br)


def _dispatch_kernel(start_ref, cnt_ref, e_ref, r_ref, hn_ref, xs_ref, zero_ref, sem):
    i = pl.program_id(0)
    tb = DISPATCH_TB

    def row_copy(src_ref, t, dst_row):
        return pltpu.make_async_copy(src_ref.at[pl.ds(t, 1), :], xs_ref.at[pl.ds(dst_row, 1), :], sem)

    @pl.when(i == 0)
    def _():
        zero_ref[...] = jnp.zeros_like(zero_ref)

        def per_expert(e, carry):
            base = start_ref[e]
            n = cnt_ref[e]
            end = ((n + MOE_BM - 1) // MOE_BM) * MOE_BM

            def fill(r, c):
                row_copy(zero_ref, 0, base + r).start()
                return c

            def drain(r, c):
                row_copy(zero_ref, 0, base + r).wait()
                return c

            lax.fori_loop(n, end, fill, 0)
            lax.fori_loop(n, end, drain, 0)
            return carry

        lax.fori_loop(0, N_EXPERTS, per_expert, 0)

        last = N_EXPERTS - 1
        used = start_ref[last] + ((cnt_ref[last] + MOE_BM - 1) // MOE_BM) * MOE_BM
        zr = zero_ref.shape[0]

        def tail_copy(r):
            return pltpu.make_async_copy(zero_ref, xs_ref.at[pl.ds(pl.multiple_of(used + r * zr, zr), zr), :], sem)

        def tail_fill(r, c):
            tail_copy(r).start()
            return c

        def tail_drain(r, c):
            tail_copy(r).wait()
            return c

        n_tail = (xs_ref.shape[0] - used) // zr
        lax.fori_loop(0, n_tail, tail_fill, 0)
        lax.fori_loop(0, n_tail, tail_drain, 0)

    def token_copy(t, k):
        a = t * TOP_K + k
        return row_copy(hn_ref, t, start_ref[e_ref[a]] + r_ref[a])

    def issue(t, carry):
        for k in range(TOP_K):
            token_copy(t, k).start()
        return carry

    def drain(t, carry):
        for k in range(TOP_K):
            token_copy(t, k).wait()
        return carry

    lax.fori_loop(0, tb, issue, 0)
    lax.fori_loop(0, tb, drain, 0)


def _dispatch(pad_start, cnt, e_flat, r_flat, hn, n_slots):
    T, D = hn.shape
    tb = DISPATCH_TB
    grid_spec = pltpu.PrefetchScalarGridSpec(
        num_scalar_prefetch=2,
        grid=(T // tb,),
        in_specs=[
            pl.BlockSpec((tb * TOP_K,), lambda i, s, c: (i,), memory_space=pltpu.SMEM),
            pl.BlockSpec((tb * TOP_K,), lambda i, s, c: (i,), memory_space=pltpu.SMEM),
            pl.BlockSpec((tb, D), lambda i, s, c: (i, 0)),
        ],
        out_specs=pl.BlockSpec(memory_space=pl.ANY),
        scratch_shapes=[pltpu.VMEM((MOE_BM // 2, D), F32), pltpu.SemaphoreType.DMA(())],
    )
    return pl.pallas_call(
        _dispatch_kernel,
        grid_spec=grid_spec,
        out_shape=jax.ShapeDtypeStruct((n_slots, D), F32),
        compiler_params=_cparams(("arbitrary",), 32),
        name="dispatch",
    )(pad_start, cnt, e_flat, r_flat, hn)


def _expert_changed(be_ref, i):
    return (i == 0) | (be_ref[i] != be_ref[jnp.maximum(i - 1, 0)])


def _for_covering_rows(valid, bm, fn):
    for rows in range(MOE_SUB, bm + 1, MOE_SUB):
        @pl.when((valid > rows - MOE_SUB) & (valid <= rows))
        def _(rows=rows):
            fn(rows)


def _gateup_kernel(be_ref, nu_ref, bv_ref, xs_ref, wg_ref, wu_ref, bg_ref, bu_ref, o_ref, wg_s, wu_s):
    i = pl.program_id(1)
    bm = o_ref.shape[0]
    valid = bv_ref[i]

    @pl.when((valid > 0) & _expert_changed(be_ref, i))
    def _():
        wg_s[...] = wg_ref[0].astype(BF16)
        wu_s[...] = wu_ref[0].astype(BF16)

    def compute(rows):
        x = xs_ref[0:rows, :].astype(BF16)
        gate = _dot(x, wg_s[...]) + bg_ref[0]
        up = _dot(x, wu_s[...]) + bu_ref[0]
        gate = jnp.minimum(gate, SWIGLU_LIMIT)
        up = jnp.clip(up, -SWIGLU_LIMIT, SWIGLU_LIMIT)
        glu = gate * jax.nn.sigmoid(gate * SWIGLU_ALPHA)
        o_ref[0:rows, :] = ((up + 1.0) * glu).astype(o_ref.dtype)
        if rows < bm:
            o_ref[rows:bm, :] = jnp.zeros((bm - rows, o_ref.shape[1]), o_ref.dtype)

    _for_covering_rows(valid, bm, compute)

    @pl.when(valid == 0)
    def _():
        o_ref[...] = jnp.zeros_like(o_ref)


def _down_kernel(be_ref, nu_ref, bv_ref, h_ref, w_ref, b_ref, o_ref, w_s):
    i = pl.program_id(1)
    bm = o_ref.shape[0]
    valid = bv_ref[i]

    @pl.when((valid > 0) & _expert_changed(be_ref, i))
    def _():
        w_s[...] = w_ref[0].astype(BF16)

    def compute(rows):
        o_ref[0:rows, :] = _dot(h_ref[0:rows, :], w_s[...]) + b_ref[0]
        if rows < bm:
            o_ref[rows:bm, :] = jnp.zeros((bm - rows, o_ref.shape[1]), o_ref.dtype)

    _for_covering_rows(valid, bm, compute)

    @pl.when(valid == 0)
    def _():
        o_ref[...] = jnp.zeros_like(o_ref)


def _experts(blk_expert, n_used, blk_valid, xs, w_gate_up, b_gate_up, w_down, b_down):
    P = xs.shape[0]
    bm, tn, tn_down = MOE_BM, MOE_TN, MOE_DOWN_TN
    n_blk = P // bm
    d_exp = w_down.shape[1]
    up_off = d_exp // tn

    def blk(i, nu):
        return jnp.minimum(i, nu[0] - 1)

    gateup_spec = pltpu.PrefetchScalarGridSpec(
        num_scalar_prefetch=3,
        grid=(d_exp // tn, n_blk),
        in_specs=[
            pl.BlockSpec((bm, D_MODEL), lambda j, i, be, nu, bv: (blk(i, nu), 0)),
            pl.BlockSpec((1, D_MODEL, tn), lambda j, i, be, nu, bv: (be[blk(i, nu)], 0, j)),
            pl.BlockSpec((1, D_MODEL, tn), lambda j, i, be, nu, bv: (be[blk(i, nu)], 0, up_off + j)),
            pl.BlockSpec((1, 1, tn), lambda j, i, be, nu, bv: (be[blk(i, nu)], 0, j)),
            pl.BlockSpec((1, 1, tn), lambda j, i, be, nu, bv: (be[blk(i, nu)], 0, up_off + j)),
        ],
        out_specs=pl.BlockSpec((bm, tn), lambda j, i, be, nu, bv: (i, j)),
        scratch_shapes=[pltpu.VMEM((D_MODEL, tn), BF16), pltpu.VMEM((D_MODEL, tn), BF16)],
    )
    hmid = pl.pallas_call(
        _gateup_kernel,
        grid_spec=gateup_spec,
        out_shape=jax.ShapeDtypeStruct((P, d_exp), BF16),
        compiler_params=_cparams(("arbitrary", "arbitrary"), 56),
        name="expert_gate_up",
    )(blk_expert, n_used, blk_valid, xs, w_gate_up, w_gate_up, b_gate_up, b_gate_up)

    down_spec = pltpu.PrefetchScalarGridSpec(
        num_scalar_prefetch=3,
        grid=(D_MODEL // tn_down, n_blk),
        in_specs=[
            pl.BlockSpec((bm, d_exp), lambda j, i, be, nu, bv: (blk(i, nu), 0)),
            pl.BlockSpec((1, d_exp, tn_down), lambda j, i, be, nu, bv: (be[blk(i, nu)], 0, j)),
            pl.BlockSpec((1, 1, tn_down), lambda j, i, be, nu, bv: (be[blk(i, nu)], 0, j)),
        ],
        out_specs=pl.BlockSpec((bm, tn_down), lambda j, i, be, nu, bv: (i, j)),
        scratch_shapes=[pltpu.VMEM((d_exp, tn_down), BF16)],
    )
    return pl.pallas_call(
        _down_kernel,
        grid_spec=down_spec,
        out_shape=jax.ShapeDtypeStruct((P, D_MODEL), F32),
        compiler_params=_cparams(("arbitrary", "arbitrary"), 58),
        name="expert_down",
    )(blk_expert, n_used, blk_valid, hmid, w_down, b_down)


def _combine_kernel(start_ref, e_ref, r_ref, h_ref, g_ref, ys_ref, o_ref, buf, sem):
    tb = COMBINE_TB

    def row_copy(t, k):
        a = t * TOP_K + k
        src_row = start_ref[e_ref[a]] + r_ref[a]
        return pltpu.make_async_copy(ys_ref.at[pl.ds(src_row, 1), :], buf.at[k, pl.ds(t, 1), :], sem)

    def issue(t, carry):
        for k in range(TOP_K):
            row_copy(t, k).start()
        return carry

    def drain(t, carry):
        for k in range(TOP_K):
            row_copy(t, k).wait()
        return carry

    lax.fori_loop(0, tb, issue, 0)
    lax.fori_loop(0, tb, drain, 0)
    g = g_ref[...]
    acc = h_ref[...]
    for k in range(TOP_K):
        acc = acc + g[:, k:k + 1] * buf[k]
    o_ref[...] = acc


def _combine(pad_start, e_flat, r_flat, h1, gates, ys):
    T = h1.shape[0]
    tb = COMBINE_TB
    grid_spec = pltpu.PrefetchScalarGridSpec(
        num_scalar_prefetch=1,
        grid=(T // tb,),
        in_specs=[
            pl.BlockSpec((tb * TOP_K,), lambda i, s: (i,), memory_space=pltpu.SMEM),
            pl.BlockSpec((tb * TOP_K,), lambda i, s: (i,), memory_space=pltpu.SMEM),
            pl.BlockSpec((tb, D_MODEL), lambda i, s: (i, 0)),
            pl.BlockSpec((tb, LANES), lambda i, s: (i, 0)),
            pl.BlockSpec(memory_space=pl.ANY),
        ],
        out_specs=pl.BlockSpec((tb, D_MODEL), lambda i, s: (i, 0)),
        scratch_shapes=[pltpu.VMEM((TOP_K, tb, D_MODEL), F32), pltpu.SemaphoreType.DMA(())],
    )
    return pl.pallas_call(
        _combine_kernel,
        grid_spec=grid_spec,
        out_shape=jax.ShapeDtypeStruct((T, D_MODEL), F32),
        compiler_params=_cparams(("arbitrary",), 32),
        name="combine",
    )(pad_start, e_flat, r_flat, h1, gates, ys)


def _pad_lanes(a, width=LANES):
    return jnp.pad(a, ((0, 0), (0, width - a.shape[1])))


def _layer(h, mem, positions, norm_mix, w_in, b_forget, fox_q_norm, fox_k_norm, swa_q_norm, swa_k_norm,
           swa_sinks, xmem_q_norm, xmem_k_norm, norm_mem, w_mem_kv, w_up_fox, w_up_swa, w_up_xmem, w_out,
           norm_ffn, w_router, b_router, w_gate_up, b_gate_up, w_down, b_down):
    B, S, D = h.shape
    M = mem.shape[1]
    T = B * S
    x2 = h.reshape(T, D)

    attn_end = 3 * FOX_W + FOX_HEADS
    gate_start = attn_end + SWA_QW + 2 * SWA_KVW + XMEM_W
    head_order = np.asarray(SWA_HEAD_ORDER)
    w_t = jnp.transpose(w_in)
    w_sq_t = w_t[attn_end:attn_end + SWA_QW].reshape(SWA_HEADS, HEAD_DIM, D)[head_order].reshape(SWA_QW, D)
    w_attn = _transpose_cast(jnp.concatenate([w_t[:3 * FOX_W], w_sq_t, w_t[attn_end + SWA_QW:gate_start]], axis=0))
    w_gates = _transpose_cast(w_t[gate_start:])
    w_up_swa = w_up_swa.reshape(SWA_HEADS, HEAD_DIM, D)[head_order].reshape(SWA_QW, D)
    swa_sinks = swa_sinks[head_order]
    wf = _pad_lanes(jnp.transpose(w_t[3 * FOX_W:attn_end]))
    wf_hi, wf_lo = _split2(wf)
    bf = _pad_lanes(b_forget.reshape(1, FOX_HEADS))

    def tile_gain(g, reps):
        return jnp.tile(g.reshape(1, -1), (1, reps))

    xn, c = _prenorm(x2, norm_mix.reshape(1, D), wf_hi, wf_lo, bf, S)
    proj = _matmul(xn, w_attn, F32, False, "in_proj_attn")
    gates = _matmul(xn, w_gates, BF16, True, "in_proj_gates")

    qa, ka, vb = _fox_prep(proj, c, tile_gain(fox_q_norm, FOX_HEADS), tile_gain(fox_k_norm, FOX_HEADS))
    o_fox = _fox_attn(qa, ka, vb, B, S)

    o_swa = _swa_attn(proj, positions.reshape(T, 1), swa_sinks.astype(F32),
                      tile_gain(swa_q_norm, SWA_HEADS), tile_gain(swa_k_norm, SWA_KV_HEADS), B, S)

    mk, mv = _mem_kv(mem.reshape(B * M, D), norm_mem.reshape(1, D), w_mem_kv.astype(BF16),
                     xmem_k_norm.reshape(1, XMEM_HEAD_DIM))
    o_x = _mem_attn(proj, mk, mv, xmem_q_norm.reshape(1, XMEM_HEAD_DIM), S, M)

    wr = _pad_lanes(w_router)
    wr_hi, wr_lo = _split2(wr)
    h1, hn, route_i, route_g, counts = _merge(
        o_fox, o_swa, o_x, gates, x2, w_up_fox.astype(BF16), w_up_swa.astype(BF16), w_up_xmem.astype(BF16),
        w_out.astype(BF16), norm_ffn.reshape(1, D), wr_hi, wr_lo, _pad_lanes(b_router.reshape(1, N_EXPERTS)))

    bm = MOE_BM
    cnt = counts[0, :N_EXPERTS].astype(jnp.int32)
    padded = ((cnt + bm - 1) // bm) * bm
    pad_end = jnp.cumsum(padded)
    pad_start = (pad_end - padded).astype(jnp.int32)
    P = T * TOP_K + N_EXPERTS * bm
    n_blk = P // bm
    blk_first = jnp.arange(n_blk, dtype=jnp.int32) * bm
    blk_expert = jnp.minimum(jnp.sum((pad_end[None, :] <= blk_first[:, None]).astype(jnp.int32), axis=1),
                             N_EXPERTS - 1)
    n_used = (pad_end[-1:] // bm).astype(jnp.int32)
    is_blk_expert = blk_expert[:, None] == jnp.arange(N_EXPERTS, dtype=jnp.int32)[None, :]
    tokens_end = jnp.sum(jnp.where(is_blk_expert, (pad_start + cnt)[None, :], 0), axis=1)
    blk_valid = jnp.where(blk_first < pad_end[-1], jnp.clip(tokens_end - blk_first, 0, bm), 0).astype(jnp.int32)
    e_flat = route_i[:, :TOP_K].reshape(T * TOP_K)
    r_flat = route_i[:, TOP_K:2 * TOP_K].reshape(T * TOP_K)

    xs = _dispatch(pad_start, cnt, e_flat, r_flat, hn, P)
    ys = _experts(blk_expert, n_used, blk_valid, xs, w_gate_up, b_gate_up.reshape(N_EXPERTS, 1, -1),
                  w_down, b_down.reshape(N_EXPERTS, 1, -1))
    out = _combine(pad_start, e_flat, r_flat, h1, route_g, ys)
    return out.reshape(B, S, D)


def kernel(x, mem, positions, norm_mix, w_in, b_forget, fox_q_norm, fox_k_norm, swa_q_norm, swa_k_norm, swa_sinks, xmem_q_norm, xmem_k_norm, norm_mem, w_mem_kv, w_up_fox, w_up_swa, w_up_xmem, w_out, norm_ffn, w_router, b_router, w_gate_up, b_gate_up, w_down, b_down):
    h = x
    for layer in range(norm_mix.shape[0]):
        h = _layer(
            h, mem, positions, norm_mix[layer], w_in[layer], b_forget[layer],
            fox_q_norm[layer], fox_k_norm[layer], swa_q_norm[layer], swa_k_norm[layer],
            swa_sinks[layer], xmem_q_norm[layer], xmem_k_norm[layer], norm_mem[layer],
            w_mem_kv[layer], w_up_fox[layer], w_up_swa[layer], w_up_xmem[layer],
            w_out[layer], norm_ffn[layer], w_router[layer], b_router[layer],
            w_gate_up[layer], b_gate_up[layer], w_down[layer], b_down[layer])
    return h
```

```python
import functools

import numpy as np
import jax
import jax.numpy as jnp
from jax import lax
from jax.experimental import pallas as pl
from jax.experimental.pallas import tpu as pltpu

F32 = jnp.float32
BF16 = jnp.bfloat16

D_MODEL = 2048
HEAD_DIM = 64
FOX_HEADS = 12
SWA_HEADS = 12
SWA_KV_HEADS = 4
SWA_GROUP = SWA_HEADS // SWA_KV_HEADS
WINDOW = 128
XMEM_HEADS = 4
XMEM_HEAD_DIM = 128
N_EXPERTS = 32
TOP_K = 4
SWIGLU_LIMIT = 7.0
SWIGLU_ALPHA = 1.702
ROPE_THETA = 10000.0
NORM_EPS = 1e-6

FOX_W = FOX_HEADS * HEAD_DIM
SWA_QW = SWA_HEADS * HEAD_DIM
SWA_KVW = SWA_KV_HEADS * HEAD_DIM
XMEM_W = XMEM_HEADS * XMEM_HEAD_DIM
FOX_PAIRS = FOX_HEADS // 2

LANES = 128
NEG_BIG = -1e30
LOG2E = 1.4426950408889634
MIB = 1024 * 1024

PRE_TM = 512
MM_TM = 1024
MM_TN = 1024
PREP_TM = 512
FOX_TQ = 512
FOX_TK = 512
MEM_TQ = 512
MERGE_TM = 256
MOE_BM = 512
MOE_SUB = 128
MOE_TN = 1024
MOE_DOWN_TN = 2048
WEIGHT_CAST_ROWS = 256
DISPATCH_TB = 256
COMBINE_TB = 128


def _cparams(semantics, vmem_mib):
    return pltpu.CompilerParams(dimension_semantics=semantics, vmem_limit_bytes=vmem_mib * MIB)


def _dot(a, b):
    return jnp.dot(a, b, preferred_element_type=F32)


def _dot_nt(a, b):
    return lax.dot_general(a, b, (((1,), (1,)), ((), ())), preferred_element_type=F32)


def _split2(x):
    hi = x.astype(BF16)
    lo = (x - hi.astype(F32)).astype(BF16)
    return hi, lo


def _split3(x):
    hi = x.astype(BF16)
    r = x - hi.astype(F32)
    mid = r.astype(BF16)
    lo = (r - mid.astype(F32)).astype(BF16)
    return hi, mid, lo


def _prenorm_kernel(x_ref, g_ref, wf_hi_ref, wf_lo_ref, bf_ref, xn_ref, c_ref, carry_ref, *, blocks_per_seq):
    i = pl.program_id(0)

    @pl.when(i % blocks_per_seq == 0)
    def _():
        carry_ref[...] = jnp.zeros_like(carry_ref)

    x = x_ref[...]
    tm = x.shape[0]
    ms = jnp.mean(x * x, axis=-1, keepdims=True)
    xn = x * lax.rsqrt(ms + NORM_EPS) * g_ref[...]
    xn_hi, xn_lo = _split2(xn)
    xn_ref[...] = xn_hi
    wf_hi = wf_hi_ref[...]
    z = _dot(xn_hi, wf_hi) + _dot(xn_lo, wf_hi) + _dot(xn_hi, wf_lo_ref[...]) + bf_ref[...]
    logf = jnp.minimum(z, 0.0) - jnp.log1p(jnp.exp(-jnp.abs(z)))
    lane = lax.broadcasted_iota(jnp.int32, logf.shape, 1)
    logf = jnp.where(lane < FOX_HEADS, logf, 0.0)
    hi, mid, lo = _split3(logf)
    r = lax.broadcasted_iota(jnp.int32, (tm, tm), 0)
    cc = lax.broadcasted_iota(jnp.int32, (tm, tm), 1)
    tri = jnp.where(cc <= r, 1.0, 0.0).astype(BF16)
    c = _dot(tri, hi) + _dot(tri, mid) + _dot(tri, lo) + carry_ref[...]
    c_ref[...] = c
    carry_ref[...] = c[tm - 1:tm, :]


def _prenorm(x2, gain, wf_hi, wf_lo, bf, seq):
    T = x2.shape[0]
    tm = PRE_TM
    return pl.pallas_call(
        functools.partial(_prenorm_kernel, blocks_per_seq=seq // tm),
        grid=(T // tm,),
        in_specs=[
            pl.BlockSpec((tm, D_MODEL), lambda i: (i, 0)),
            pl.BlockSpec((1, D_MODEL), lambda i: (0, 0)),
            pl.BlockSpec((D_MODEL, LANES), lambda i: (0, 0)),
            pl.BlockSpec((D_MODEL, LANES), lambda i: (0, 0)),
            pl.BlockSpec((1, LANES), lambda i: (0, 0)),
        ],
        out_specs=[
            pl.BlockSpec((tm, D_MODEL), lambda i: (i, 0)),
            pl.BlockSpec((tm, LANES), lambda i: (i, 0)),
        ],
        out_shape=[
            jax.ShapeDtypeStruct((T, D_MODEL), BF16),
            jax.ShapeDtypeStruct((T, LANES), F32),
        ],
        scratch_shapes=[pltpu.VMEM((1, LANES), F32)],
        compiler_params=_cparams(("arbitrary",), 40),
        name="prenorm",
    )(x2, gain, wf_hi, wf_lo, bf)


def _mm_kernel(x_ref, w_ref, o_ref, *, sigmoid):
    acc = _dot(x_ref[...], w_ref[...])
    if sigmoid:
        acc = jax.nn.sigmoid(acc)
    o_ref[...] = acc.astype(o_ref.dtype)


def _matmul(x, w, out_dtype, sigmoid, name):
    M, K = x.shape
    N = w.shape[1]
    tm, tn = MM_TM, MM_TN
    return pl.pallas_call(
        functools.partial(_mm_kernel, sigmoid=sigmoid),
        grid=(M // tm, N // tn),
        in_specs=[
            pl.BlockSpec((tm, K), lambda i, j: (i, 0)),
            pl.BlockSpec((K, tn), lambda i, j: (0, j)),
        ],
        out_specs=pl.BlockSpec((tm, tn), lambda i, j: (i, j)),
        out_shape=jax.ShapeDtypeStruct((M, N), out_dtype),
        compiler_params=_cparams(("parallel", "parallel"), 48),
        name=name,
    )(x, w)


def _transpose_cast_kernel(w_ref, o_ref):
    o_ref[...] = w_ref[...].T.astype(o_ref.dtype)


def _transpose_cast(w_t):
    N, K = w_t.shape
    tn = 512
    return pl.pallas_call(
        _transpose_cast_kernel,
        grid=(N // tn,),
        in_specs=[pl.BlockSpec((tn, K), lambda i: (i, 0))],
        out_specs=pl.BlockSpec((K, tn), lambda i: (0, i)),
        out_shape=jax.ShapeDtypeStruct((K, N), BF16),
        compiler_params=_cparams(("parallel",), 32),
        name="weight_transpose_cast",
    )(w_t)


def _foxprep_kernel(q_ref, k_ref, v_ref, c_ref, gq_ref, gk_ref, bd_ref, selq_ref, selk_ref,
                    onesq_ref, onesk_ref, qa_ref, ka_ref, vt_ref):
    bd = bd_ref[...]

    def headnorm(x, g):
        hi, lo = _split2(x * x)
        ssq = _dot(hi, bd) + _dot(lo, bd)
        return x * lax.rsqrt(ssq * (1.0 / HEAD_DIM) + NORM_EPS) * g

    qn = headnorm(q_ref[...], gq_ref[...]) * (HEAD_DIM ** -0.5 * LOG2E)
    kn = headnorm(k_ref[...], gk_ref[...])
    v = v_ref[...]
    hi, mid, lo = _split3(c_ref[...] * LOG2E)
    c3 = jnp.concatenate([hi, mid, lo], axis=1)
    lane = lax.broadcasted_iota(jnp.int32, (qn.shape[0], LANES), 1)
    ones_col = jnp.where(lane == HEAD_DIM, 1.0, 0.0)
    for p in range(FOX_PAIRS):
        qp = qn[:, p * LANES:(p + 1) * LANES]
        vp = v[:, p * LANES:(p + 1) * LANES]
        for j in range(2):
            h = 2 * p + j
            keep = (lane < HEAD_DIM) if j == 0 else (lane >= HEAD_DIM)
            qa_ref[h, :, 0:LANES] = jnp.where(keep, qp, 0.0).astype(BF16)
            qa_ref[h, :, LANES:2 * LANES] = (_dot(c3, selq_ref[h]) + onesq_ref[h]).astype(BF16)
            vh = vp if j == 0 else pltpu.roll(vp, HEAD_DIM, axis=1)
            vt_ref[h, 0] = jnp.where(lane < HEAD_DIM, vh, ones_col).T.astype(BF16)
        ka_ref[p, :, 0:LANES] = kn[:, p * LANES:(p + 1) * LANES].astype(BF16)
        ka_ref[p, :, LANES:2 * LANES] = (_dot(c3, selk_ref[p]) + onesk_ref[...]).astype(BF16)


def _fox_tables():
    selq = np.zeros((FOX_HEADS, 3 * LANES, LANES), np.float32)
    selk = np.zeros((FOX_PAIRS, 3 * LANES, LANES), np.float32)
    onesq = np.zeros((FOX_HEADS, 1, LANES), np.float32)
    onesk = np.zeros((1, LANES), np.float32)
    for h in range(FOX_HEADS):
        p, j = divmod(h, 2)
        for piece in range(3):
            selq[h, piece * LANES + h, 6 * j + piece] = 1.0
            selk[p, piece * LANES + h, 6 * j + 3 + piece] = -1.0
            onesq[h, 0, 6 * j + 3 + piece] = 1.0
            onesk[0, 6 * j + piece] = 1.0
    bd = np.kron(np.eye(FOX_HEADS, dtype=np.float32), np.ones((HEAD_DIM, HEAD_DIM), np.float32))
    return (jnp.asarray(bd, BF16), jnp.asarray(selq, BF16), jnp.asarray(selk, BF16),
            jnp.asarray(onesq, F32), jnp.asarray(onesk, F32))


def _fox_prep(proj, c, gq, gk):
    T = proj.shape[0]
    tm = PREP_TM
    bd, selq, selk, onesq, onesk = _fox_tables()
    const2 = lambda i: (0, 0)
    const3 = lambda i: (0, 0, 0)
    return pl.pallas_call(
        _foxprep_kernel,
        grid=(T // tm,),
        in_specs=[
            pl.BlockSpec((tm, FOX_W), lambda i: (i, 0)),
            pl.BlockSpec((tm, FOX_W), lambda i: (i, 1)),
            pl.BlockSpec((tm, FOX_W), lambda i: (i, 2)),
            pl.BlockSpec((tm, LANES), lambda i: (i, 0)),
            pl.BlockSpec((1, FOX_W), const2),
            pl.BlockSpec((1, FOX_W), const2),
            pl.BlockSpec((FOX_W, FOX_W), const2),
            pl.BlockSpec((FOX_HEADS, 3 * LANES, LANES), const3),
            pl.BlockSpec((FOX_PAIRS, 3 * LANES, LANES), const3),
            pl.BlockSpec((FOX_HEADS, 1, LANES), const3),
            pl.BlockSpec((1, LANES), const2),
        ],
        out_specs=[
            pl.BlockSpec((FOX_HEADS, tm, 2 * LANES), lambda i: (0, i, 0)),
            pl.BlockSpec((FOX_PAIRS, tm, 2 * LANES), lambda i: (0, i, 0)),
            pl.BlockSpec((FOX_HEADS, 1, LANES, tm), lambda i: (0, i, 0, 0)),
        ],
        out_shape=[
            jax.ShapeDtypeStruct((FOX_HEADS, T, 2 * LANES), BF16),
            jax.ShapeDtypeStruct((FOX_PAIRS, T, 2 * LANES), BF16),
            jax.ShapeDtypeStruct((FOX_HEADS, T // tm, LANES, tm), BF16),
        ],
        compiler_params=_cparams(("parallel",), 48),
        name="fox_prep",
    )(proj, proj, proj, c, gq, gk, bd, selq, selk, onesq, onesk)


def _fox_attn_kernel(q_ref, k_ref, vt_ref, o_ref, m_ref, acc_ref, sa_ref, sb_ref):
    qi = pl.program_id(2)
    tk = FOX_TK
    m_ref[...] = jnp.full(m_ref.shape, NEG_BIG, F32)
    acc_ref[...] = jnp.zeros(acc_ref.shape, F32)

    def scores(kb, s_ref):
        off = pl.multiple_of(kb * tk, tk)
        k = k_ref[pl.ds(off, tk), :]
        for j in range(2):
            s_ref[j] = _dot_nt(k, q_ref[j])

    def softmax_pv(kb, s_ref, masked):
        for j in range(2):
            s = s_ref[j]
            if masked:
                key = lax.broadcasted_iota(jnp.int32, s.shape, 0)
                qry = lax.broadcasted_iota(jnp.int32, s.shape, 1)
                s = jnp.where(key <= qry, s, NEG_BIG)
            m_old = m_ref[j]
            m_new = jnp.maximum(m_old, jnp.max(s, axis=0, keepdims=True))
            alpha = jnp.exp2(m_old - m_new)
            p = jnp.exp2(s - m_new)
            acc_ref[j] = alpha * acc_ref[j] + _dot(vt_ref[j, kb], p.astype(BF16))
            m_ref[j] = m_new

    scores(0, sa_ref)

    def two_blocks(i, carry):
        scores(2 * i + 1, sb_ref)
        softmax_pv(2 * i, sa_ref, False)
        scores(2 * i + 2, sa_ref)
        softmax_pv(2 * i + 1, sb_ref, False)
        return carry

    lax.fori_loop(0, qi // 2, two_blocks, 0)

    @pl.when(qi % 2 == 0)
    def _():
        softmax_pv(qi, sa_ref, True)

    @pl.when(qi % 2 == 1)
    def _():
        scores(qi, sb_ref)
        softmax_pv(qi - 1, sa_ref, False)
        softmax_pv(qi, sb_ref, True)

    outs = [acc_ref[j, 0:HEAD_DIM, :] / acc_ref[j, HEAD_DIM:HEAD_DIM + 1, :] for j in range(2)]
    o_ref[...] = jnp.concatenate(outs, axis=0).T.astype(o_ref.dtype)


def _fox_attn(qa, ka, vt, batch, seq):
    T = qa.shape[1]
    tq, tk = FOX_TQ, FOX_TK
    assert tq == tk == PREP_TM
    nq = seq // tq
    nk = seq // tk
    qa4 = qa.reshape(FOX_PAIRS, 2, T, 2 * LANES)
    vt5 = vt.reshape(FOX_PAIRS, 2, batch * nk, LANES, tk)
    return pl.pallas_call(
        _fox_attn_kernel,
        grid=(batch, FOX_PAIRS, nq),
        in_specs=[
            pl.BlockSpec((None, 2, tq, 2 * LANES), lambda b, p, q: (p, 0, b * nq + q, 0)),
            pl.BlockSpec((None, seq, 2 * LANES), lambda b, p, q: (p, b, 0)),
            pl.BlockSpec((None, 2, nk, LANES, tk), lambda b, p, q: (p, 0, b, 0, 0)),
        ],
        out_specs=pl.BlockSpec((tq, LANES), lambda b, p, q: (b * nq + q, p)),
        out_shape=jax.ShapeDtypeStruct((T, FOX_W), BF16),
        scratch_shapes=[
            pltpu.VMEM((2, 1, tq), F32),
            pltpu.VMEM((2, LANES, tq), F32),
            pltpu.VMEM((2, tk, tq), F32),
            pltpu.VMEM((2, tk, tq), F32),
        ],
        compiler_params=_cparams(("parallel", "parallel", "arbitrary"), 48),
        name="fox_attn",
    )(qa4, ka, vt5)


SWA_HEAD_ORDER = (0, 3, 1, 4, 2, 5, 6, 9, 7, 10, 8, 11)
SWA_KV_TILES = SWA_KV_HEADS // 2
SWA_Q_TILES_PER_KV_TILE = SWA_HEADS // 2 // SWA_KV_TILES


def _swa_kernel(sinks_ref, q_ref, k_ref, v_ref, pos_ref, freq_ref, sign_ref, gq_ref, gk_ref, bdq_ref, bdk_ref,
                o_ref, kprev_ref, vtprev_ref):
    n = pl.program_id(1)
    W = WINDOW
    half = HEAD_DIM // 2

    @pl.when(n == 0)
    def _():
        kprev_ref[...] = jnp.zeros_like(kprev_ref)
        vtprev_ref[...] = jnp.zeros_like(vtprev_ref)

    lane = lax.broadcasted_iota(jnp.int32, (W, LANES), 1)
    ang = pos_ref[...].astype(F32) * freq_ref[...]
    cos1 = jnp.cos(ang)
    sin1 = jnp.sin(ang) * sign_ref[...]
    first_half1 = (lane & half) == 0

    def norm_rope(x, g, bd):
        reps = x.shape[1] // LANES
        hi, lo = _split2(x * x)
        ssq = _dot(hi, bd) + _dot(lo, bd)
        xn = x * lax.rsqrt(ssq * (1.0 / HEAD_DIM) + NORM_EPS) * g
        first_half = jnp.tile(first_half1, (1, reps))
        swapped = jnp.where(first_half, pltpu.roll(xn, x.shape[1] - half, axis=1), pltpu.roll(xn, half, axis=1))
        return xn * jnp.tile(cos1, (1, reps)) + swapped * jnp.tile(sin1, (1, reps))

    q = norm_rope(q_ref[...], gq_ref[...], bdq_ref[...]) * (HEAD_DIM ** -0.5 * LOG2E)
    k = norm_rope(k_ref[...], gk_ref[...], bdk_ref[...]).astype(BF16)
    v = v_ref[...]

    key = lax.broadcasted_iota(jnp.int32, (2 * W, 2 * W), 0)
    qry = lax.broadcasted_iota(jnp.int32, (2 * W, 2 * W), 1) & (W - 1)
    visible = ((key < W) & (key > qry) & (n > 0)) | ((key >= W) & (key - W <= qry))
    low_q = lax.broadcasted_iota(jnp.int32, (1, 2 * W), 1) < W

    for kt in range(SWA_KV_TILES):
        k_cur = k[:, kt * LANES:(kt + 1) * LANES]
        vt_cur = v[:, kt * LANES:(kt + 1) * LANES].T.astype(BF16)
        keys = jnp.concatenate([kprev_ref[kt], k_cur], axis=0)
        vt = jnp.concatenate([vtprev_ref[kt], vt_cur], axis=1)
        for r in range(SWA_Q_TILES_PER_KV_TILE):
            t = kt * SWA_Q_TILES_PER_KV_TILE + r
            qt = q[:, t * LANES:(t + 1) * LANES]
            qs = jnp.concatenate([jnp.where(lane < HEAD_DIM, qt, 0.0), jnp.where(lane >= HEAD_DIM, qt, 0.0)],
                                 axis=0).astype(BF16)
            s = jnp.where(visible, _dot_nt(keys, qs), NEG_BIG)
            sink = jnp.where(low_q, sinks_ref[2 * t], sinks_ref[2 * t + 1]) * LOG2E
            m = jnp.maximum(jnp.max(s, axis=0, keepdims=True), sink)
            p = jnp.exp2(s - m)
            denom = jnp.sum(p, axis=0, keepdims=True) + jnp.exp2(sink - m)
            ot = _dot(vt, p.astype(BF16)) / denom
            pair = jnp.concatenate([ot[0:HEAD_DIM, 0:W], ot[HEAD_DIM:LANES, W:2 * W]], axis=0)
            o_ref[:, t * LANES:(t + 1) * LANES] = pair.T.astype(o_ref.dtype)
        kprev_ref[kt] = k_cur
        vtprev_ref[kt] = vt_cur


def _swa_attn(proj, pos2, sinks, gq, gk, batch, seq):
    T = proj.shape[0]
    W = WINDOW
    assert W == LANES
    nb = seq // W
    half = HEAD_DIM // 2
    inv_freq = np.power(ROPE_THETA, -np.arange(0, HEAD_DIM, 2, dtype=np.float32) / HEAD_DIM).astype(np.float32)
    freq = np.tile(inv_freq, LANES // half).reshape(1, LANES)
    sign = np.tile(np.concatenate([-np.ones(half, np.float32), np.ones(half, np.float32)]),
                   LANES // HEAD_DIM).reshape(1, LANES)
    head_ones = np.ones((HEAD_DIM, HEAD_DIM), np.float32)
    bdq = jnp.asarray(np.kron(np.eye(SWA_HEADS, dtype=np.float32), head_ones), BF16)
    bdk = jnp.asarray(np.kron(np.eye(SWA_KV_HEADS, dtype=np.float32), head_ones), BF16)
    q_col = (3 * FOX_W) // SWA_QW
    k_col = (3 * FOX_W + SWA_QW) // SWA_KVW
    const2 = lambda b, n, s: (0, 0)
    grid_spec = pltpu.PrefetchScalarGridSpec(
        num_scalar_prefetch=1,
        grid=(batch, nb),
        in_specs=[
            pl.BlockSpec((W, SWA_QW), lambda b, n, s: (b * nb + n, q_col)),
            pl.BlockSpec((W, SWA_KVW), lambda b, n, s: (b * nb + n, k_col)),
            pl.BlockSpec((W, SWA_KVW), lambda b, n, s: (b * nb + n, k_col + 1)),
            pl.BlockSpec((W, 1), lambda b, n, s: (b * nb + n, 0)),
            pl.BlockSpec((1, LANES), const2),
            pl.BlockSpec((1, LANES), const2),
            pl.BlockSpec((1, SWA_QW), const2),
            pl.BlockSpec((1, SWA_KVW), const2),
            pl.BlockSpec((SWA_QW, SWA_QW), const2),
            pl.BlockSpec((SWA_KVW, SWA_KVW), const2),
        ],
        out_specs=pl.BlockSpec((W, SWA_QW), lambda b, n, s: (b * nb + n, 0)),
        scratch_shapes=[
            pltpu.VMEM((SWA_KV_TILES, W, LANES), BF16),
            pltpu.VMEM((SWA_KV_TILES, LANES, W), BF16),
        ],
    )
    return pl.pallas_call(
        _swa_kernel,
        grid_spec=grid_spec,
        out_shape=jax.ShapeDtypeStruct((T, SWA_QW), BF16),
        compiler_params=_cparams(("parallel", "arbitrary"), 32),
        name="swa_attn",
    )(sinks, proj, proj, proj, pos2, jnp.asarray(freq), jnp.asarray(sign), gq, gk, bdq, bdk)


def _memkv_kernel(mem_ref, g_ref, w_ref, gk_ref, k_ref, v_ref):
    x = mem_ref[...]
    ms = jnp.mean(x * x, axis=-1, keepdims=True)
    mn = (x * lax.rsqrt(ms + NORM_EPS) * g_ref[...]).astype(BF16)
    kv = _dot(mn, w_ref[...])
    for h in range(XMEM_HEADS):
        kh = kv[:, h * LANES:(h + 1) * LANES]
        ms_h = jnp.mean(kh * kh, axis=-1, keepdims=True)
        k_ref[:, h * LANES:(h + 1) * LANES] = (kh * lax.rsqrt(ms_h + NORM_EPS) * gk_ref[...]).astype(BF16)
    v_ref[...] = kv[:, XMEM_W:].astype(BF16)


def _mem_kv(mem2, gain, w, gk):
    R = mem2.shape[0]
    tm = 256
    return pl.pallas_call(
        _memkv_kernel,
        grid=(R // tm,),
        in_specs=[
            pl.BlockSpec((tm, D_MODEL), lambda i: (i, 0)),
            pl.BlockSpec((1, D_MODEL), lambda i: (0, 0)),
            pl.BlockSpec((D_MODEL, 2 * XMEM_W), lambda i: (0, 0)),
            pl.BlockSpec((1, XMEM_HEAD_DIM), lambda i: (0, 0)),
        ],
        out_specs=[
            pl.BlockSpec((tm, XMEM_W), lambda i: (i, 0)),
            pl.BlockSpec((tm, XMEM_W), lambda i: (i, 0)),
        ],
        out_shape=[
            jax.ShapeDtypeStruct((R, XMEM_W), BF16),
            jax.ShapeDtypeStruct((R, XMEM_W), BF16),
        ],
        compiler_params=_cparams(("parallel",), 32),
        name="mem_kv",
    )(mem2, gain, w, gk)


def _memattn_kernel(q_ref, k_ref, v_ref, gq_ref, o_ref):
    q = q_ref[...]
    for h in range(XMEM_HEADS):
        sl = slice(h * LANES, (h + 1) * LANES)
        qh = q[:, sl]
        ms = jnp.mean(qh * qh, axis=-1, keepdims=True)
        qn = (qh * lax.rsqrt(ms + NORM_EPS) * gq_ref[...] * (XMEM_HEAD_DIM ** -0.5)).astype(BF16)
        s = _dot_nt(qn, k_ref[:, sl])
        m = jnp.max(s, axis=-1, keepdims=True)
        p = jnp.exp(s - m)
        l = jnp.sum(p, axis=-1, keepdims=True)
        o_ref[:, sl] = (_dot(p.astype(BF16), v_ref[:, sl]) / l).astype(o_ref.dtype)


def _mem_attn(proj, mk, mv, gq, seq, n_mem):
    T = proj.shape[0]
    tq = MEM_TQ
    per_seq = seq // tq
    q_col = (3 * FOX_W + SWA_QW + 2 * SWA_KVW) // XMEM_W
    return pl.pallas_call(
        _memattn_kernel,
        grid=(T // tq,),
        in_specs=[
            pl.BlockSpec((tq, XMEM_W), lambda i: (i, q_col)),
            pl.BlockSpec((n_mem, XMEM_W), lambda i: (i // per_seq, 0)),
            pl.BlockSpec((n_mem, XMEM_W), lambda i: (i // per_seq, 0)),
            pl.BlockSpec((1, XMEM_HEAD_DIM), lambda i: (0, 0)),
        ],
        out_specs=pl.BlockSpec((tq, XMEM_W), lambda i: (i, 0)),
        out_shape=jax.ShapeDtypeStruct((T, XMEM_W), BF16),
        compiler_params=_cparams(("parallel",), 32),
        name="mem_attn",
    )(proj, mk, mv, gq)


def _merge_kernel(of_ref, os_ref, ox_ref, g0_ref, g1_ref, g2_ref, x_ref, wf_ref, ws_ref, wx_ref, wo_ref,
                  gn_ref, wr_hi_ref, wr_lo_ref, br_ref,
                  h_ref, hn_ref, ri_ref, rg_ref, cnt_ref, carry_ref):
    i = pl.program_id(0)

    @pl.when(i == 0)
    def _():
        carry_ref[...] = jnp.zeros_like(carry_ref)

    merged = (g0_ref[...].astype(F32) * _dot(of_ref[...], wf_ref[...])
              + g1_ref[...].astype(F32) * _dot(os_ref[...], ws_ref[...])
              + g2_ref[...].astype(F32) * _dot(ox_ref[...], wx_ref[...]))
    h = x_ref[...] + _dot(merged.astype(BF16), wo_ref[...])
    h_ref[...] = h
    ms = jnp.mean(h * h, axis=-1, keepdims=True)
    hn = h * lax.rsqrt(ms + NORM_EPS) * gn_ref[...]
    hn_hi, hn_lo = _split2(hn)
    hn_ref[...] = hn
    wr_hi = wr_hi_ref[...]
    logits = _dot(hn_hi, wr_hi) + _dot(hn_lo, wr_hi) + _dot(hn_hi, wr_lo_ref[...]) + br_ref[...]
    tm = logits.shape[0]
    lane = lax.broadcasted_iota(jnp.int32, (tm, LANES), 1).astype(F32)
    work = jnp.where(lane < N_EXPERTS, logits, NEG_BIG)
    vals, idxs = [], []
    for _ in range(TOP_K):
        mx = jnp.max(work, axis=-1, keepdims=True)
        ix = jnp.min(jnp.where(work == mx, lane, float(LANES)), axis=-1, keepdims=True)
        vals.append(mx)
        idxs.append(ix)
        work = jnp.where(lane == ix, NEG_BIG, work)
    es = [jnp.exp(v - vals[0]) for v in vals]
    den = es[0] + es[1] + es[2] + es[3]
    onehot = jnp.zeros((tm, LANES), F32)
    for ix in idxs:
        onehot = onehot + jnp.where(lane == ix, 1.0, 0.0)
    r = lax.broadcasted_iota(jnp.int32, (tm, tm), 0)
    cc = lax.broadcasted_iota(jnp.int32, (tm, tm), 1)
    tri = jnp.where(cc < r, 1.0, 0.0).astype(BF16)
    before = _dot(tri, onehot.astype(BF16)) + carry_ref[...]
    ri = jnp.zeros((tm, LANES), jnp.int32)
    rg = jnp.zeros((tm, LANES), F32)
    for k in range(TOP_K):
        rank = jnp.sum(jnp.where(lane == idxs[k], before, 0.0), axis=-1, keepdims=True)
        ri = jnp.where(lane == k, idxs[k].astype(jnp.int32), ri)
        ri = jnp.where(lane == TOP_K + k, rank.astype(jnp.int32), ri)
        rg = jnp.where(lane == k, es[k] / den, rg)
    ri_ref[...] = ri
    rg_ref[...] = rg
    total = carry_ref[...] + jnp.sum(onehot, axis=0, keepdims=True)
    carry_ref[...] = total
    cnt_ref[...] = total


def _merge(o_fox, o_swa, o_x, gates, x2, wf, ws, wx, wo, gn, wr_hi, wr_lo, br):
    T = x2.shape[0]
    tm = MERGE_TM
    row = lambda i: (i, 0)
    const = lambda i: (0, 0)
    resident = functools.partial(pl.BlockSpec, index_map=const, pipeline_mode=pl.Buffered(1))
    return pl.pallas_call(
        _merge_kernel,
        grid=(T // tm,),
        in_specs=[
            pl.BlockSpec((tm, FOX_W), row),
            pl.BlockSpec((tm, SWA_QW), row),
            pl.BlockSpec((tm, XMEM_W), row),
            pl.BlockSpec((tm, D_MODEL), lambda i: (i, 0)),
            pl.BlockSpec((tm, D_MODEL), lambda i: (i, 1)),
            pl.BlockSpec((tm, D_MODEL), lambda i: (i, 2)),
            pl.BlockSpec((tm, D_MODEL), row),
            resident((FOX_W, D_MODEL)),
            resident((SWA_QW, D_MODEL)),
            resident((XMEM_W, D_MODEL)),
            resident((D_MODEL, D_MODEL)),
            resident((1, D_MODEL)),
            resident((D_MODEL, LANES)),
            resident((D_MODEL, LANES)),
            resident((1, LANES)),
        ],
        out_specs=[
            pl.BlockSpec((tm, D_MODEL), row),
            pl.BlockSpec((tm, D_MODEL), row),
            pl.BlockSpec((tm, LANES), row),
            pl.BlockSpec((tm, LANES), row),
            pl.BlockSpec((1, LANES), const),
        ],
        out_shape=[
            jax.ShapeDtypeStruct((T, D_MODEL), F32),
            jax.ShapeDtypeStruct((T, D_MODEL), F32),
            jax.ShapeDtypeStruct((T, LANES), jnp.int32),
            jax.ShapeDtypeStruct((T, LANES), F32),
            jax.ShapeDtypeStruct((1, LANES), F32),
        ],
        scratch_shapes=[pltpu.VMEM((1, LANES), F32)],
        compiler_params=_cparams(("arbitrary",), 56),
        name="merge_router",
    )(o_fox, o_swa, o_x, gates, gates, gates, x2, wf, ws, wx, wo, gn, wr_hi, wr_lo, br)


def _dispatch_kernel(start_ref, cnt_ref, e_ref, r_ref, hn_ref, xs_ref, zero_ref, sem):
    i = pl.program_id(0)
    tb = DISPATCH_TB

    def row_copy(src_ref, t, dst_row):
        return pltpu.make_async_copy(src_ref.at[pl.ds(t, 1), :], xs_ref.at[pl.ds(dst_row, 1), :], sem)

    @pl.when(i == 0)
    def _():
        zero_ref[...] = jnp.zeros_like(zero_ref)

        def per_expert(e, carry):
            base = start_ref[e]
            n = cnt_ref[e]
            end = ((n + MOE_BM - 1) // MOE_BM) * MOE_BM

            def fill(r, c):
                row_copy(zero_ref, 0, base + r).start()
                return c

            def drain(r, c):
                row_copy(zero_ref, 0, base + r).wait()
                return c

            lax.fori_loop(n, end, fill, 0)
            lax.fori_loop(n, end, drain, 0)
            return carry

        lax.fori_loop(0, N_EXPERTS, per_expert, 0)

        last = N_EXPERTS - 1
        used = start_ref[last] + ((cnt_ref[last] + MOE_BM - 1) // MOE_BM) * MOE_BM
        zr = zero_ref.shape[0]

        def tail_copy(r):
            return pltpu.make_async_copy(zero_ref, xs_ref.at[pl.ds(pl.multiple_of(used + r * zr, zr), zr), :], sem)

        def tail_fill(r, c):
            tail_copy(r).start()
            return c

        def tail_drain(r, c):
            tail_copy(r).wait()
            return c

        n_tail = (xs_ref.shape[0] - used) // zr
        lax.fori_loop(0, n_tail, tail_fill, 0)
        lax.fori_loop(0, n_tail, tail_drain, 0)

    def token_copy(t, k):
        a = t * TOP_K + k
        return row_copy(hn_ref, t, start_ref[e_ref[a]] + r_ref[a])

    def issue(t, carry):
        for k in range(TOP_K):
            token_copy(t, k).start()
        return carry

    def drain(t, carry):
        for k in range(TOP_K):
            token_copy(t, k).wait()
        return carry

    lax.fori_loop(0, tb, issue, 0)
    lax.fori_loop(0, tb, drain, 0)


def _dispatch(pad_start, cnt, e_flat, r_flat, hn, n_slots):
    T, D = hn.shape
    tb = DISPATCH_TB
    grid_spec = pltpu.PrefetchScalarGridSpec(
        num_scalar_prefetch=2,
        grid=(T // tb,),
        in_specs=[
            pl.BlockSpec((tb * TOP_K,), lambda i, s, c: (i,), memory_space=pltpu.SMEM),
            pl.BlockSpec((tb * TOP_K,), lambda i, s, c: (i,), memory_space=pltpu.SMEM),
            pl.BlockSpec((tb, D), lambda i, s, c: (i, 0)),
        ],
        out_specs=pl.BlockSpec(memory_space=pl.ANY),
        scratch_shapes=[pltpu.VMEM((MOE_BM // 2, D), F32), pltpu.SemaphoreType.DMA(())],
    )
    return pl.pallas_call(
        _dispatch_kernel,
        grid_spec=grid_spec,
        out_shape=jax.ShapeDtypeStruct((n_slots, D), F32),
        compiler_params=_cparams(("arbitrary",), 32),
        name="dispatch",
    )(pad_start, cnt, e_flat, r_flat, hn)


def _expert_changed(be_ref, i):
    return (i == 0) | (be_ref[i] != be_ref[jnp.maximum(i - 1, 0)])


def _for_covering_rows(valid, bm, fn):
    for rows in range(MOE_SUB, bm + 1, MOE_SUB):
        @pl.when((valid > rows - MOE_SUB) & (valid <= rows))
        def _(rows=rows):
            fn(rows)


def _stream_expert_weights(be_ref, nx_ref, valid, tile_copies, consume):
    j = pl.program_id(0)
    i = pl.program_id(1)
    n_pass = pl.num_programs(0)

    @pl.when((i == 0) & (j == 0))
    def _():
        for cp in tile_copies(be_ref[0], 0):
            cp.start()

    @pl.when((valid > 0) & _expert_changed(be_ref, i))
    def _():
        e = be_ref[i]
        for cp in tile_copies(e, j):
            cp.wait()
        consume()
        e_next = nx_ref[e]
        j_next = j + (e_next <= e).astype(jnp.int32)

        @pl.when(j_next < n_pass)
        def _():
            for cp in tile_copies(e_next, j_next):
                cp.start()


def _gateup_kernel(be_ref, nu_ref, bv_ref, nx_ref, xs_ref, bg_ref, bu_ref, w_hbm, o_ref, wbuf, wg_s, wu_s, sem):
    i = pl.program_id(1)
    bm, tn = o_ref.shape
    up_off = w_hbm.shape[2] // 2 // tn
    valid = bv_ref[i]

    def tile_copies(e, j):
        def one(half):
            col = pl.multiple_of((half * up_off + j) * tn, tn)
            return pltpu.make_async_copy(w_hbm.at[e, :, pl.ds(col, tn)], wbuf.at[half], sem.at[half])
        return [one(0), one(1)]

    def consume():
        for r in range(0, wg_s.shape[0], WEIGHT_CAST_ROWS):
            rows = slice(r, r + WEIGHT_CAST_ROWS)
            wg_s[rows, :] = wbuf[0, rows, :].astype(BF16)
            wu_s[rows, :] = wbuf[1, rows, :].astype(BF16)

    _stream_expert_weights(be_ref, nx_ref, valid, tile_copies, consume)

    def compute(rows):
        x = xs_ref[0:rows, :].astype(BF16)
        gate = _dot(x, wg_s[...]) + bg_ref[0]
        up = _dot(x, wu_s[...]) + bu_ref[0]
        gate = jnp.minimum(gate, SWIGLU_LIMIT)
        up = jnp.clip(up, -SWIGLU_LIMIT, SWIGLU_LIMIT)
        glu = gate * jax.nn.sigmoid(gate * SWIGLU_ALPHA)
        o_ref[0:rows, :] = ((up + 1.0) * glu).astype(o_ref.dtype)
        if rows < bm:
            o_ref[rows:bm, :] = jnp.zeros((bm - rows, o_ref.shape[1]), o_ref.dtype)

    _for_covering_rows(valid, bm, compute)

    @pl.when(valid == 0)
    def _():
        o_ref[...] = jnp.zeros_like(o_ref)


def _down_kernel(be_ref, nu_ref, bv_ref, nx_ref, h_ref, b_ref, w_hbm, o_ref, wbuf, w_s, sem):
    i = pl.program_id(1)
    bm, tn = o_ref.shape
    valid = bv_ref[i]

    def tile_copies(e, j):
        col = pl.multiple_of(j * tn, tn)
        return [pltpu.make_async_copy(w_hbm.at[e, :, pl.ds(col, tn)], wbuf, sem)]

    def consume():
        for r in range(0, w_s.shape[0], WEIGHT_CAST_ROWS):
            rows = slice(r, r + WEIGHT_CAST_ROWS)
            w_s[rows, :] = wbuf[rows, :].astype(BF16)

    _stream_expert_weights(be_ref, nx_ref, valid, tile_copies, consume)

    def compute(rows):
        o_ref[0:rows, :] = _dot(h_ref[0:rows, :], w_s[...]) + b_ref[0]
        if rows < bm:
            o_ref[rows:bm, :] = jnp.zeros((bm - rows, o_ref.shape[1]), o_ref.dtype)

    _for_covering_rows(valid, bm, compute)

    @pl.when(valid == 0)
    def _():
        o_ref[...] = jnp.zeros_like(o_ref)


def _experts(blk_expert, n_used, blk_valid, next_expert, xs, w_gate_up, b_gate_up, w_down, b_down):
    P = xs.shape[0]
    bm, tn, tn_down = MOE_BM, MOE_TN, MOE_DOWN_TN
    n_blk = P // bm
    d_exp = w_down.shape[1]
    up_off = d_exp // tn

    def blk(i, nu):
        return jnp.minimum(i, nu[0] - 1)

    gateup_spec = pltpu.PrefetchScalarGridSpec(
        num_scalar_prefetch=4,
        grid=(d_exp // tn, n_blk),
        in_specs=[
            pl.BlockSpec((bm, D_MODEL), lambda j, i, be, nu, bv, nx: (blk(i, nu), 0)),
            pl.BlockSpec((1, 1, tn), lambda j, i, be, nu, bv, nx: (be[blk(i, nu)], 0, j)),
            pl.BlockSpec((1, 1, tn), lambda j, i, be, nu, bv, nx: (be[blk(i, nu)], 0, up_off + j)),
            pl.BlockSpec(memory_space=pl.ANY),
        ],
        out_specs=pl.BlockSpec((bm, tn), lambda j, i, be, nu, bv, nx: (i, j)),
        scratch_shapes=[pltpu.VMEM((2, D_MODEL, tn), F32), pltpu.VMEM((D_MODEL, tn), BF16),
                        pltpu.VMEM((D_MODEL, tn), BF16), pltpu.SemaphoreType.DMA((2,))],
    )
    hmid = pl.pallas_call(
        _gateup_kernel,
        grid_spec=gateup_spec,
        out_shape=jax.ShapeDtypeStruct((P, d_exp), BF16),
        compiler_params=_cparams(("arbitrary", "arbitrary"), 56),
        name="expert_gate_up",
    )(blk_expert, n_used, blk_valid, next_expert, xs, b_gate_up, b_gate_up, w_gate_up)

    down_spec = pltpu.PrefetchScalarGridSpec(
        num_scalar_prefetch=4,
        grid=(D_MODEL // tn_down, n_blk),
        in_specs=[
            pl.BlockSpec((bm, d_exp), lambda j, i, be, nu, bv, nx: (blk(i, nu), 0)),
            pl.BlockSpec((1, 1, tn_down), lambda j, i, be, nu, bv, nx: (be[blk(i, nu)], 0, j)),
            pl.BlockSpec(memory_space=pl.ANY),
        ],
        out_specs=pl.BlockSpec((bm, tn_down), lambda j, i, be, nu, bv, nx: (i, j)),
        scratch_shapes=[pltpu.VMEM((d_exp, tn_down), F32), pltpu.VMEM((d_exp, tn_down), BF16),
                        pltpu.SemaphoreType.DMA(())],
    )
    return pl.pallas_call(
        _down_kernel,
        grid_spec=down_spec,
        out_shape=jax.ShapeDtypeStruct((P, D_MODEL), F32),
        compiler_params=_cparams(("arbitrary", "arbitrary"), 58),
        name="expert_down",
    )(blk_expert, n_used, blk_valid, next_expert, hmid, b_down, w_down)


def _combine_kernel(start_ref, e_ref, r_ref, h_ref, g_ref, ys_ref, o_ref, buf, sem):
    tb = COMBINE_TB

    def row_copy(t, k):
        a = t * TOP_K + k
        src_row = start_ref[e_ref[a]] + r_ref[a]
        return pltpu.make_async_copy(ys_ref.at[pl.ds(src_row, 1), :], buf.at[k, pl.ds(t, 1), :], sem)

    def issue(t, carry):
        for k in range(TOP_K):
            row_copy(t, k).start()
        return carry

    def drain(t, carry):
        for k in range(TOP_K):
            row_copy(t, k).wait()
        return carry

    lax.fori_loop(0, tb, issue, 0)
    lax.fori_loop(0, tb, drain, 0)
    g = g_ref[...]
    acc = h_ref[...]
    for k in range(TOP_K):
        acc = acc + g[:, k:k + 1] * buf[k]
    o_ref[...] = acc


def _combine(pad_start, e_flat, r_flat, h1, gates, ys):
    T = h1.shape[0]
    tb = COMBINE_TB
    grid_spec = pltpu.PrefetchScalarGridSpec(
        num_scalar_prefetch=1,
        grid=(T // tb,),
        in_specs=[
            pl.BlockSpec((tb * TOP_K,), lambda i, s: (i,), memory_space=pltpu.SMEM),
            pl.BlockSpec((tb * TOP_K,), lambda i, s: (i,), memory_space=pltpu.SMEM),
            pl.BlockSpec((tb, D_MODEL), lambda i, s: (i, 0)),
            pl.BlockSpec((tb, LANES), lambda i, s: (i, 0)),
            pl.BlockSpec(memory_space=pl.ANY),
        ],
        out_specs=pl.BlockSpec((tb, D_MODEL), lambda i, s: (i, 0)),
        scratch_shapes=[pltpu.VMEM((TOP_K, tb, D_MODEL), F32), pltpu.SemaphoreType.DMA(())],
    )
    return pl.pallas_call(
        _combine_kernel,
        grid_spec=grid_spec,
        out_shape=jax.ShapeDtypeStruct((T, D_MODEL), F32),
        compiler_params=_cparams(("arbitrary",), 32),
        name="combine",
    )(pad_start, e_flat, r_flat, h1, gates, ys)


def _pad_lanes(a, width=LANES):
    return jnp.pad(a, ((0, 0), (0, width - a.shape[1])))


def _layer(h, mem, positions, norm_mix, w_in, b_forget, fox_q_norm, fox_k_norm, swa_q_norm, swa_k_norm,
           swa_sinks, xmem_q_norm, xmem_k_norm, norm_mem, w_mem_kv, w_up_fox, w_up_swa, w_up_xmem, w_out,
           norm_ffn, w_router, b_router, w_gate_up, b_gate_up, w_down, b_down):
    B, S, D = h.shape
    M = mem.shape[1]
    T = B * S
    x2 = h.reshape(T, D)

    attn_end = 3 * FOX_W + FOX_HEADS
    gate_start = attn_end + SWA_QW + 2 * SWA_KVW + XMEM_W
    head_order = np.asarray(SWA_HEAD_ORDER)
    w_t = jnp.transpose(w_in)
    w_sq_t = w_t[attn_end:attn_end + SWA_QW].reshape(SWA_HEADS, HEAD_DIM, D)[head_order].reshape(SWA_QW, D)
    w_attn = _transpose_cast(jnp.concatenate([w_t[:3 * FOX_W], w_sq_t, w_t[attn_end + SWA_QW:gate_start]], axis=0))
    w_gates = _transpose_cast(w_t[gate_start:])
    w_up_swa = w_up_swa.reshape(SWA_HEADS, HEAD_DIM, D)[head_order].reshape(SWA_QW, D)
    swa_sinks = swa_sinks[head_order]
    wf = _pad_lanes(jnp.transpose(w_t[3 * FOX_W:attn_end]))
    wf_hi, wf_lo = _split2(wf)
    bf = _pad_lanes(b_forget.reshape(1, FOX_HEADS))

    def tile_gain(g, reps):
        return jnp.tile(g.reshape(1, -1), (1, reps))

    xn, c = _prenorm(x2, norm_mix.reshape(1, D), wf_hi, wf_lo, bf, S)
    proj = _matmul(xn, w_attn, F32, False, "in_proj_attn")
    gates = _matmul(xn, w_gates, BF16, True, "in_proj_gates")

    qa, ka, vb = _fox_prep(proj, c, tile_gain(fox_q_norm, FOX_HEADS), tile_gain(fox_k_norm, FOX_HEADS))
    o_fox = _fox_attn(qa, ka, vb, B, S)

    o_swa = _swa_attn(proj, positions.reshape(T, 1), swa_sinks.astype(F32),
                      tile_gain(swa_q_norm, SWA_HEADS), tile_gain(swa_k_norm, SWA_KV_HEADS), B, S)

    mk, mv = _mem_kv(mem.reshape(B * M, D), norm_mem.reshape(1, D), w_mem_kv.astype(BF16),
                     xmem_k_norm.reshape(1, XMEM_HEAD_DIM))
    o_x = _mem_attn(proj, mk, mv, xmem_q_norm.reshape(1, XMEM_HEAD_DIM), S, M)

    wr = _pad_lanes(w_router)
    wr_hi, wr_lo = _split2(wr)
    h1, hn, route_i, route_g, counts = _merge(
        o_fox, o_swa, o_x, gates, x2, w_up_fox.astype(BF16), w_up_swa.astype(BF16), w_up_xmem.astype(BF16),
        w_out.astype(BF16), norm_ffn.reshape(1, D), wr_hi, wr_lo, _pad_lanes(b_router.reshape(1, N_EXPERTS)))

    bm = MOE_BM
    cnt = counts[0, :N_EXPERTS].astype(jnp.int32)
    padded = ((cnt + bm - 1) // bm) * bm
    pad_end = jnp.cumsum(padded)
    pad_start = (pad_end - padded).astype(jnp.int32)
    P = T * TOP_K + N_EXPERTS * bm
    n_blk = P // bm
    blk_first = jnp.arange(n_blk, dtype=jnp.int32) * bm
    blk_expert = jnp.minimum(jnp.sum((pad_end[None, :] <= blk_first[:, None]).astype(jnp.int32), axis=1),
                             N_EXPERTS - 1)
    n_used = (pad_end[-1:] // bm).astype(jnp.int32)
    is_blk_expert = blk_expert[:, None] == jnp.arange(N_EXPERTS, dtype=jnp.int32)[None, :]
    tokens_end = jnp.sum(jnp.where(is_blk_expert, (pad_start + cnt)[None, :], 0), axis=1)
    blk_valid = jnp.where(blk_first < pad_end[-1], jnp.clip(tokens_end - blk_first, 0, bm), 0).astype(jnp.int32)
    ids = jnp.arange(N_EXPERTS, dtype=jnp.int32)
    later = jnp.min(jnp.where((cnt > 0)[None, :] & (ids[None, :] > ids[:, None]), ids[None, :], N_EXPERTS), axis=1)
    first = jnp.min(jnp.where(cnt > 0, ids, N_EXPERTS))
    next_expert = jnp.where(later < N_EXPERTS, later, first).astype(jnp.int32)
    e_flat = route_i[:, :TOP_K].reshape(T * TOP_K)
    r_flat = route_i[:, TOP_K:2 * TOP_K].reshape(T * TOP_K)

    xs = _dispatch(pad_start, cnt, e_flat, r_flat, hn, P)
    ys = _experts(blk_expert, n_used, blk_valid, next_expert, xs, w_gate_up, b_gate_up.reshape(N_EXPERTS, 1, -1),
                  w_down, b_down.reshape(N_EXPERTS, 1, -1))
    out = _combine(pad_start, e_flat, r_flat, h1, route_g, ys)
    return out.reshape(B, S, D)


def kernel(x, mem, positions, norm_mix, w_in, b_forget, fox_q_norm, fox_k_norm, swa_q_norm, swa_k_norm, swa_sinks, xmem_q_norm, xmem_k_norm, norm_mem, w_mem_kv, w_up_fox, w_up_swa, w_up_xmem, w_out, norm_ffn, w_router, b_router, w_gate_up, b_gate_up, w_down, b_down):
    h = x
    for layer in range(norm_mix.shape[0]):
        h = _layer(
            h, mem, positions, norm_mix[layer], w_in[layer], b_forget[layer],
            fox_q_norm[layer], fox_k_norm[layer], swa_q_norm[layer], swa_k_norm[layer],
            swa_sinks[layer], xmem_q_norm[layer], xmem_k_norm[layer], norm_mem[layer],
            w_mem_kv[layer], w_up_fox[layer], w_up_swa[layer], w_up_xmem[layer],
            w_out[layer], norm_ffn[layer], w_router[layer], b_router[layer],
            w_gate_up[layer], b_gate_up[layer], w_down[layer], b_down[layer])
    return h
```

```python
import functools

import numpy as np
import jax
import jax.numpy as jnp
from jax import lax
from jax.experimental import pallas as pl
from jax.experimental.pallas import tpu as pltpu

F32 = jnp.float32
BF16 = jnp.bfloat16

D_MODEL = 2048
HEAD_DIM = 64
FOX_HEADS = 12
SWA_HEADS = 12
SWA_KV_HEADS = 4
SWA_GROUP = SWA_HEADS // SWA_KV_HEADS
WINDOW = 128
XMEM_HEADS = 4
XMEM_HEAD_DIM = 128
N_EXPERTS = 32
TOP_K = 4
SWIGLU_LIMIT = 7.0
SWIGLU_ALPHA = 1.702
ROPE_THETA = 10000.0
NORM_EPS = 1e-6

FOX_W = FOX_HEADS * HEAD_DIM
SWA_QW = SWA_HEADS * HEAD_DIM
SWA_KVW = SWA_KV_HEADS * HEAD_DIM
XMEM_W = XMEM_HEADS * XMEM_HEAD_DIM
FOX_PAIRS = FOX_HEADS // 2

LANES = 128
MXU_TILE = 256
NEG_BIG = -1e30
LOG2E = 1.4426950408889634
MIB = 1024 * 1024

PRE_TM = 512
MM_TM = 1024
MM_TN = 1024
PREP_TM = 512
FOX_TQ = 1024
FOX_TK = 512
MEM_TQ = 512
MERGE_TM = 256
MOE_BM = 512
MOE_SUB = 128
MOE_TN = 1024
MOE_DOWN_TN = 2048
WEIGHT_CAST_ROWS = 256
DISPATCH_TB = 256
COMBINE_TB = 128


def _cparams(semantics, vmem_mib):
    return pltpu.CompilerParams(dimension_semantics=semantics, vmem_limit_bytes=vmem_mib * MIB)


def _dot(a, b):
    return jnp.dot(a, b, preferred_element_type=F32)


def _dot_nt(a, b):
    return lax.dot_general(a, b, (((1,), (1,)), ((), ())), preferred_element_type=F32)


def _split2(x):
    hi = x.astype(BF16)
    lo = (x - hi.astype(F32)).astype(BF16)
    return hi, lo


def _split3(x):
    hi = x.astype(BF16)
    r = x - hi.astype(F32)
    mid = r.astype(BF16)
    lo = (r - mid.astype(F32)).astype(BF16)
    return hi, mid, lo


def _prenorm_kernel(x_ref, g_ref, wf_hi_ref, wf_lo_ref, bf_ref, xn_ref, c_ref, carry_ref, *, blocks_per_seq):
    i = pl.program_id(0)

    @pl.when(i % blocks_per_seq == 0)
    def _():
        carry_ref[...] = jnp.zeros_like(carry_ref)

    x = x_ref[...]
    tm = x.shape[0]
    ms = jnp.mean(x * x, axis=-1, keepdims=True)
    xn = x * lax.rsqrt(ms + NORM_EPS) * g_ref[...]
    xn_hi, xn_lo = _split2(xn)
    xn_ref[...] = xn_hi
    wf_hi = wf_hi_ref[...]
    z = _dot(xn_hi, wf_hi) + _dot(xn_lo, wf_hi) + _dot(xn_hi, wf_lo_ref[...]) + bf_ref[...]
    logf = jnp.minimum(z, 0.0) - jnp.log1p(jnp.exp(-jnp.abs(z)))
    lane = lax.broadcasted_iota(jnp.int32, logf.shape, 1)
    logf = jnp.where(lane < FOX_HEADS, logf, 0.0)
    hi, mid, lo = _split3(logf)
    r = lax.broadcasted_iota(jnp.int32, (tm, tm), 0)
    cc = lax.broadcasted_iota(jnp.int32, (tm, tm), 1)
    tri = jnp.where(cc <= r, 1.0, 0.0).astype(BF16)
    c = _dot(tri, hi) + _dot(tri, mid) + _dot(tri, lo) + carry_ref[...]
    c_ref[...] = c
    carry_ref[...] = c[tm - 1:tm, :]


def _prenorm(x2, gain, wf_hi, wf_lo, bf, seq):
    T = x2.shape[0]
    tm = PRE_TM
    return pl.pallas_call(
        functools.partial(_prenorm_kernel, blocks_per_seq=seq // tm),
        grid=(T // tm,),
        in_specs=[
            pl.BlockSpec((tm, D_MODEL), lambda i: (i, 0)),
            pl.BlockSpec((1, D_MODEL), lambda i: (0, 0)),
            pl.BlockSpec((D_MODEL, LANES), lambda i: (0, 0)),
            pl.BlockSpec((D_MODEL, LANES), lambda i: (0, 0)),
            pl.BlockSpec((1, LANES), lambda i: (0, 0)),
        ],
        out_specs=[
            pl.BlockSpec((tm, D_MODEL), lambda i: (i, 0)),
            pl.BlockSpec((tm, LANES), lambda i: (i, 0)),
        ],
        out_shape=[
            jax.ShapeDtypeStruct((T, D_MODEL), BF16),
            jax.ShapeDtypeStruct((T, LANES), F32),
        ],
        scratch_shapes=[pltpu.VMEM((1, LANES), F32)],
        compiler_params=_cparams(("arbitrary",), 40),
        name="prenorm",
    )(x2, gain, wf_hi, wf_lo, bf)


def _mm_kernel(x_ref, w_ref, o_ref, *, sigmoid):
    acc = _dot(x_ref[...], w_ref[...])
    if sigmoid:
        acc = jax.nn.sigmoid(acc)
    o_ref[...] = acc.astype(o_ref.dtype)


def _matmul(x, w, out_dtype, sigmoid, name):
    M, K = x.shape
    N = w.shape[1]
    tm, tn = MM_TM, MM_TN
    return pl.pallas_call(
        functools.partial(_mm_kernel, sigmoid=sigmoid),
        grid=(M // tm, N // tn),
        in_specs=[
            pl.BlockSpec((tm, K), lambda i, j: (i, 0)),
            pl.BlockSpec((K, tn), lambda i, j: (0, j)),
        ],
        out_specs=pl.BlockSpec((tm, tn), lambda i, j: (i, j)),
        out_shape=jax.ShapeDtypeStruct((M, N), out_dtype),
        compiler_params=_cparams(("parallel", "parallel"), 48),
        name=name,
    )(x, w)


def _transpose_cast_kernel(w_ref, o_ref):
    o_ref[...] = w_ref[...].T.astype(o_ref.dtype)


def _transpose_cast(w_t):
    N, K = w_t.shape
    tn = 512
    return pl.pallas_call(
        _transpose_cast_kernel,
        grid=(N // tn,),
        in_specs=[pl.BlockSpec((tn, K), lambda i: (i, 0))],
        out_specs=pl.BlockSpec((K, tn), lambda i: (0, i)),
        out_shape=jax.ShapeDtypeStruct((K, N), BF16),
        compiler_params=_cparams(("parallel",), 32),
        name="weight_transpose_cast",
    )(w_t)


def _foxprep_kernel(q_ref, k_ref, v_ref, c_ref, gq_ref, gk_ref, bd_ref, selq_ref, selk_ref,
                    onesq_ref, onesk_ref, qa_ref, ka_ref, vt_ref):
    bd = bd_ref[...]

    def headnorm(x, g):
        hi, lo = _split2(x * x)
        w = bd.shape[0]
        ssq = jnp.concatenate([_dot(hi[:, c:c + w], bd) + _dot(lo[:, c:c + w], bd) for c in range(0, x.shape[1], w)],
                              axis=1)
        return x * lax.rsqrt(ssq * (1.0 / HEAD_DIM) + NORM_EPS) * g

    qn = headnorm(q_ref[...], gq_ref[...]) * (HEAD_DIM ** -0.5 * LOG2E)
    kn = headnorm(k_ref[...], gk_ref[...])
    v = v_ref[...]
    hi, mid, lo = _split3(c_ref[...] * LOG2E)
    c3 = jnp.concatenate([hi, mid, lo], axis=1)
    lane = lax.broadcasted_iota(jnp.int32, (qn.shape[0], LANES), 1)
    ones_col = jnp.where(lane == HEAD_DIM, 1.0, 0.0)
    for p in range(FOX_PAIRS):
        qp = qn[:, p * LANES:(p + 1) * LANES]
        vp = v[:, p * LANES:(p + 1) * LANES]
        for j in range(2):
            h = 2 * p + j
            keep = (lane < HEAD_DIM) if j == 0 else (lane >= HEAD_DIM)
            qa_ref[h, :, 0:LANES] = jnp.where(keep, qp, 0.0).astype(BF16)
            qa_ref[h, :, LANES:2 * LANES] = (_dot(c3, selq_ref[h]) + onesq_ref[h]).astype(BF16)
            vh = vp if j == 0 else pltpu.roll(vp, HEAD_DIM, axis=1)
            vt_ref[h, 0] = jnp.where(lane < HEAD_DIM, vh, ones_col).T.astype(BF16)
        ka_ref[p, :, 0:LANES] = kn[:, p * LANES:(p + 1) * LANES].astype(BF16)
        ka_ref[p, :, LANES:2 * LANES] = (_dot(c3, selk_ref[p]) + onesk_ref[...]).astype(BF16)


def _fox_tables():
    selq = np.zeros((FOX_HEADS, 3 * LANES, LANES), np.float32)
    selk = np.zeros((FOX_PAIRS, 3 * LANES, LANES), np.float32)
    onesq = np.zeros((FOX_HEADS, 1, LANES), np.float32)
    onesk = np.zeros((1, LANES), np.float32)
    for h in range(FOX_HEADS):
        p, j = divmod(h, 2)
        for piece in range(3):
            selq[h, piece * LANES + h, 6 * j + piece] = 1.0
            selk[p, piece * LANES + h, 6 * j + 3 + piece] = -1.0
            onesq[h, 0, 6 * j + 3 + piece] = 1.0
            onesk[0, 6 * j + piece] = 1.0
    bd = np.kron(np.eye(MXU_TILE // HEAD_DIM, dtype=np.float32), np.ones((HEAD_DIM, HEAD_DIM), np.float32))
    return (jnp.asarray(bd, BF16), jnp.asarray(selq, BF16), jnp.asarray(selk, BF16),
            jnp.asarray(onesq, F32), jnp.asarray(onesk, F32))


def _fox_prep(proj, c, gq, gk):
    T = proj.shape[0]
    tm = PREP_TM
    bd, selq, selk, onesq, onesk = _fox_tables()
    const2 = lambda i: (0, 0)
    const3 = lambda i: (0, 0, 0)
    return pl.pallas_call(
        _foxprep_kernel,
        grid=(T // tm,),
        in_specs=[
            pl.BlockSpec((tm, FOX_W), lambda i: (i, 0)),
            pl.BlockSpec((tm, FOX_W), lambda i: (i, 1)),
            pl.BlockSpec((tm, FOX_W), lambda i: (i, 2)),
            pl.BlockSpec((tm, LANES), lambda i: (i, 0)),
            pl.BlockSpec((1, FOX_W), const2),
            pl.BlockSpec((1, FOX_W), const2),
            pl.BlockSpec((MXU_TILE, MXU_TILE), const2),
            pl.BlockSpec((FOX_HEADS, 3 * LANES, LANES), const3),
            pl.BlockSpec((FOX_PAIRS, 3 * LANES, LANES), const3),
            pl.BlockSpec((FOX_HEADS, 1, LANES), const3),
            pl.BlockSpec((1, LANES), const2),
        ],
        out_specs=[
            pl.BlockSpec((FOX_HEADS, tm, 2 * LANES), lambda i: (0, i, 0)),
            pl.BlockSpec((FOX_PAIRS, tm, 2 * LANES), lambda i: (0, i, 0)),
            pl.BlockSpec((FOX_HEADS, 1, LANES, tm), lambda i: (0, i, 0, 0)),
        ],
        out_shape=[
            jax.ShapeDtypeStruct((FOX_HEADS, T, 2 * LANES), BF16),
            jax.ShapeDtypeStruct((FOX_PAIRS, T, 2 * LANES), BF16),
            jax.ShapeDtypeStruct((FOX_HEADS, T // tm, LANES, tm), BF16),
        ],
        compiler_params=_cparams(("parallel",), 48),
        name="fox_prep",
    )(proj, proj, proj, c, gq, gk, bd, selq, selk, onesq, onesk)


def _fox_attn_kernel(q_ref, k_ref, vt_ref, o_ref, m_ref, acc_ref, sa_ref, sb_ref):
    qi = pl.program_id(2)
    tk = FOX_TK
    m_ref[...] = jnp.full(m_ref.shape, NEG_BIG, F32)
    acc_ref[...] = jnp.zeros(acc_ref.shape, F32)

    def scores(kb, s_ref, q_lo=0):
        off = pl.multiple_of(kb * tk, tk)
        k = k_ref[pl.ds(off, tk), :]
        for j in range(2):
            s_ref[j, :, q_lo:] = _dot_nt(k, q_ref[j, q_lo:, :])

    def softmax_pv(kb, s_ref, diag_offset, q_lo=0):
        for j in range(2):
            s = s_ref[j, :, q_lo:]
            if diag_offset is not None:
                key = lax.broadcasted_iota(jnp.int32, s.shape, 0) + diag_offset
                qry = lax.broadcasted_iota(jnp.int32, s.shape, 1) + q_lo
                s = jnp.where(key <= qry, s, NEG_BIG)
            m_old = m_ref[j, :, q_lo:]
            m_new = jnp.maximum(m_old, jnp.max(s, axis=0, keepdims=True))
            alpha = jnp.exp2(m_old - m_new)
            p = jnp.exp2(s - m_new)
            acc_ref[j, :, q_lo:] = alpha * acc_ref[j, :, q_lo:] + _dot(vt_ref[j, kb], p.astype(BF16))
            m_ref[j, :, q_lo:] = m_new

    scores(0, sa_ref)

    def two_blocks(i, carry):
        scores(2 * i + 1, sb_ref)
        softmax_pv(2 * i, sa_ref, None)
        scores(2 * i + 2, sa_ref)
        softmax_pv(2 * i + 1, sb_ref, None)
        return carry

    lax.fori_loop(0, qi, two_blocks, 0)
    scores(2 * qi + 1, sb_ref, q_lo=tk)
    softmax_pv(2 * qi, sa_ref, 0)
    softmax_pv(2 * qi + 1, sb_ref, tk, q_lo=tk)

    outs = [acc_ref[j, 0:HEAD_DIM, :] / acc_ref[j, HEAD_DIM:HEAD_DIM + 1, :] for j in range(2)]
    o_ref[...] = jnp.concatenate(outs, axis=0).T.astype(o_ref.dtype)


def _fox_attn(qa, ka, vt, batch, seq):
    T = qa.shape[1]
    tq, tk = FOX_TQ, FOX_TK
    assert tq == 2 * tk and tk == PREP_TM
    nq = seq // tq
    nk = seq // tk
    qa4 = qa.reshape(FOX_PAIRS, 2, T, 2 * LANES)
    vt5 = vt.reshape(FOX_PAIRS, 2, batch * nk, LANES, tk)
    return pl.pallas_call(
        _fox_attn_kernel,
        grid=(batch, FOX_PAIRS, nq),
        in_specs=[
            pl.BlockSpec((None, 2, tq, 2 * LANES), lambda b, p, q: (p, 0, b * nq + q, 0)),
            pl.BlockSpec((None, seq, 2 * LANES), lambda b, p, q: (p, b, 0)),
            pl.BlockSpec((None, 2, nk, LANES, tk), lambda b, p, q: (p, 0, b, 0, 0)),
        ],
        out_specs=pl.BlockSpec((tq, LANES), lambda b, p, q: (b * nq + q, p)),
        out_shape=jax.ShapeDtypeStruct((T, FOX_W), BF16),
        scratch_shapes=[
            pltpu.VMEM((2, 1, tq), F32),
            pltpu.VMEM((2, LANES, tq), F32),
            pltpu.VMEM((2, tk, tq), F32),
            pltpu.VMEM((2, tk, tq), F32),
        ],
        compiler_params=_cparams(("parallel", "parallel", "arbitrary"), 48),
        name="fox_attn",
    )(qa4, ka, vt5)


SWA_HEAD_ORDER = (0, 3, 1, 4, 2, 5, 6, 9, 7, 10, 8, 11)
SWA_KV_TILES = SWA_KV_HEADS // 2
SWA_Q_TILES_PER_KV_TILE = SWA_HEADS // 2 // SWA_KV_TILES


def _swa_kernel(sinks_ref, q_ref, k_ref, v_ref, pos_ref, freq_ref, sign_ref, gq_ref, gk_ref, bdq_ref, bdk_ref,
                o_ref, kprev_ref, vtprev_ref):
    n = pl.program_id(1)
    W = WINDOW
    half = HEAD_DIM // 2

    @pl.when(n == 0)
    def _():
        kprev_ref[...] = jnp.zeros_like(kprev_ref)
        vtprev_ref[...] = jnp.zeros_like(vtprev_ref)

    lane = lax.broadcasted_iota(jnp.int32, (W, LANES), 1)
    ang = pos_ref[...].astype(F32) * freq_ref[...]
    cos1 = jnp.cos(ang)
    sin1 = jnp.sin(ang) * sign_ref[...]
    first_half1 = (lane & half) == 0

    def norm_rope(x, g, bd):
        reps = x.shape[1] // LANES
        hi, lo = _split2(x * x)
        w = bd.shape[0]
        ssq = jnp.concatenate([_dot(hi[:, c:c + w], bd) + _dot(lo[:, c:c + w], bd) for c in range(0, x.shape[1], w)],
                              axis=1)
        xn = x * lax.rsqrt(ssq * (1.0 / HEAD_DIM) + NORM_EPS) * g
        first_half = jnp.tile(first_half1, (1, reps))
        swapped = jnp.where(first_half, pltpu.roll(xn, x.shape[1] - half, axis=1), pltpu.roll(xn, half, axis=1))
        return xn * jnp.tile(cos1, (1, reps)) + swapped * jnp.tile(sin1, (1, reps))

    q = norm_rope(q_ref[...], gq_ref[...], bdq_ref[...]) * (HEAD_DIM ** -0.5 * LOG2E)
    k = norm_rope(k_ref[...], gk_ref[...], bdk_ref[...]).astype(BF16)
    v = v_ref[...]

    key = lax.broadcasted_iota(jnp.int32, (2 * W, 2 * W), 0)
    qry = lax.broadcasted_iota(jnp.int32, (2 * W, 2 * W), 1) & (W - 1)
    visible = ((key < W) & (key > qry) & (n > 0)) | ((key >= W) & (key - W <= qry))
    low_q = lax.broadcasted_iota(jnp.int32, (1, 2 * W), 1) < W

    for kt in range(SWA_KV_TILES):
        k_cur = k[:, kt * LANES:(kt + 1) * LANES]
        vt_cur = v[:, kt * LANES:(kt + 1) * LANES].T.astype(BF16)
        keys = jnp.concatenate([kprev_ref[kt], k_cur], axis=0)
        vt = jnp.concatenate([vtprev_ref[kt], vt_cur], axis=1)
        for r in range(SWA_Q_TILES_PER_KV_TILE):
            t = kt * SWA_Q_TILES_PER_KV_TILE + r
            qt = q[:, t * LANES:(t + 1) * LANES]
            qs = jnp.concatenate([jnp.where(lane < HEAD_DIM, qt, 0.0), jnp.where(lane >= HEAD_DIM, qt, 0.0)],
                                 axis=0).astype(BF16)
            s = jnp.where(visible, _dot_nt(keys, qs), NEG_BIG)
            sink = jnp.where(low_q, sinks_ref[2 * t], sinks_ref[2 * t + 1]) * LOG2E
            m = jnp.maximum(jnp.max(s, axis=0, keepdims=True), sink)
            p = jnp.exp2(s - m)
            denom = jnp.sum(p, axis=0, keepdims=True) + jnp.exp2(sink - m)
            ot = _dot(vt, p.astype(BF16)) / denom
            pair = jnp.concatenate([ot[0:HEAD_DIM, 0:W], ot[HEAD_DIM:LANES, W:2 * W]], axis=0)
            o_ref[:, t * LANES:(t + 1) * LANES] = pair.T.astype(o_ref.dtype)
        kprev_ref[kt] = k_cur
        vtprev_ref[kt] = vt_cur


def _swa_attn(proj, pos2, sinks, gq, gk, batch, seq):
    T = proj.shape[0]
    W = WINDOW
    assert W == LANES
    nb = seq // W
    half = HEAD_DIM // 2
    inv_freq = np.power(ROPE_THETA, -np.arange(0, HEAD_DIM, 2, dtype=np.float32) / HEAD_DIM).astype(np.float32)
    freq = np.tile(inv_freq, LANES // half).reshape(1, LANES)
    sign = np.tile(np.concatenate([-np.ones(half, np.float32), np.ones(half, np.float32)]),
                   LANES // HEAD_DIM).reshape(1, LANES)
    head_ones = np.ones((HEAD_DIM, HEAD_DIM), np.float32)
    bdq = bdk = jnp.asarray(np.kron(np.eye(MXU_TILE // HEAD_DIM, dtype=np.float32), head_ones), BF16)
    q_col = (3 * FOX_W) // SWA_QW
    k_col = (3 * FOX_W + SWA_QW) // SWA_KVW
    const2 = lambda b, n, s: (0, 0)
    grid_spec = pltpu.PrefetchScalarGridSpec(
        num_scalar_prefetch=1,
        grid=(batch, nb),
        in_specs=[
            pl.BlockSpec((W, SWA_QW), lambda b, n, s: (b * nb + n, q_col)),
            pl.BlockSpec((W, SWA_KVW), lambda b, n, s: (b * nb + n, k_col)),
            pl.BlockSpec((W, SWA_KVW), lambda b, n, s: (b * nb + n, k_col + 1)),
            pl.BlockSpec((W, 1), lambda b, n, s: (b * nb + n, 0)),
            pl.BlockSpec((1, LANES), const2),
            pl.BlockSpec((1, LANES), const2),
            pl.BlockSpec((1, SWA_QW), const2),
            pl.BlockSpec((1, SWA_KVW), const2),
            pl.BlockSpec((MXU_TILE, MXU_TILE), const2),
            pl.BlockSpec((MXU_TILE, MXU_TILE), const2),
        ],
        out_specs=pl.BlockSpec((W, SWA_QW), lambda b, n, s: (b * nb + n, 0)),
        scratch_shapes=[
            pltpu.VMEM((SWA_KV_TILES, W, LANES), BF16),
            pltpu.VMEM((SWA_KV_TILES, LANES, W), BF16),
        ],
    )
    return pl.pallas_call(
        _swa_kernel,
        grid_spec=grid_spec,
        out_shape=jax.ShapeDtypeStruct((T, SWA_QW), BF16),
        compiler_params=_cparams(("parallel", "arbitrary"), 32),
        name="swa_attn",
    )(sinks, proj, proj, proj, pos2, jnp.asarray(freq), jnp.asarray(sign), gq, gk, bdq, bdk)


def _memkv_kernel(mem_ref, g_ref, w_ref, gk_ref, k_ref, v_ref):
    x = mem_ref[...]
    ms = jnp.mean(x * x, axis=-1, keepdims=True)
    mn = (x * lax.rsqrt(ms + NORM_EPS) * g_ref[...]).astype(BF16)
    kv = _dot(mn, w_ref[...])
    for h in range(XMEM_HEADS):
        kh = kv[:, h * LANES:(h + 1) * LANES]
        ms_h = jnp.mean(kh * kh, axis=-1, keepdims=True)
        k_ref[:, h * LANES:(h + 1) * LANES] = (kh * lax.rsqrt(ms_h + NORM_EPS) * gk_ref[...]).astype(BF16)
    v_ref[...] = kv[:, XMEM_W:].astype(BF16)


def _mem_kv(mem2, gain, w, gk):
    R = mem2.shape[0]
    tm = 256
    return pl.pallas_call(
        _memkv_kernel,
        grid=(R // tm,),
        in_specs=[
            pl.BlockSpec((tm, D_MODEL), lambda i: (i, 0)),
            pl.BlockSpec((1, D_MODEL), lambda i: (0, 0)),
            pl.BlockSpec((D_MODEL, 2 * XMEM_W), lambda i: (0, 0)),
            pl.BlockSpec((1, XMEM_HEAD_DIM), lambda i: (0, 0)),
        ],
        out_specs=[
            pl.BlockSpec((tm, XMEM_W), lambda i: (i, 0)),
            pl.BlockSpec((tm, XMEM_W), lambda i: (i, 0)),
        ],
        out_shape=[
            jax.ShapeDtypeStruct((R, XMEM_W), BF16),
            jax.ShapeDtypeStruct((R, XMEM_W), BF16),
        ],
        compiler_params=_cparams(("parallel",), 32),
        name="mem_kv",
    )(mem2, gain, w, gk)


def _memattn_kernel(q_ref, k_ref, v_ref, gq_ref, o_ref):
    q = q_ref[...]
    for h in range(XMEM_HEADS):
        sl = slice(h * LANES, (h + 1) * LANES)
        qh = q[:, sl]
        ms = jnp.mean(qh * qh, axis=-1, keepdims=True)
        qn = (qh * lax.rsqrt(ms + NORM_EPS) * gq_ref[...] * (XMEM_HEAD_DIM ** -0.5)).astype(BF16)
        s = _dot_nt(qn, k_ref[:, sl])
        m = jnp.max(s, axis=-1, keepdims=True)
        p = jnp.exp(s - m)
        l = jnp.sum(p, axis=-1, keepdims=True)
        o_ref[:, sl] = (_dot(p.astype(BF16), v_ref[:, sl]) / l).astype(o_ref.dtype)


def _mem_attn(proj, mk, mv, gq, seq, n_mem):
    T = proj.shape[0]
    tq = MEM_TQ
    per_seq = seq // tq
    q_col = (3 * FOX_W + SWA_QW + 2 * SWA_KVW) // XMEM_W
    return pl.pallas_call(
        _memattn_kernel,
        grid=(T // tq,),
        in_specs=[
            pl.BlockSpec((tq, XMEM_W), lambda i: (i, q_col)),
            pl.BlockSpec((n_mem, XMEM_W), lambda i: (i // per_seq, 0)),
            pl.BlockSpec((n_mem, XMEM_W), lambda i: (i // per_seq, 0)),
            pl.BlockSpec((1, XMEM_HEAD_DIM), lambda i: (0, 0)),
        ],
        out_specs=pl.BlockSpec((tq, XMEM_W), lambda i: (i, 0)),
        out_shape=jax.ShapeDtypeStruct((T, XMEM_W), BF16),
        compiler_params=_cparams(("parallel",), 32),
        name="mem_attn",
    )(proj, mk, mv, gq)


def _merge_kernel(of_ref, os_ref, ox_ref, g0_ref, g1_ref, g2_ref, x_ref, wf_ref, ws_ref, wx_ref, wo_ref,
                  gn_ref, wr_hi_ref, wr_lo_ref, br_ref,
                  h_ref, hn_ref, ri_ref, rg_ref, cnt_ref, carry_ref):
    i = pl.program_id(0)

    @pl.when(i == 0)
    def _():
        carry_ref[...] = jnp.zeros_like(carry_ref)

    merged = (g0_ref[...].astype(F32) * _dot(of_ref[...], wf_ref[...])
              + g1_ref[...].astype(F32) * _dot(os_ref[...], ws_ref[...])
              + g2_ref[...].astype(F32) * _dot(ox_ref[...], wx_ref[...]))
    h = x_ref[...] + _dot(merged.astype(BF16), wo_ref[...])
    h_ref[...] = h
    ms = jnp.mean(h * h, axis=-1, keepdims=True)
    hn = h * lax.rsqrt(ms + NORM_EPS) * gn_ref[...]
    hn_hi, hn_lo = _split2(hn)
    hn_ref[...] = hn
    wr_hi = wr_hi_ref[...]
    logits = _dot(hn_hi, wr_hi) + _dot(hn_lo, wr_hi) + _dot(hn_hi, wr_lo_ref[...]) + br_ref[...]
    tm = logits.shape[0]
    lane = lax.broadcasted_iota(jnp.int32, (tm, LANES), 1).astype(F32)
    work = jnp.where(lane < N_EXPERTS, logits, NEG_BIG)
    vals, idxs = [], []
    for _ in range(TOP_K):
        mx = jnp.max(work, axis=-1, keepdims=True)
        ix = jnp.min(jnp.where(work == mx, lane, float(LANES)), axis=-1, keepdims=True)
        vals.append(mx)
        idxs.append(ix)
        work = jnp.where(lane == ix, NEG_BIG, work)
    es = [jnp.exp(v - vals[0]) for v in vals]
    den = es[0] + es[1] + es[2] + es[3]
    onehot = jnp.zeros((tm, LANES), F32)
    for ix in idxs:
        onehot = onehot + jnp.where(lane == ix, 1.0, 0.0)
    r = lax.broadcasted_iota(jnp.int32, (tm, tm), 0)
    cc = lax.broadcasted_iota(jnp.int32, (tm, tm), 1)
    tri = jnp.where(cc < r, 1.0, 0.0).astype(BF16)
    before = _dot(tri, onehot.astype(BF16)) + carry_ref[...]
    ri = jnp.zeros((tm, LANES), jnp.int32)
    rg = jnp.zeros((tm, LANES), F32)
    for k in range(TOP_K):
        rank = jnp.sum(jnp.where(lane == idxs[k], before, 0.0), axis=-1, keepdims=True)
        ri = jnp.where(lane == k, idxs[k].astype(jnp.int32), ri)
        ri = jnp.where(lane == TOP_K + k, rank.astype(jnp.int32), ri)
        rg = jnp.where(lane == k, es[k] / den, rg)
    ri_ref[...] = ri
    rg_ref[...] = rg
    total = carry_ref[...] + jnp.sum(onehot, axis=0, keepdims=True)
    carry_ref[...] = total
    cnt_ref[...] = total


def _merge(o_fox, o_swa, o_x, gates, x2, wf, ws, wx, wo, gn, wr_hi, wr_lo, br):
    T = x2.shape[0]
    tm = MERGE_TM
    row = lambda i: (i, 0)
    const = lambda i: (0, 0)
    resident = functools.partial(pl.BlockSpec, index_map=const, pipeline_mode=pl.Buffered(1))
    return pl.pallas_call(
        _merge_kernel,
        grid=(T // tm,),
        in_specs=[
            pl.BlockSpec((tm, FOX_W), row),
            pl.BlockSpec((tm, SWA_QW), row),
            pl.BlockSpec((tm, XMEM_W), row),
            pl.BlockSpec((tm, D_MODEL), lambda i: (i, 0)),
            pl.BlockSpec((tm, D_MODEL), lambda i: (i, 1)),
            pl.BlockSpec((tm, D_MODEL), lambda i: (i, 2)),
            pl.BlockSpec((tm, D_MODEL), row),
            resident((FOX_W, D_MODEL)),
            resident((SWA_QW, D_MODEL)),
            resident((XMEM_W, D_MODEL)),
            resident((D_MODEL, D_MODEL)),
            resident((1, D_MODEL)),
            resident((D_MODEL, LANES)),
            resident((D_MODEL, LANES)),
            resident((1, LANES)),
        ],
        out_specs=[
            pl.BlockSpec((tm, D_MODEL), row),
            pl.BlockSpec((tm, D_MODEL), row),
            pl.BlockSpec((tm, LANES), row),
            pl.BlockSpec((tm, LANES), row),
            pl.BlockSpec((1, LANES), const),
        ],
        out_shape=[
            jax.ShapeDtypeStruct((T, D_MODEL), F32),
            jax.ShapeDtypeStruct((T, D_MODEL), F32),
            jax.ShapeDtypeStruct((T, LANES), jnp.int32),
            jax.ShapeDtypeStruct((T, LANES), F32),
            jax.ShapeDtypeStruct((1, LANES), F32),
        ],
        scratch_shapes=[pltpu.VMEM((1, LANES), F32)],
        compiler_params=_cparams(("arbitrary",), 56),
        name="merge_router",
    )(o_fox, o_swa, o_x, gates, gates, gates, x2, wf, ws, wx, wo, gn, wr_hi, wr_lo, br)


def _dispatch_kernel(start_ref, cnt_ref, e_ref, r_ref, hn_ref, xs_ref, zero_ref, sem):
    i = pl.program_id(0)
    tb = DISPATCH_TB

    def row_copy(src_ref, t, dst_row):
        return pltpu.make_async_copy(src_ref.at[pl.ds(t, 1), :], xs_ref.at[pl.ds(dst_row, 1), :], sem)

    @pl.when(i == 0)
    def _():
        zero_ref[...] = jnp.zeros_like(zero_ref)

        def per_expert(e, carry):
            base = start_ref[e]
            n = cnt_ref[e]
            end = ((n + MOE_BM - 1) // MOE_BM) * MOE_BM

            def fill(r, c):
                row_copy(zero_ref, 0, base + r).start()
                return c

            def drain(r, c):
                row_copy(zero_ref, 0, base + r).wait()
                return c

            lax.fori_loop(n, end, fill, 0)
            lax.fori_loop(n, end, drain, 0)
            return carry

        lax.fori_loop(0, N_EXPERTS, per_expert, 0)

        last = N_EXPERTS - 1
        used = start_ref[last] + ((cnt_ref[last] + MOE_BM - 1) // MOE_BM) * MOE_BM
        zr = zero_ref.shape[0]

        def tail_copy(r):
            return pltpu.make_async_copy(zero_ref, xs_ref.at[pl.ds(pl.multiple_of(used + r * zr, zr), zr), :], sem)

        def tail_fill(r, c):
            tail_copy(r).start()
            return c

        def tail_drain(r, c):
            tail_copy(r).wait()
            return c

        n_tail = (xs_ref.shape[0] - used) // zr
        lax.fori_loop(0, n_tail, tail_fill, 0)
        lax.fori_loop(0, n_tail, tail_drain, 0)

    def token_copy(t, k):
        a = t * TOP_K + k
        return row_copy(hn_ref, t, start_ref[e_ref[a]] + r_ref[a])

    def issue(t, carry):
        for k in range(TOP_K):
            token_copy(t, k).start()
        return carry

    def drain(t, carry):
        for k in range(TOP_K):
            token_copy(t, k).wait()
        return carry

    lax.fori_loop(0, tb, issue, 0)
    lax.fori_loop(0, tb, drain, 0)


def _dispatch(pad_start, cnt, e_flat, r_flat, hn, n_slots):
    T, D = hn.shape
    tb = DISPATCH_TB
    grid_spec = pltpu.PrefetchScalarGridSpec(
        num_scalar_prefetch=2,
        grid=(T // tb,),
        in_specs=[
            pl.BlockSpec((tb * TOP_K,), lambda i, s, c: (i,), memory_space=pltpu.SMEM),
            pl.BlockSpec((tb * TOP_K,), lambda i, s, c: (i,), memory_space=pltpu.SMEM),
            pl.BlockSpec((tb, D), lambda i, s, c: (i, 0)),
        ],
        out_specs=pl.BlockSpec(memory_space=pl.ANY),
        scratch_shapes=[pltpu.VMEM((MOE_BM // 2, D), hn.dtype), pltpu.SemaphoreType.DMA(())],
    )
    return pl.pallas_call(
        _dispatch_kernel,
        grid_spec=grid_spec,
        out_shape=jax.ShapeDtypeStruct((n_slots, D), hn.dtype),
        compiler_params=_cparams(("arbitrary",), 32),
        name="dispatch",
    )(pad_start, cnt, e_flat, r_flat, hn)


def _expert_changed(be_ref, i):
    return (i == 0) | (be_ref[i] != be_ref[jnp.maximum(i - 1, 0)])


def _for_covering_rows(valid, bm, fn):
    for rows in range(MOE_SUB, bm + 1, MOE_SUB):
        @pl.when((valid > rows - MOE_SUB) & (valid <= rows))
        def _(rows=rows):
            fn(rows)


def _stream_expert_weights(be_ref, nx_ref, valid, tile_copies, consume):
    j = pl.program_id(0)
    i = pl.program_id(1)
    n_pass = pl.num_programs(0)

    @pl.when((i == 0) & (j == 0))
    def _():
        for cp in tile_copies(be_ref[0], 0):
            cp.start()

    @pl.when((valid > 0) & _expert_changed(be_ref, i))
    def _():
        e = be_ref[i]
        for cp in tile_copies(e, j):
            cp.wait()
        consume()
        e_next = nx_ref[e]
        j_next = j + (e_next <= e).astype(jnp.int32)

        @pl.when(j_next < n_pass)
        def _():
            for cp in tile_copies(e_next, j_next):
                cp.start()


def _gateup_kernel(be_ref, nu_ref, bv_ref, nx_ref, xs_ref, bg_ref, bu_ref, w_hbm, o_ref, wbuf, wg_s, wu_s, sem):
    i = pl.program_id(1)
    bm, tn = o_ref.shape
    up_off = w_hbm.shape[2] // 2 // tn
    valid = bv_ref[i]

    def tile_copies(e, j):
        def one(half):
            col = pl.multiple_of((half * up_off + j) * tn, tn)
            return pltpu.make_async_copy(w_hbm.at[e, :, pl.ds(col, tn)], wbuf.at[half], sem.at[half])
        return [one(0), one(1)]

    def consume():
        for r in range(0, wg_s.shape[0], WEIGHT_CAST_ROWS):
            rows = slice(r, r + WEIGHT_CAST_ROWS)
            wg_s[rows, :] = wbuf[0, rows, :].astype(BF16)
            wu_s[rows, :] = wbuf[1, rows, :].astype(BF16)

    _stream_expert_weights(be_ref, nx_ref, valid, tile_copies, consume)

    def compute(rows):
        x = xs_ref[0:rows, :].astype(BF16)
        gate = _dot(x, wg_s[...]) + bg_ref[0]
        up = _dot(x, wu_s[...]) + bu_ref[0]
        gate = jnp.minimum(gate, SWIGLU_LIMIT)
        up = jnp.clip(up, -SWIGLU_LIMIT, SWIGLU_LIMIT)
        glu = gate * jax.nn.sigmoid(gate * SWIGLU_ALPHA)
        o_ref[0:rows, :] = ((up + 1.0) * glu).astype(o_ref.dtype)
        if rows < bm:
            o_ref[rows:bm, :] = jnp.zeros((bm - rows, o_ref.shape[1]), o_ref.dtype)

    _for_covering_rows(valid, bm, compute)

    @pl.when(valid == 0)
    def _():
        o_ref[...] = jnp.zeros_like(o_ref)


def _down_kernel(be_ref, nu_ref, bv_ref, nx_ref, h_ref, b_ref, w_hbm, o_ref, wbuf, w_s, sem):
    i = pl.program_id(1)
    bm, tn = o_ref.shape
    valid = bv_ref[i]

    def tile_copies(e, j):
        col = pl.multiple_of(j * tn, tn)
        return [pltpu.make_async_copy(w_hbm.at[e, :, pl.ds(col, tn)], wbuf, sem)]

    def consume():
        for r in range(0, w_s.shape[0], WEIGHT_CAST_ROWS):
            rows = slice(r, r + WEIGHT_CAST_ROWS)
            w_s[rows, :] = wbuf[rows, :].astype(BF16)

    _stream_expert_weights(be_ref, nx_ref, valid, tile_copies, consume)

    def compute(rows):
        o_ref[0:rows, :] = _dot(h_ref[0:rows, :], w_s[...]) + b_ref[0]
        if rows < bm:
            o_ref[rows:bm, :] = jnp.zeros((bm - rows, o_ref.shape[1]), o_ref.dtype)

    _for_covering_rows(valid, bm, compute)

    @pl.when(valid == 0)
    def _():
        o_ref[...] = jnp.zeros_like(o_ref)


def _experts(blk_expert, n_used, blk_valid, next_expert, xs, w_gate_up, b_gate_up, w_down, b_down):
    P = xs.shape[0]
    bm, tn, tn_down = MOE_BM, MOE_TN, MOE_DOWN_TN
    n_blk = P // bm
    d_exp = w_down.shape[1]
    up_off = d_exp // tn

    def blk(i, nu):
        return jnp.minimum(i, nu[0] - 1)

    gateup_spec = pltpu.PrefetchScalarGridSpec(
        num_scalar_prefetch=4,
        grid=(d_exp // tn, n_blk),
        in_specs=[
            pl.BlockSpec((bm, xs.shape[1]), lambda j, i, be, nu, bv, nx: (blk(i, nu), 0)),
            pl.BlockSpec((1, 1, tn), lambda j, i, be, nu, bv, nx: (be[blk(i, nu)], 0, j)),
            pl.BlockSpec((1, 1, tn), lambda j, i, be, nu, bv, nx: (be[blk(i, nu)], 0, up_off + j)),
            pl.BlockSpec(memory_space=pl.ANY),
        ],
        out_specs=pl.BlockSpec((bm, tn), lambda j, i, be, nu, bv, nx: (i, j)),
        scratch_shapes=[pltpu.VMEM((2, D_MODEL, tn), F32), pltpu.VMEM((D_MODEL, tn), BF16),
                        pltpu.VMEM((D_MODEL, tn), BF16), pltpu.SemaphoreType.DMA((2,))],
    )
    hmid = pl.pallas_call(
        _gateup_kernel,
        grid_spec=gateup_spec,
        out_shape=jax.ShapeDtypeStruct((P, d_exp), BF16),
        compiler_params=_cparams(("arbitrary", "arbitrary"), 56),
        name="expert_gate_up",
    )(blk_expert, n_used, blk_valid, next_expert, xs, b_gate_up, b_gate_up, w_gate_up)

    down_spec = pltpu.PrefetchScalarGridSpec(
        num_scalar_prefetch=4,
        grid=(D_MODEL // tn_down, n_blk),
        in_specs=[
            pl.BlockSpec((bm, d_exp), lambda j, i, be, nu, bv, nx: (blk(i, nu), 0)),
            pl.BlockSpec((1, 1, tn_down), lambda j, i, be, nu, bv, nx: (be[blk(i, nu)], 0, j)),
            pl.BlockSpec(memory_space=pl.ANY),
        ],
        out_specs=pl.BlockSpec((bm, tn_down), lambda j, i, be, nu, bv, nx: (i, j)),
        scratch_shapes=[pltpu.VMEM((d_exp, tn_down), F32), pltpu.VMEM((d_exp, tn_down), BF16),
                        pltpu.SemaphoreType.DMA(())],
    )
    return pl.pallas_call(
        _down_kernel,
        grid_spec=down_spec,
        out_shape=jax.ShapeDtypeStruct((P, D_MODEL), F32),
        compiler_params=_cparams(("arbitrary", "arbitrary"), 58),
        name="expert_down",
    )(blk_expert, n_used, blk_valid, next_expert, hmid, b_down, w_down)


def _combine_kernel(start_ref, e_ref, r_ref, h_ref, g_ref, ys_ref, o_ref, buf, sem):
    tb = COMBINE_TB

    def row_copy(t, k):
        a = t * TOP_K + k
        src_row = start_ref[e_ref[a]] + r_ref[a]
        return pltpu.make_async_copy(ys_ref.at[pl.ds(src_row, 1), :], buf.at[k, pl.ds(t, 1), :], sem)

    def issue(t, carry):
        for k in range(TOP_K):
            row_copy(t, k).start()
        return carry

    def drain(t, carry):
        for k in range(TOP_K):
            row_copy(t, k).wait()
        return carry

    lax.fori_loop(0, tb, issue, 0)
    lax.fori_loop(0, tb, drain, 0)
    g = g_ref[...]
    acc = h_ref[...]
    for k in range(TOP_K):
        acc = acc + g[:, k:k + 1] * buf[k]
    o_ref[...] = acc


def _combine(pad_start, e_flat, r_flat, h1, gates, ys):
    T = h1.shape[0]
    tb = COMBINE_TB
    grid_spec = pltpu.PrefetchScalarGridSpec(
        num_scalar_prefetch=1,
        grid=(T // tb,),
        in_specs=[
            pl.BlockSpec((tb * TOP_K,), lambda i, s: (i,), memory_space=pltpu.SMEM),
            pl.BlockSpec((tb * TOP_K,), lambda i, s: (i,), memory_space=pltpu.SMEM),
            pl.BlockSpec((tb, D_MODEL), lambda i, s: (i, 0)),
            pl.BlockSpec((tb, LANES), lambda i, s: (i, 0)),
            pl.BlockSpec(memory_space=pl.ANY),
        ],
        out_specs=pl.BlockSpec((tb, D_MODEL), lambda i, s: (i, 0)),
        scratch_shapes=[pltpu.VMEM((TOP_K, tb, D_MODEL), F32), pltpu.SemaphoreType.DMA(())],
    )
    return pl.pallas_call(
        _combine_kernel,
        grid_spec=grid_spec,
        out_shape=jax.ShapeDtypeStruct((T, D_MODEL), F32),
        compiler_params=_cparams(("arbitrary",), 32),
        name="combine",
    )(pad_start, e_flat, r_flat, h1, gates, ys)


def _pad_lanes(a, width=LANES):
    return jnp.pad(a, ((0, 0), (0, width - a.shape[1])))


def _layer(h, mem, positions, norm_mix, w_in, b_forget, fox_q_norm, fox_k_norm, swa_q_norm, swa_k_norm,
           swa_sinks, xmem_q_norm, xmem_k_norm, norm_mem, w_mem_kv, w_up_fox, w_up_swa, w_up_xmem, w_out,
           norm_ffn, w_router, b_router, w_gate_up, b_gate_up, w_down, b_down):
    B, S, D = h.shape
    M = mem.shape[1]
    T = B * S
    x2 = h.reshape(T, D)

    attn_end = 3 * FOX_W + FOX_HEADS
    gate_start = attn_end + SWA_QW + 2 * SWA_KVW + XMEM_W
    head_order = np.asarray(SWA_HEAD_ORDER)
    w_t = jnp.transpose(w_in)
    w_sq_t = w_t[attn_end:attn_end + SWA_QW].reshape(SWA_HEADS, HEAD_DIM, D)[head_order].reshape(SWA_QW, D)
    w_attn = _transpose_cast(jnp.concatenate([w_t[:3 * FOX_W], w_sq_t, w_t[attn_end + SWA_QW:gate_start]], axis=0))
    w_gates = _transpose_cast(w_t[gate_start:])
    w_up_swa = w_up_swa.reshape(SWA_HEADS, HEAD_DIM, D)[head_order].reshape(SWA_QW, D)
    swa_sinks = swa_sinks[head_order]
    wf = _pad_lanes(jnp.transpose(w_t[3 * FOX_W:attn_end]))
    wf_hi, wf_lo = _split2(wf)
    bf = _pad_lanes(b_forget.reshape(1, FOX_HEADS))

    def tile_gain(g, reps):
        return jnp.tile(g.reshape(1, -1), (1, reps))

    xn, c = _prenorm(x2, norm_mix.reshape(1, D), wf_hi, wf_lo, bf, S)
    proj = _matmul(xn, w_attn, F32, False, "in_proj_attn")
    gates = _matmul(xn, w_gates, BF16, True, "in_proj_gates")

    qa, ka, vb = _fox_prep(proj, c, tile_gain(fox_q_norm, FOX_HEADS), tile_gain(fox_k_norm, FOX_HEADS))
    o_fox = _fox_attn(qa, ka, vb, B, S)

    o_swa = _swa_attn(proj, positions.reshape(T, 1), swa_sinks.astype(F32),
                      tile_gain(swa_q_norm, SWA_HEADS), tile_gain(swa_k_norm, SWA_KV_HEADS), B, S)

    mk, mv = _mem_kv(mem.reshape(B * M, D), norm_mem.reshape(1, D), w_mem_kv.astype(BF16),
                     xmem_k_norm.reshape(1, XMEM_HEAD_DIM))
    o_x = _mem_attn(proj, mk, mv, xmem_q_norm.reshape(1, XMEM_HEAD_DIM), S, M)

    wr = _pad_lanes(w_router)
    wr_hi, wr_lo = _split2(wr)
    h1, hn, route_i, route_g, counts = _merge(
        o_fox, o_swa, o_x, gates, x2, w_up_fox.astype(BF16), w_up_swa.astype(BF16), w_up_xmem.astype(BF16),
        w_out.astype(BF16), norm_ffn.reshape(1, D), wr_hi, wr_lo, _pad_lanes(b_router.reshape(1, N_EXPERTS)))

    bm = MOE_BM
    cnt = counts[0, :N_EXPERTS].astype(jnp.int32)
    padded = ((cnt + bm - 1) // bm) * bm
    pad_end = jnp.cumsum(padded)
    pad_start = (pad_end - padded).astype(jnp.int32)
    P = T * TOP_K + N_EXPERTS * bm
    n_blk = P // bm
    blk_first = jnp.arange(n_blk, dtype=jnp.int32) * bm
    blk_expert = jnp.minimum(jnp.sum((pad_end[None, :] <= blk_first[:, None]).astype(jnp.int32), axis=1),
                             N_EXPERTS - 1)
    n_used = (pad_end[-1:] // bm).astype(jnp.int32)
    is_blk_expert = blk_expert[:, None] == jnp.arange(N_EXPERTS, dtype=jnp.int32)[None, :]
    tokens_end = jnp.sum(jnp.where(is_blk_expert, (pad_start + cnt)[None, :], 0), axis=1)
    blk_valid = jnp.where(blk_first < pad_end[-1], jnp.clip(tokens_end - blk_first, 0, bm), 0).astype(jnp.int32)
    ids = jnp.arange(N_EXPERTS, dtype=jnp.int32)
    later = jnp.min(jnp.where((cnt > 0)[None, :] & (ids[None, :] > ids[:, None]), ids[None, :], N_EXPERTS), axis=1)
    first = jnp.min(jnp.where(cnt > 0, ids, N_EXPERTS))
    next_expert = jnp.where(later < N_EXPERTS, later, first).astype(jnp.int32)
    e_flat = route_i[:, :TOP_K].reshape(T * TOP_K)
    r_flat = route_i[:, TOP_K:2 * TOP_K].reshape(T * TOP_K)

    xs = _dispatch(pad_start, cnt, e_flat, r_flat, hn, P)
    ys = _experts(blk_expert, n_used, blk_valid, next_expert, xs, w_gate_up, b_gate_up.reshape(N_EXPERTS, 1, -1),
                  w_down, b_down.reshape(N_EXPERTS, 1, -1))
    out = _combine(pad_start, e_flat, r_flat, h1, route_g, ys)
    return out.reshape(B, S, D)


def kernel(x, mem, positions, norm_mix, w_in, b_forget, fox_q_norm, fox_k_norm, swa_q_norm, swa_k_norm, swa_sinks, xmem_q_norm, xmem_k_norm, norm_mem, w_mem_kv, w_up_fox, w_up_swa, w_up_xmem, w_out, norm_ffn, w_router, b_router, w_gate_up, b_gate_up, w_down, b_down):
    h = x
    for layer in range(norm_mix.shape[0]):
        h = _layer(
            h, mem, positions, norm_mix[layer], w_in[layer], b_forget[layer],
            fox_q_norm[layer], fox_k_norm[layer], swa_q_norm[layer], swa_k_norm[layer],
            swa_sinks[layer], xmem_q_norm[layer], xmem_k_norm[layer], norm_mem[layer],
            w_mem_kv[layer], w_up_fox[layer], w_up_swa[layer], w_up_xmem[layer],
            w_out[layer], norm_ffn[layer], w_router[layer], b_router[layer],
            w_gate_up[layer], b_gate_up[layer], w_down[layer], b_down[layer])
    return h
```

```python
import functools

import numpy as np
import jax
import jax.numpy as jnp
from jax import lax
from jax.experimental import pallas as pl
from jax.experimental.pallas import tpu as pltpu

F32 = jnp.float32
BF16 = jnp.bfloat16

D_MODEL = 2048
HEAD_DIM = 64
FOX_HEADS = 12
SWA_HEADS = 12
SWA_KV_HEADS = 4
SWA_GROUP = SWA_HEADS // SWA_KV_HEADS
WINDOW = 128
XMEM_HEADS = 4
XMEM_HEAD_DIM = 128
N_EXPERTS = 32
TOP_K = 4
SWIGLU_LIMIT = 7.0
SWIGLU_ALPHA = 1.702
ROPE_THETA = 10000.0
NORM_EPS = 1e-6

FOX_W = FOX_HEADS * HEAD_DIM
SWA_QW = SWA_HEADS * HEAD_DIM
SWA_KVW = SWA_KV_HEADS * HEAD_DIM
XMEM_W = XMEM_HEADS * XMEM_HEAD_DIM
FOX_PAIRS = FOX_HEADS // 2

LANES = 128
MXU_TILE = 256
NEG_BIG = -1e30
LOG2E = 1.4426950408889634
MIB = 1024 * 1024

PRE_TM = 512
MM_TM = 1024
MM_TN = 1024
PREP_TM = 512
FOX_TQ = 1024
FOX_TK = 512
MEM_TQ = 512
MERGE_TM = 256
MOE_BM = 512
MOE_SUB = 128
MOE_TN = 1024
MOE_DOWN_TN = 2048
WEIGHT_CAST_ROWS = 256
DISPATCH_TB = 256
COMBINE_TB = 128
ROW_DMA_THREADS = 2


def _cparams(semantics, vmem_mib):
    return pltpu.CompilerParams(dimension_semantics=semantics, vmem_limit_bytes=vmem_mib * MIB)


def _dot(a, b):
    return jnp.dot(a, b, preferred_element_type=F32)


def _dot_nt(a, b):
    return lax.dot_general(a, b, (((1,), (1,)), ((), ())), preferred_element_type=F32)


def _split2(x):
    hi = x.astype(BF16)
    lo = (x - hi.astype(F32)).astype(BF16)
    return hi, lo


def _split3(x):
    hi = x.astype(BF16)
    r = x - hi.astype(F32)
    mid = r.astype(BF16)
    lo = (r - mid.astype(F32)).astype(BF16)
    return hi, mid, lo


def _prenorm_kernel(x_ref, g_ref, wf_hi_ref, wf_lo_ref, bf_ref, xn_ref, c_ref, carry_ref, *, blocks_per_seq):
    i = pl.program_id(0)

    @pl.when(i % blocks_per_seq == 0)
    def _():
        carry_ref[...] = jnp.zeros_like(carry_ref)

    x = x_ref[...]
    tm = x.shape[0]
    ms = jnp.mean(x * x, axis=-1, keepdims=True)
    xn = x * lax.rsqrt(ms + NORM_EPS) * g_ref[...]
    xn_hi, xn_lo = _split2(xn)
    xn_ref[...] = xn_hi
    wf_hi = wf_hi_ref[...]
    z = _dot(xn_hi, wf_hi) + _dot(xn_lo, wf_hi) + _dot(xn_hi, wf_lo_ref[...]) + bf_ref[...]
    logf = jnp.minimum(z, 0.0) - jnp.log1p(jnp.exp(-jnp.abs(z)))
    lane = lax.broadcasted_iota(jnp.int32, logf.shape, 1)
    logf = jnp.where(lane < FOX_HEADS, logf, 0.0)
    hi, mid, lo = _split3(logf)
    r = lax.broadcasted_iota(jnp.int32, (tm, tm), 0)
    cc = lax.broadcasted_iota(jnp.int32, (tm, tm), 1)
    tri = jnp.where(cc <= r, 1.0, 0.0).astype(BF16)
    c = _dot(tri, hi) + _dot(tri, mid) + _dot(tri, lo) + carry_ref[...]
    c_ref[...] = c
    carry_ref[...] = c[tm - 1:tm, :]


def _prenorm(x2, gain, wf_hi, wf_lo, bf, seq):
    T = x2.shape[0]
    tm = PRE_TM
    return pl.pallas_call(
        functools.partial(_prenorm_kernel, blocks_per_seq=seq // tm),
        grid=(T // tm,),
        in_specs=[
            pl.BlockSpec((tm, D_MODEL), lambda i: (i, 0)),
            pl.BlockSpec((1, D_MODEL), lambda i: (0, 0)),
            pl.BlockSpec((D_MODEL, LANES), lambda i: (0, 0)),
            pl.BlockSpec((D_MODEL, LANES), lambda i: (0, 0)),
            pl.BlockSpec((1, LANES), lambda i: (0, 0)),
        ],
        out_specs=[
            pl.BlockSpec((tm, D_MODEL), lambda i: (i, 0)),
            pl.BlockSpec((tm, LANES), lambda i: (i, 0)),
        ],
        out_shape=[
            jax.ShapeDtypeStruct((T, D_MODEL), BF16),
            jax.ShapeDtypeStruct((T, LANES), F32),
        ],
        scratch_shapes=[pltpu.VMEM((1, LANES), F32)],
        compiler_params=_cparams(("arbitrary",), 40),
        name="prenorm",
    )(x2, gain, wf_hi, wf_lo, bf)


def _mm_kernel(x_ref, w_ref, o_ref, *, sigmoid):
    acc = _dot(x_ref[...], w_ref[...])
    if sigmoid:
        acc = jax.nn.sigmoid(acc)
    o_ref[...] = acc.astype(o_ref.dtype)


def _matmul(x, w, out_dtype, sigmoid, name):
    M, K = x.shape
    N = w.shape[1]
    tm, tn = MM_TM, MM_TN
    return pl.pallas_call(
        functools.partial(_mm_kernel, sigmoid=sigmoid),
        grid=(M // tm, N // tn),
        in_specs=[
            pl.BlockSpec((tm, K), lambda i, j: (i, 0)),
            pl.BlockSpec((K, tn), lambda i, j: (0, j)),
        ],
        out_specs=pl.BlockSpec((tm, tn), lambda i, j: (i, j)),
        out_shape=jax.ShapeDtypeStruct((M, N), out_dtype),
        compiler_params=_cparams(("parallel", "parallel"), 48),
        name=name,
    )(x, w)


def _transpose_cast_kernel(w_ref, o_ref):
    o_ref[...] = w_ref[...].T.astype(o_ref.dtype)


def _transpose_cast(w_t):
    N, K = w_t.shape
    tn = 512
    return pl.pallas_call(
        _transpose_cast_kernel,
        grid=(N // tn,),
        in_specs=[pl.BlockSpec((tn, K), lambda i: (i, 0))],
        out_specs=pl.BlockSpec((K, tn), lambda i: (0, i)),
        out_shape=jax.ShapeDtypeStruct((K, N), BF16),
        compiler_params=_cparams(("parallel",), 32),
        name="weight_transpose_cast",
    )(w_t)


def _foxprep_kernel(q_ref, k_ref, v_ref, c_ref, gq_ref, gk_ref, bd_ref, selq_ref, selk_ref,
                    onesq_ref, onesk_ref, qa_ref, ka_ref, vt_ref):
    bd = bd_ref[...]

    def headnorm(x, g):
        hi, lo = _split2(x * x)
        w = bd.shape[0]
        ssq = jnp.concatenate([_dot(hi[:, c:c + w], bd) + _dot(lo[:, c:c + w], bd) for c in range(0, x.shape[1], w)],
                              axis=1)
        return x * lax.rsqrt(ssq * (1.0 / HEAD_DIM) + NORM_EPS) * g

    qn = headnorm(q_ref[...], gq_ref[...]) * (HEAD_DIM ** -0.5 * LOG2E)
    kn = headnorm(k_ref[...], gk_ref[...])
    v = v_ref[...]
    hi, mid, lo = _split3(c_ref[...] * LOG2E)
    c3 = jnp.concatenate([hi, mid, lo], axis=1)
    lane = lax.broadcasted_iota(jnp.int32, (qn.shape[0], LANES), 1)
    ones_col = jnp.where(lane == HEAD_DIM, 1.0, 0.0)
    for p in range(FOX_PAIRS):
        qp = qn[:, p * LANES:(p + 1) * LANES]
        vp = v[:, p * LANES:(p + 1) * LANES]
        for j in range(2):
            h = 2 * p + j
            keep = (lane < HEAD_DIM) if j == 0 else (lane >= HEAD_DIM)
            qa_ref[h, :, 0:LANES] = jnp.where(keep, qp, 0.0).astype(BF16)
            qa_ref[h, :, LANES:2 * LANES] = (_dot(c3, selq_ref[h]) + onesq_ref[h]).astype(BF16)
            vh = vp if j == 0 else pltpu.roll(vp, HEAD_DIM, axis=1)
            vt_ref[h, 0] = jnp.where(lane < HEAD_DIM, vh, ones_col).T.astype(BF16)
        ka_ref[p, :, 0:LANES] = kn[:, p * LANES:(p + 1) * LANES].astype(BF16)
        ka_ref[p, :, LANES:2 * LANES] = (_dot(c3, selk_ref[p]) + onesk_ref[...]).astype(BF16)


def _fox_tables():
    selq = np.zeros((FOX_HEADS, 3 * LANES, LANES), np.float32)
    selk = np.zeros((FOX_PAIRS, 3 * LANES, LANES), np.float32)
    onesq = np.zeros((FOX_HEADS, 1, LANES), np.float32)
    onesk = np.zeros((1, LANES), np.float32)
    for h in range(FOX_HEADS):
        p, j = divmod(h, 2)
        for piece in range(3):
            selq[h, piece * LANES + h, 6 * j + piece] = 1.0
            selk[p, piece * LANES + h, 6 * j + 3 + piece] = -1.0
            onesq[h, 0, 6 * j + 3 + piece] = 1.0
            onesk[0, 6 * j + piece] = 1.0
    bd = np.kron(np.eye(MXU_TILE // HEAD_DIM, dtype=np.float32), np.ones((HEAD_DIM, HEAD_DIM), np.float32))
    return (jnp.asarray(bd, BF16), jnp.asarray(selq, BF16), jnp.asarray(selk, BF16),
            jnp.asarray(onesq, F32), jnp.asarray(onesk, F32))


def _fox_prep(proj, c, gq, gk):
    T = proj.shape[0]
    tm = PREP_TM
    bd, selq, selk, onesq, onesk = _fox_tables()
    const2 = lambda i: (0, 0)
    const3 = lambda i: (0, 0, 0)
    return pl.pallas_call(
        _foxprep_kernel,
        grid=(T // tm,),
        in_specs=[
            pl.BlockSpec((tm, FOX_W), lambda i: (i, 0)),
            pl.BlockSpec((tm, FOX_W), lambda i: (i, 1)),
            pl.BlockSpec((tm, FOX_W), lambda i: (i, 2)),
            pl.BlockSpec((tm, LANES), lambda i: (i, 0)),
            pl.BlockSpec((1, FOX_W), const2),
            pl.BlockSpec((1, FOX_W), const2),
            pl.BlockSpec((MXU_TILE, MXU_TILE), const2),
            pl.BlockSpec((FOX_HEADS, 3 * LANES, LANES), const3),
            pl.BlockSpec((FOX_PAIRS, 3 * LANES, LANES), const3),
            pl.BlockSpec((FOX_HEADS, 1, LANES), const3),
            pl.BlockSpec((1, LANES), const2),
        ],
        out_specs=[
            pl.BlockSpec((FOX_HEADS, tm, 2 * LANES), lambda i: (0, i, 0)),
            pl.BlockSpec((FOX_PAIRS, tm, 2 * LANES), lambda i: (0, i, 0)),
            pl.BlockSpec((FOX_HEADS, 1, LANES, tm), lambda i: (0, i, 0, 0)),
        ],
        out_shape=[
            jax.ShapeDtypeStruct((FOX_HEADS, T, 2 * LANES), BF16),
            jax.ShapeDtypeStruct((FOX_PAIRS, T, 2 * LANES), BF16),
            jax.ShapeDtypeStruct((FOX_HEADS, T // tm, LANES, tm), BF16),
        ],
        compiler_params=_cparams(("parallel",), 48),
        name="fox_prep",
    )(proj, proj, proj, c, gq, gk, bd, selq, selk, onesq, onesk)


def _fox_attn_kernel(q_ref, k_ref, vt_ref, o_ref, m_ref, acc_ref, sa_ref, sb_ref):
    qi = pl.program_id(2)
    tk = FOX_TK
    m_ref[...] = jnp.full(m_ref.shape, NEG_BIG, F32)
    acc_ref[...] = jnp.zeros(acc_ref.shape, F32)

    def scores(kb, s_ref, q_lo=0):
        off = pl.multiple_of(kb * tk, tk)
        k = k_ref[pl.ds(off, tk), :]
        for j in range(2):
            s_ref[j, :, q_lo:] = _dot_nt(k, q_ref[j, q_lo:, :])

    def softmax_pv(kb, s_ref, diag_offset, q_lo=0):
        for j in range(2):
            s = s_ref[j, :, q_lo:]
            if diag_offset is not None:
                key = lax.broadcasted_iota(jnp.int32, s.shape, 0) + diag_offset
                qry = lax.broadcasted_iota(jnp.int32, s.shape, 1) + q_lo
                s = jnp.where(key <= qry, s, NEG_BIG)
            m_old = m_ref[j, :, q_lo:]
            m_new = jnp.maximum(m_old, jnp.max(s, axis=0, keepdims=True))
            alpha = jnp.exp2(m_old - m_new)
            p = jnp.exp2(s - m_new)
            acc_ref[j, :, q_lo:] = alpha * acc_ref[j, :, q_lo:] + _dot(vt_ref[j, kb], p.astype(BF16))
            m_ref[j, :, q_lo:] = m_new

    scores(0, sa_ref)

    def two_blocks(i, carry):
        scores(2 * i + 1, sb_ref)
        softmax_pv(2 * i, sa_ref, None)
        scores(2 * i + 2, sa_ref)
        softmax_pv(2 * i + 1, sb_ref, None)
        return carry

    lax.fori_loop(0, qi, two_blocks, 0)
    scores(2 * qi + 1, sb_ref, q_lo=tk)
    softmax_pv(2 * qi, sa_ref, 0)
    softmax_pv(2 * qi + 1, sb_ref, tk, q_lo=tk)

    outs = [acc_ref[j, 0:HEAD_DIM, :] / acc_ref[j, HEAD_DIM:HEAD_DIM + 1, :] for j in range(2)]
    o_ref[...] = jnp.concatenate(outs, axis=0).T.astype(o_ref.dtype)


def _fox_attn(qa, ka, vt, batch, seq):
    T = qa.shape[1]
    tq, tk = FOX_TQ, FOX_TK
    assert tq == 2 * tk and tk == PREP_TM
    nq = seq // tq
    nk = seq // tk
    qa4 = qa.reshape(FOX_PAIRS, 2, T, 2 * LANES)
    vt5 = vt.reshape(FOX_PAIRS, 2, batch * nk, LANES, tk)
    return pl.pallas_call(
        _fox_attn_kernel,
        grid=(batch, FOX_PAIRS, nq),
        in_specs=[
            pl.BlockSpec((None, 2, tq, 2 * LANES), lambda b, p, q: (p, 0, b * nq + q, 0)),
            pl.BlockSpec((None, seq, 2 * LANES), lambda b, p, q: (p, b, 0)),
            pl.BlockSpec((None, 2, nk, LANES, tk), lambda b, p, q: (p, 0, b, 0, 0)),
        ],
        out_specs=pl.BlockSpec((tq, LANES), lambda b, p, q: (b * nq + q, p)),
        out_shape=jax.ShapeDtypeStruct((T, FOX_W), BF16),
        scratch_shapes=[
            pltpu.VMEM((2, 1, tq), F32),
            pltpu.VMEM((2, LANES, tq), F32),
            pltpu.VMEM((2, tk, tq), F32),
            pltpu.VMEM((2, tk, tq), F32),
        ],
        compiler_params=_cparams(("parallel", "parallel", "arbitrary"), 48),
        name="fox_attn",
    )(qa4, ka, vt5)


SWA_HEAD_ORDER = (0, 3, 1, 4, 2, 5, 6, 9, 7, 10, 8, 11)
SWA_KV_TILES = SWA_KV_HEADS // 2
SWA_Q_TILES_PER_KV_TILE = SWA_HEADS // 2 // SWA_KV_TILES


def _swa_kernel(sinks_ref, q_ref, k_ref, v_ref, pos_ref, freq_ref, sign_ref, gq_ref, gk_ref, bdq_ref, bdk_ref,
                o_ref, kprev_ref, vtprev_ref):
    n = pl.program_id(1)
    W = WINDOW
    half = HEAD_DIM // 2

    @pl.when(n == 0)
    def _():
        kprev_ref[...] = jnp.zeros_like(kprev_ref)
        vtprev_ref[...] = jnp.zeros_like(vtprev_ref)

    lane = lax.broadcasted_iota(jnp.int32, (W, LANES), 1)
    ang = pos_ref[...].astype(F32) * freq_ref[...]
    cos1 = jnp.cos(ang)
    sin1 = jnp.sin(ang) * sign_ref[...]
    first_half1 = (lane & half) == 0

    def norm_rope(x, g, bd):
        reps = x.shape[1] // LANES
        hi, lo = _split2(x * x)
        w = bd.shape[0]
        ssq = jnp.concatenate([_dot(hi[:, c:c + w], bd) + _dot(lo[:, c:c + w], bd) for c in range(0, x.shape[1], w)],
                              axis=1)
        xn = x * lax.rsqrt(ssq * (1.0 / HEAD_DIM) + NORM_EPS) * g
        first_half = jnp.tile(first_half1, (1, reps))
        swapped = jnp.where(first_half, pltpu.roll(xn, x.shape[1] - half, axis=1), pltpu.roll(xn, half, axis=1))
        return xn * jnp.tile(cos1, (1, reps)) + swapped * jnp.tile(sin1, (1, reps))

    q = norm_rope(q_ref[...], gq_ref[...], bdq_ref[...]) * (HEAD_DIM ** -0.5 * LOG2E)
    k = norm_rope(k_ref[...], gk_ref[...], bdk_ref[...]).astype(BF16)
    v = v_ref[...]

    key = lax.broadcasted_iota(jnp.int32, (2 * W, 2 * W), 0)
    qry = lax.broadcasted_iota(jnp.int32, (2 * W, 2 * W), 1) & (W - 1)
    visible = ((key < W) & (key > qry) & (n > 0)) | ((key >= W) & (key - W <= qry))
    low_q = lax.broadcasted_iota(jnp.int32, (1, 2 * W), 1) < W

    for kt in range(SWA_KV_TILES):
        k_cur = k[:, kt * LANES:(kt + 1) * LANES]
        vt_cur = v[:, kt * LANES:(kt + 1) * LANES].T.astype(BF16)
        keys = jnp.concatenate([kprev_ref[kt], k_cur], axis=0)
        vt = jnp.concatenate([vtprev_ref[kt], vt_cur], axis=1)
        for r in range(SWA_Q_TILES_PER_KV_TILE):
            t = kt * SWA_Q_TILES_PER_KV_TILE + r
            qt = q[:, t * LANES:(t + 1) * LANES]
            qs = jnp.concatenate([jnp.where(lane < HEAD_DIM, qt, 0.0), jnp.where(lane >= HEAD_DIM, qt, 0.0)],
                                 axis=0).astype(BF16)
            s = jnp.where(visible, _dot_nt(keys, qs), NEG_BIG)
            sink = jnp.where(low_q, sinks_ref[2 * t], sinks_ref[2 * t + 1]) * LOG2E
            m = jnp.maximum(jnp.max(s, axis=0, keepdims=True), sink)
            p = jnp.exp2(s - m)
            denom = jnp.sum(p, axis=0, keepdims=True) + jnp.exp2(sink - m)
            ot = _dot(vt, p.astype(BF16)) / denom
            pair = jnp.concatenate([ot[0:HEAD_DIM, 0:W], ot[HEAD_DIM:LANES, W:2 * W]], axis=0)
            o_ref[:, t * LANES:(t + 1) * LANES] = pair.T.astype(o_ref.dtype)
        kprev_ref[kt] = k_cur
        vtprev_ref[kt] = vt_cur


def _swa_attn(proj, pos2, sinks, gq, gk, batch, seq):
    T = proj.shape[0]
    W = WINDOW
    assert W == LANES
    nb = seq // W
    half = HEAD_DIM // 2
    inv_freq = np.power(ROPE_THETA, -np.arange(0, HEAD_DIM, 2, dtype=np.float32) / HEAD_DIM).astype(np.float32)
    freq = np.tile(inv_freq, LANES // half).reshape(1, LANES)
    sign = np.tile(np.concatenate([-np.ones(half, np.float32), np.ones(half, np.float32)]),
                   LANES // HEAD_DIM).reshape(1, LANES)
    head_ones = np.ones((HEAD_DIM, HEAD_DIM), np.float32)
    bdq = bdk = jnp.asarray(np.kron(np.eye(MXU_TILE // HEAD_DIM, dtype=np.float32), head_ones), BF16)
    q_col = (3 * FOX_W) // SWA_QW
    k_col = (3 * FOX_W + SWA_QW) // SWA_KVW
    const2 = lambda b, n, s: (0, 0)
    grid_spec = pltpu.PrefetchScalarGridSpec(
        num_scalar_prefetch=1,
        grid=(batch, nb),
        in_specs=[
            pl.BlockSpec((W, SWA_QW), lambda b, n, s: (b * nb + n, q_col)),
            pl.BlockSpec((W, SWA_KVW), lambda b, n, s: (b * nb + n, k_col)),
            pl.BlockSpec((W, SWA_KVW), lambda b, n, s: (b * nb + n, k_col + 1)),
            pl.BlockSpec((W, 1), lambda b, n, s: (b * nb + n, 0)),
            pl.BlockSpec((1, LANES), const2),
            pl.BlockSpec((1, LANES), const2),
            pl.BlockSpec((1, SWA_QW), const2),
            pl.BlockSpec((1, SWA_KVW), const2),
            pl.BlockSpec((MXU_TILE, MXU_TILE), const2),
            pl.BlockSpec((MXU_TILE, MXU_TILE), const2),
        ],
        out_specs=pl.BlockSpec((W, SWA_QW), lambda b, n, s: (b * nb + n, 0)),
        scratch_shapes=[
            pltpu.VMEM((SWA_KV_TILES, W, LANES), BF16),
            pltpu.VMEM((SWA_KV_TILES, LANES, W), BF16),
        ],
    )
    return pl.pallas_call(
        _swa_kernel,
        grid_spec=grid_spec,
        out_shape=jax.ShapeDtypeStruct((T, SWA_QW), BF16),
        compiler_params=_cparams(("parallel", "arbitrary"), 32),
        name="swa_attn",
    )(sinks, proj, proj, proj, pos2, jnp.asarray(freq), jnp.asarray(sign), gq, gk, bdq, bdk)


def _memkv_kernel(mem_ref, g_ref, w_ref, gk_ref, k_ref, v_ref):
    x = mem_ref[...]
    ms = jnp.mean(x * x, axis=-1, keepdims=True)
    mn = (x * lax.rsqrt(ms + NORM_EPS) * g_ref[...]).astype(BF16)
    kv = _dot(mn, w_ref[...])
    for h in range(XMEM_HEADS):
        kh = kv[:, h * LANES:(h + 1) * LANES]
        ms_h = jnp.mean(kh * kh, axis=-1, keepdims=True)
        k_ref[:, h * LANES:(h + 1) * LANES] = (kh * lax.rsqrt(ms_h + NORM_EPS) * gk_ref[...]).astype(BF16)
    v_ref[...] = kv[:, XMEM_W:].astype(BF16)


def _mem_kv(mem2, gain, w, gk):
    R = mem2.shape[0]
    tm = 256
    return pl.pallas_call(
        _memkv_kernel,
        grid=(R // tm,),
        in_specs=[
            pl.BlockSpec((tm, D_MODEL), lambda i: (i, 0)),
            pl.BlockSpec((1, D_MODEL), lambda i: (0, 0)),
            pl.BlockSpec((D_MODEL, 2 * XMEM_W), lambda i: (0, 0)),
            pl.BlockSpec((1, XMEM_HEAD_DIM), lambda i: (0, 0)),
        ],
        out_specs=[
            pl.BlockSpec((tm, XMEM_W), lambda i: (i, 0)),
            pl.BlockSpec((tm, XMEM_W), lambda i: (i, 0)),
        ],
        out_shape=[
            jax.ShapeDtypeStruct((R, XMEM_W), BF16),
            jax.ShapeDtypeStruct((R, XMEM_W), BF16),
        ],
        compiler_params=_cparams(("parallel",), 32),
        name="mem_kv",
    )(mem2, gain, w, gk)


def _memattn_kernel(q_ref, k_ref, v_ref, gq_ref, o_ref):
    q = q_ref[...]
    for h in range(XMEM_HEADS):
        sl = slice(h * LANES, (h + 1) * LANES)
        qh = q[:, sl]
        ms = jnp.mean(qh * qh, axis=-1, keepdims=True)
        qn = (qh * lax.rsqrt(ms + NORM_EPS) * gq_ref[...] * (XMEM_HEAD_DIM ** -0.5)).astype(BF16)
        s = _dot_nt(qn, k_ref[:, sl])
        m = jnp.max(s, axis=-1, keepdims=True)
        p = jnp.exp(s - m)
        l = jnp.sum(p, axis=-1, keepdims=True)
        o_ref[:, sl] = (_dot(p.astype(BF16), v_ref[:, sl]) / l).astype(o_ref.dtype)


def _mem_attn(proj, mk, mv, gq, seq, n_mem):
    T = proj.shape[0]
    tq = MEM_TQ
    per_seq = seq // tq
    q_col = (3 * FOX_W + SWA_QW + 2 * SWA_KVW) // XMEM_W
    return pl.pallas_call(
        _memattn_kernel,
        grid=(T // tq,),
        in_specs=[
            pl.BlockSpec((tq, XMEM_W), lambda i: (i, q_col)),
            pl.BlockSpec((n_mem, XMEM_W), lambda i: (i // per_seq, 0)),
            pl.BlockSpec((n_mem, XMEM_W), lambda i: (i // per_seq, 0)),
            pl.BlockSpec((1, XMEM_HEAD_DIM), lambda i: (0, 0)),
        ],
        out_specs=pl.BlockSpec((tq, XMEM_W), lambda i: (i, 0)),
        out_shape=jax.ShapeDtypeStruct((T, XMEM_W), BF16),
        compiler_params=_cparams(("parallel",), 32),
        name="mem_attn",
    )(proj, mk, mv, gq)


def _merge_kernel(of_ref, os_ref, ox_ref, g0_ref, g1_ref, g2_ref, x_ref, wf_ref, ws_ref, wx_ref, wo_ref,
                  gn_ref, wr_hi_ref, wr_lo_ref, br_ref,
                  h_ref, hn_ref, ri_ref, rg_ref, cnt_ref, carry_ref):
    i = pl.program_id(0)

    @pl.when(i == 0)
    def _():
        carry_ref[...] = jnp.zeros_like(carry_ref)

    merged = (g0_ref[...].astype(F32) * _dot(of_ref[...], wf_ref[...])
              + g1_ref[...].astype(F32) * _dot(os_ref[...], ws_ref[...])
              + g2_ref[...].astype(F32) * _dot(ox_ref[...], wx_ref[...]))
    h = x_ref[...] + _dot(merged.astype(BF16), wo_ref[...])
    h_ref[...] = h
    ms = jnp.mean(h * h, axis=-1, keepdims=True)
    hn = h * lax.rsqrt(ms + NORM_EPS) * gn_ref[...]
    hn_hi, hn_lo = _split2(hn)
    hn_ref[...] = hn
    wr_hi = wr_hi_ref[...]
    logits = _dot(hn_hi, wr_hi) + _dot(hn_lo, wr_hi) + _dot(hn_hi, wr_lo_ref[...]) + br_ref[...]
    tm = logits.shape[0]
    lane = lax.broadcasted_iota(jnp.int32, (tm, LANES), 1).astype(F32)
    work = jnp.where(lane < N_EXPERTS, logits, NEG_BIG)
    vals, idxs = [], []
    for _ in range(TOP_K):
        mx = jnp.max(work, axis=-1, keepdims=True)
        ix = jnp.min(jnp.where(work == mx, lane, float(LANES)), axis=-1, keepdims=True)
        vals.append(mx)
        idxs.append(ix)
        work = jnp.where(lane == ix, NEG_BIG, work)
    es = [jnp.exp(v - vals[0]) for v in vals]
    den = es[0] + es[1] + es[2] + es[3]
    onehot = jnp.zeros((tm, LANES), F32)
    for ix in idxs:
        onehot = onehot + jnp.where(lane == ix, 1.0, 0.0)
    r = lax.broadcasted_iota(jnp.int32, (tm, tm), 0)
    cc = lax.broadcasted_iota(jnp.int32, (tm, tm), 1)
    tri = jnp.where(cc < r, 1.0, 0.0).astype(BF16)
    before = _dot(tri, onehot.astype(BF16)) + carry_ref[...]
    ri = jnp.zeros((tm, LANES), jnp.int32)
    rg = jnp.zeros((tm, LANES), F32)
    for k in range(TOP_K):
        rank = jnp.sum(jnp.where(lane == idxs[k], before, 0.0), axis=-1, keepdims=True)
        ri = jnp.where(lane == k, idxs[k].astype(jnp.int32), ri)
        ri = jnp.where(lane == TOP_K + k, rank.astype(jnp.int32), ri)
        rg = jnp.where(lane == k, es[k] / den, rg)
    ri_ref[...] = ri
    rg_ref[...] = rg
    total = carry_ref[...] + jnp.sum(onehot, axis=0, keepdims=True)
    carry_ref[...] = total
    cnt_ref[...] = total


def _merge(o_fox, o_swa, o_x, gates, x2, wf, ws, wx, wo, gn, wr_hi, wr_lo, br):
    T = x2.shape[0]
    tm = MERGE_TM
    row = lambda i: (i, 0)
    const = lambda i: (0, 0)
    resident = functools.partial(pl.BlockSpec, index_map=const, pipeline_mode=pl.Buffered(1))
    return pl.pallas_call(
        _merge_kernel,
        grid=(T // tm,),
        in_specs=[
            pl.BlockSpec((tm, FOX_W), row),
            pl.BlockSpec((tm, SWA_QW), row),
            pl.BlockSpec((tm, XMEM_W), row),
            pl.BlockSpec((tm, D_MODEL), lambda i: (i, 0)),
            pl.BlockSpec((tm, D_MODEL), lambda i: (i, 1)),
            pl.BlockSpec((tm, D_MODEL), lambda i: (i, 2)),
            pl.BlockSpec((tm, D_MODEL), row),
            resident((FOX_W, D_MODEL)),
            resident((SWA_QW, D_MODEL)),
            resident((XMEM_W, D_MODEL)),
            resident((D_MODEL, D_MODEL)),
            resident((1, D_MODEL)),
            resident((D_MODEL, LANES)),
            resident((D_MODEL, LANES)),
            resident((1, LANES)),
        ],
        out_specs=[
            pl.BlockSpec((tm, D_MODEL), row),
            pl.BlockSpec((tm, D_MODEL), row),
            pl.BlockSpec((tm, LANES), row),
            pl.BlockSpec((tm, LANES), row),
            pl.BlockSpec((1, LANES), const),
        ],
        out_shape=[
            jax.ShapeDtypeStruct((T, D_MODEL), F32),
            jax.ShapeDtypeStruct((T, D_MODEL), F32),
            jax.ShapeDtypeStruct((T, LANES), jnp.int32),
            jax.ShapeDtypeStruct((T, LANES), F32),
            jax.ShapeDtypeStruct((1, LANES), F32),
        ],
        scratch_shapes=[pltpu.VMEM((1, LANES), F32)],
        compiler_params=_cparams(("arbitrary",), 56),
        name="merge_router",
    )(o_fox, o_swa, o_x, gates, gates, gates, x2, wf, ws, wx, wo, gn, wr_hi, wr_lo, br)


def _dispatch_kernel(start_ref, cnt_ref, e_ref, r_ref, hn_ref, xs_ref, zero_ref, sem):
    i = pl.program_id(0)
    tb = DISPATCH_TB

    def row_copy(src_ref, t, dst_row):
        return pltpu.make_async_copy(src_ref.at[pl.ds(t, 1), :], xs_ref.at[pl.ds(dst_row, 1), :], sem)

    @pl.when(i == 0)
    def _():
        zero_ref[...] = jnp.zeros_like(zero_ref)

        def per_expert(e, carry):
            base = start_ref[e]
            n = cnt_ref[e]
            end = ((n + MOE_BM - 1) // MOE_BM) * MOE_BM

            def fill(r, c):
                row_copy(zero_ref, 0, base + r).start()
                return c

            def drain(r, c):
                row_copy(zero_ref, 0, base + r).wait()
                return c

            lax.fori_loop(n, end, fill, 0)
            lax.fori_loop(n, end, drain, 0)
            return carry

        lax.fori_loop(0, N_EXPERTS, per_expert, 0)

        last = N_EXPERTS - 1
        used = start_ref[last] + ((cnt_ref[last] + MOE_BM - 1) // MOE_BM) * MOE_BM
        zr = zero_ref.shape[0]

        def tail_copy(r):
            return pltpu.make_async_copy(zero_ref, xs_ref.at[pl.ds(pl.multiple_of(used + r * zr, zr), zr), :], sem)

        def tail_fill(r, c):
            tail_copy(r).start()
            return c

        def tail_drain(r, c):
            tail_copy(r).wait()
            return c

        n_tail = (xs_ref.shape[0] - used) // zr
        lax.fori_loop(0, n_tail, tail_fill, 0)
        lax.fori_loop(0, n_tail, tail_drain, 0)

    def token_copy(t, k):
        a = t * TOP_K + k
        return row_copy(hn_ref, t, start_ref[e_ref[a]] + r_ref[a])

    def issue(t, carry):
        for k in range(TOP_K):
            token_copy(t, k).start(priority=k % ROW_DMA_THREADS)
        return carry

    def drain(t, carry):
        for k in range(TOP_K):
            token_copy(t, k).wait()
        return carry

    lax.fori_loop(0, tb, issue, 0)
    lax.fori_loop(0, tb, drain, 0)


def _dispatch(pad_start, cnt, e_flat, r_flat, hn, n_slots):
    T, D = hn.shape
    tb = DISPATCH_TB
    grid_spec = pltpu.PrefetchScalarGridSpec(
        num_scalar_prefetch=2,
        grid=(T // tb,),
        in_specs=[
            pl.BlockSpec((tb * TOP_K,), lambda i, s, c: (i,), memory_space=pltpu.SMEM),
            pl.BlockSpec((tb * TOP_K,), lambda i, s, c: (i,), memory_space=pltpu.SMEM),
            pl.BlockSpec((tb, D), lambda i, s, c: (i, 0)),
        ],
        out_specs=pl.BlockSpec(memory_space=pl.ANY),
        scratch_shapes=[pltpu.VMEM((MOE_BM // 2, D), hn.dtype), pltpu.SemaphoreType.DMA(())],
    )
    return pl.pallas_call(
        _dispatch_kernel,
        grid_spec=grid_spec,
        out_shape=jax.ShapeDtypeStruct((n_slots, D), hn.dtype),
        compiler_params=_cparams(("arbitrary",), 32),
        name="dispatch",
    )(pad_start, cnt, e_flat, r_flat, hn)


def _expert_changed(be_ref, i):
    return (i == 0) | (be_ref[i] != be_ref[jnp.maximum(i - 1, 0)])


def _for_covering_rows(valid, bm, fn):
    for rows in range(MOE_SUB, bm + 1, MOE_SUB):
        @pl.when((valid > rows - MOE_SUB) & (valid <= rows))
        def _(rows=rows):
            fn(rows)


def _stream_expert_weights(be_ref, nx_ref, valid, tile_copies, consume):
    j = pl.program_id(0)
    i = pl.program_id(1)
    n_pass = pl.num_programs(0)

    @pl.when((i == 0) & (j == 0))
    def _():
        for cp in tile_copies(be_ref[0], 0):
            cp.start()

    @pl.when((valid > 0) & _expert_changed(be_ref, i))
    def _():
        e = be_ref[i]
        for cp in tile_copies(e, j):
            cp.wait()
        consume()
        e_next = nx_ref[e]
        j_next = j + (e_next <= e).astype(jnp.int32)

        @pl.when(j_next < n_pass)
        def _():
            for cp in tile_copies(e_next, j_next):
                cp.start()


def _gateup_kernel(be_ref, nu_ref, bv_ref, nx_ref, xs_ref, bg_ref, bu_ref, w_hbm, o_ref, wbuf, wg_s, wu_s, sem):
    i = pl.program_id(1)
    bm, tn = o_ref.shape
    up_off = w_hbm.shape[2] // 2 // tn
    valid = bv_ref[i]

    def tile_copies(e, j):
        def one(half):
            col = pl.multiple_of((half * up_off + j) * tn, tn)
            return pltpu.make_async_copy(w_hbm.at[e, :, pl.ds(col, tn)], wbuf.at[half], sem.at[half])
        return [one(0), one(1)]

    def consume():
        for r in range(0, wg_s.shape[0], WEIGHT_CAST_ROWS):
            rows = slice(r, r + WEIGHT_CAST_ROWS)
            wg_s[rows, :] = wbuf[0, rows, :].astype(BF16)
            wu_s[rows, :] = wbuf[1, rows, :].astype(BF16)

    _stream_expert_weights(be_ref, nx_ref, valid, tile_copies, consume)

    def compute(rows):
        x = xs_ref[0:rows, :].astype(BF16)
        gate = _dot(x, wg_s[...]) + bg_ref[0]
        up = _dot(x, wu_s[...]) + bu_ref[0]
        gate = jnp.minimum(gate, SWIGLU_LIMIT)
        up = jnp.clip(up, -SWIGLU_LIMIT, SWIGLU_LIMIT)
        glu = gate * jax.nn.sigmoid(gate * SWIGLU_ALPHA)
        o_ref[0:rows, :] = ((up + 1.0) * glu).astype(o_ref.dtype)
        if rows < bm:
            o_ref[rows:bm, :] = jnp.zeros((bm - rows, o_ref.shape[1]), o_ref.dtype)

    _for_covering_rows(valid, bm, compute)

    @pl.when(valid == 0)
    def _():
        o_ref[...] = jnp.zeros_like(o_ref)


def _down_kernel(be_ref, nu_ref, bv_ref, nx_ref, h_ref, b_ref, w_hbm, o_ref, wbuf, w_s, sem):
    i = pl.program_id(1)
    bm, tn = o_ref.shape
    valid = bv_ref[i]

    def tile_copies(e, j):
        col = pl.multiple_of(j * tn, tn)
        return [pltpu.make_async_copy(w_hbm.at[e, :, pl.ds(col, tn)], wbuf, sem)]

    def consume():
        for r in range(0, w_s.shape[0], WEIGHT_CAST_ROWS):
            rows = slice(r, r + WEIGHT_CAST_ROWS)
            w_s[rows, :] = wbuf[rows, :].astype(BF16)

    _stream_expert_weights(be_ref, nx_ref, valid, tile_copies, consume)

    def compute(rows):
        o_ref[0:rows, :] = _dot(h_ref[0:rows, :], w_s[...]) + b_ref[0]
        if rows < bm:
            o_ref[rows:bm, :] = jnp.zeros((bm - rows, o_ref.shape[1]), o_ref.dtype)

    _for_covering_rows(valid, bm, compute)

    @pl.when(valid == 0)
    def _():
        o_ref[...] = jnp.zeros_like(o_ref)


def _experts(blk_expert, n_used, blk_valid, next_expert, xs, w_gate_up, b_gate_up, w_down, b_down):
    P = xs.shape[0]
    bm, tn, tn_down = MOE_BM, MOE_TN, MOE_DOWN_TN
    n_blk = P // bm
    d_exp = w_down.shape[1]
    up_off = d_exp // tn

    def blk(i, nu):
        return jnp.minimum(i, nu[0] - 1)

    gateup_spec = pltpu.PrefetchScalarGridSpec(
        num_scalar_prefetch=4,
        grid=(d_exp // tn, n_blk),
        in_specs=[
            pl.BlockSpec((bm, xs.shape[1]), lambda j, i, be, nu, bv, nx: (blk(i, nu), 0)),
            pl.BlockSpec((1, 1, tn), lambda j, i, be, nu, bv, nx: (be[blk(i, nu)], 0, j)),
            pl.BlockSpec((1, 1, tn), lambda j, i, be, nu, bv, nx: (be[blk(i, nu)], 0, up_off + j)),
            pl.BlockSpec(memory_space=pl.ANY),
        ],
        out_specs=pl.BlockSpec((bm, tn), lambda j, i, be, nu, bv, nx: (i, j)),
        scratch_shapes=[pltpu.VMEM((2, D_MODEL, tn), F32), pltpu.VMEM((D_MODEL, tn), BF16),
                        pltpu.VMEM((D_MODEL, tn), BF16), pltpu.SemaphoreType.DMA((2,))],
    )
    hmid = pl.pallas_call(
        _gateup_kernel,
        grid_spec=gateup_spec,
        out_shape=jax.ShapeDtypeStruct((P, d_exp), BF16),
        compiler_params=_cparams(("arbitrary", "arbitrary"), 56),
        name="expert_gate_up",
    )(blk_expert, n_used, blk_valid, next_expert, xs, b_gate_up, b_gate_up, w_gate_up)

    down_spec = pltpu.PrefetchScalarGridSpec(
        num_scalar_prefetch=4,
        grid=(D_MODEL // tn_down, n_blk),
        in_specs=[
            pl.BlockSpec((bm, d_exp), lambda j, i, be, nu, bv, nx: (blk(i, nu), 0)),
            pl.BlockSpec((1, 1, tn_down), lambda j, i, be, nu, bv, nx: (be[blk(i, nu)], 0, j)),
            pl.BlockSpec(memory_space=pl.ANY),
        ],
        out_specs=pl.BlockSpec((bm, tn_down), lambda j, i, be, nu, bv, nx: (i, j)),
        scratch_shapes=[pltpu.VMEM((d_exp, tn_down), F32), pltpu.VMEM((d_exp, tn_down), BF16),
                        pltpu.SemaphoreType.DMA(())],
    )
    return pl.pallas_call(
        _down_kernel,
        grid_spec=down_spec,
        out_shape=jax.ShapeDtypeStruct((P, D_MODEL), F32),
        compiler_params=_cparams(("arbitrary", "arbitrary"), 58),
        name="expert_down",
    )(blk_expert, n_used, blk_valid, next_expert, hmid, b_down, w_down)


def _combine_kernel(start_ref, e_ref, r_ref, h_ref, g_ref, ys_ref, o_ref, buf, sem):
    tb = COMBINE_TB

    def row_copy(t, k):
        a = t * TOP_K + k
        src_row = start_ref[e_ref[a]] + r_ref[a]
        return pltpu.make_async_copy(ys_ref.at[pl.ds(src_row, 1), :], buf.at[k, pl.ds(t, 1), :], sem)

    def issue(t, carry):
        for k in range(TOP_K):
            row_copy(t, k).start(priority=k % ROW_DMA_THREADS)
        return carry

    def drain(t, carry):
        for k in range(TOP_K):
            row_copy(t, k).wait()
        return carry

    lax.fori_loop(0, tb, issue, 0)
    lax.fori_loop(0, tb, drain, 0)
    g = g_ref[...]
    acc = h_ref[...]
    for k in range(TOP_K):
        acc = acc + g[:, k:k + 1] * buf[k]
    o_ref[...] = acc


def _combine(pad_start, e_flat, r_flat, h1, gates, ys):
    T = h1.shape[0]
    tb = COMBINE_TB
    grid_spec = pltpu.PrefetchScalarGridSpec(
        num_scalar_prefetch=1,
        grid=(T // tb,),
        in_specs=[
            pl.BlockSpec((tb * TOP_K,), lambda i, s: (i,), memory_space=pltpu.SMEM),
            pl.BlockSpec((tb * TOP_K,), lambda i, s: (i,), memory_space=pltpu.SMEM),
            pl.BlockSpec((tb, D_MODEL), lambda i, s: (i, 0)),
            pl.BlockSpec((tb, LANES), lambda i, s: (i, 0)),
            pl.BlockSpec(memory_space=pl.ANY),
        ],
        out_specs=pl.BlockSpec((tb, D_MODEL), lambda i, s: (i, 0)),
        scratch_shapes=[pltpu.VMEM((TOP_K, tb, D_MODEL), F32), pltpu.SemaphoreType.DMA(())],
    )
    return pl.pallas_call(
        _combine_kernel,
        grid_spec=grid_spec,
        out_shape=jax.ShapeDtypeStruct((T, D_MODEL), F32),
        compiler_params=_cparams(("arbitrary",), 32),
        name="combine",
    )(pad_start, e_flat, r_flat, h1, gates, ys)


def _pad_lanes(a, width=LANES):
    return jnp.pad(a, ((0, 0), (0, width - a.shape[1])))


def _layer(h, mem, positions, norm_mix, w_in, b_forget, fox_q_norm, fox_k_norm, swa_q_norm, swa_k_norm,
           swa_sinks, xmem_q_norm, xmem_k_norm, norm_mem, w_mem_kv, w_up_fox, w_up_swa, w_up_xmem, w_out,
           norm_ffn, w_router, b_router, w_gate_up, b_gate_up, w_down, b_down):
    B, S, D = h.shape
    M = mem.shape[1]
    T = B * S
    x2 = h.reshape(T, D)

    attn_end = 3 * FOX_W + FOX_HEADS
    gate_start = attn_end + SWA_QW + 2 * SWA_KVW + XMEM_W
    head_order = np.asarray(SWA_HEAD_ORDER)
    w_t = jnp.transpose(w_in)
    w_sq_t = w_t[attn_end:attn_end + SWA_QW].reshape(SWA_HEADS, HEAD_DIM, D)[head_order].reshape(SWA_QW, D)
    w_attn = _transpose_cast(jnp.concatenate([w_t[:3 * FOX_W], w_sq_t, w_t[attn_end + SWA_QW:gate_start]], axis=0))
    w_gates = _transpose_cast(w_t[gate_start:])
    w_up_swa = w_up_swa.reshape(SWA_HEADS, HEAD_DIM, D)[head_order].reshape(SWA_QW, D)
    swa_sinks = swa_sinks[head_order]
    wf = _pad_lanes(jnp.transpose(w_t[3 * FOX_W:attn_end]))
    wf_hi, wf_lo = _split2(wf)
    bf = _pad_lanes(b_forget.reshape(1, FOX_HEADS))

    def tile_gain(g, reps):
        return jnp.tile(g.reshape(1, -1), (1, reps))

    xn, c = _prenorm(x2, norm_mix.reshape(1, D), wf_hi, wf_lo, bf, S)
    proj = _matmul(xn, w_attn, F32, False, "in_proj_attn")
    gates = _matmul(xn, w_gates, BF16, True, "in_proj_gates")

    qa, ka, vb = _fox_prep(proj, c, tile_gain(fox_q_norm, FOX_HEADS), tile_gain(fox_k_norm, FOX_HEADS))
    o_fox = _fox_attn(qa, ka, vb, B, S)

    o_swa = _swa_attn(proj, positions.reshape(T, 1), swa_sinks.astype(F32),
                      tile_gain(swa_q_norm, SWA_HEADS), tile_gain(swa_k_norm, SWA_KV_HEADS), B, S)

    mk, mv = _mem_kv(mem.reshape(B * M, D), norm_mem.reshape(1, D), w_mem_kv.astype(BF16),
                     xmem_k_norm.reshape(1, XMEM_HEAD_DIM))
    o_x = _mem_attn(proj, mk, mv, xmem_q_norm.reshape(1, XMEM_HEAD_DIM), S, M)

    wr = _pad_lanes(w_router)
    wr_hi, wr_lo = _split2(wr)
    h1, hn, route_i, route_g, counts = _merge(
        o_fox, o_swa, o_x, gates, x2, w_up_fox.astype(BF16), w_up_swa.astype(BF16), w_up_xmem.astype(BF16),
        w_out.astype(BF16), norm_ffn.reshape(1, D), wr_hi, wr_lo, _pad_lanes(b_router.reshape(1, N_EXPERTS)))

    bm = MOE_BM
    cnt = counts[0, :N_EXPERTS].astype(jnp.int32)
    padded = ((cnt + bm - 1) // bm) * bm
    pad_end = jnp.cumsum(padded)
    pad_start = (pad_end - padded).astype(jnp.int32)
    P = T * TOP_K + N_EXPERTS * bm
    n_blk = P // bm
    blk_first = jnp.arange(n_blk, dtype=jnp.int32) * bm
    blk_expert = jnp.minimum(jnp.sum((pad_end[None, :] <= blk_first[:, None]).astype(jnp.int32), axis=1),
                             N_EXPERTS - 1)
    n_used = (pad_end[-1:] // bm).astype(jnp.int32)
    is_blk_expert = blk_expert[:, None] == jnp.arange(N_EXPERTS, dtype=jnp.int32)[None, :]
    tokens_end = jnp.sum(jnp.where(is_blk_expert, (pad_start + cnt)[None, :], 0), axis=1)
    blk_valid = jnp.where(blk_first < pad_end[-1], jnp.clip(tokens_end - blk_first, 0, bm), 0).astype(jnp.int32)
    ids = jnp.arange(N_EXPERTS, dtype=jnp.int32)
    later = jnp.min(jnp.where((cnt > 0)[None, :] & (ids[None, :] > ids[:, None]), ids[None, :], N_EXPERTS), axis=1)
    first = jnp.min(jnp.where(cnt > 0, ids, N_EXPERTS))
    next_expert = jnp.where(later < N_EXPERTS, later, first).astype(jnp.int32)
    e_flat = route_i[:, :TOP_K].reshape(T * TOP_K)
    r_flat = route_i[:, TOP_K:2 * TOP_K].reshape(T * TOP_K)

    xs = _dispatch(pad_start, cnt, e_flat, r_flat, hn, P)
    ys = _experts(blk_expert, n_used, blk_valid, next_expert, xs, w_gate_up, b_gate_up.reshape(N_EXPERTS, 1, -1),
                  w_down, b_down.reshape(N_EXPERTS, 1, -1))
    out = _combine(pad_start, e_flat, r_flat, h1, route_g, ys)
    return out.reshape(B, S, D)


def kernel(x, mem, positions, norm_mix, w_in, b_forget, fox_q_norm, fox_k_norm, swa_q_norm, swa_k_norm, swa_sinks, xmem_q_norm, xmem_k_norm, norm_mem, w_mem_kv, w_up_fox, w_up_swa, w_up_xmem, w_out, norm_ffn, w_router, b_router, w_gate_up, b_gate_up, w_down, b_down):
    h = x
    for layer in range(norm_mix.shape[0]):
        h = _layer(
            h, mem, positions, norm_mix[layer], w_in[layer], b_forget[layer],
            fox_q_norm[layer], fox_k_norm[layer], swa_q_norm[layer], swa_k_norm[layer],
            swa_sinks[layer], xmem_q_norm[layer], xmem_k_norm[layer], norm_mem[layer],
            w_mem_kv[layer], w_up_fox[layer], w_up_swa[layer], w_up_xmem[layer],
            w_out[layer], norm_ffn[layer], w_router[layer], b_router[layer],
            w_gate_up[layer], b_gate_up[layer], w_down[layer], b_down[layer])
    return h
```

```python
import functools

import numpy as np
import jax
import jax.numpy as jnp
from jax import lax
from jax.experimental import pallas as pl
from jax.experimental.pallas import tpu as pltpu

F32 = jnp.float32
BF16 = jnp.bfloat16

D_MODEL = 2048
HEAD_DIM = 64
FOX_HEADS = 12
SWA_HEADS = 12
SWA_KV_HEADS = 4
SWA_GROUP = SWA_HEADS // SWA_KV_HEADS
WINDOW = 128
XMEM_HEADS = 4
XMEM_HEAD_DIM = 128
N_EXPERTS = 32
TOP_K = 4
SWIGLU_LIMIT = 7.0
SWIGLU_ALPHA = 1.702
ROPE_THETA = 10000.0
NORM_EPS = 1e-6

FOX_W = FOX_HEADS * HEAD_DIM
SWA_QW = SWA_HEADS * HEAD_DIM
SWA_KVW = SWA_KV_HEADS * HEAD_DIM
XMEM_W = XMEM_HEADS * XMEM_HEAD_DIM
FOX_PAIRS = FOX_HEADS // 2

LANES = 128
MXU_TILE = 256
NEG_BIG = -1e30
LOG2E = 1.4426950408889634
MIB = 1024 * 1024

PRE_TM = 512
MM_TM = 1024
MM_TN = 1024
PREP_TM = 512
FOX_TQ = 1024
FOX_TK = 512
MEM_TQ = 512
MERGE_TM = 256
ROUTE_TM = 1024
MOE_BM = 512
MOE_SUB = 128
MOE_TN = 1024
MOE_DOWN_TN = 2048
WEIGHT_CAST_ROWS = 256
DISPATCH_TB = 256
COMBINE_TB = 128


def _cparams(semantics, vmem_mib):
    return pltpu.CompilerParams(dimension_semantics=semantics, vmem_limit_bytes=vmem_mib * MIB)


def _dot(a, b):
    return jnp.dot(a, b, preferred_element_type=F32)


def _dot_nt(a, b):
    return lax.dot_general(a, b, (((1,), (1,)), ((), ())), preferred_element_type=F32)


def _split2(x):
    hi = x.astype(BF16)
    lo = (x - hi.astype(F32)).astype(BF16)
    return hi, lo


def _split3(x):
    hi = x.astype(BF16)
    r = x - hi.astype(F32)
    mid = r.astype(BF16)
    lo = (r - mid.astype(F32)).astype(BF16)
    return hi, mid, lo


def _prenorm_kernel(x_ref, g_ref, wf_hi_ref, wf_lo_ref, bf_ref, xn_ref, c_ref, carry_ref, *, blocks_per_seq):
    i = pl.program_id(0)

    @pl.when(i % blocks_per_seq == 0)
    def _():
        carry_ref[...] = jnp.zeros_like(carry_ref)

    x = x_ref[...]
    tm = x.shape[0]
    ms = jnp.mean(x * x, axis=-1, keepdims=True)
    xn = x * lax.rsqrt(ms + NORM_EPS) * g_ref[...]
    xn_hi, xn_lo = _split2(xn)
    xn_ref[...] = xn_hi
    wf_hi = wf_hi_ref[...]
    z = _dot(xn_hi, wf_hi) + _dot(xn_lo, wf_hi) + _dot(xn_hi, wf_lo_ref[...]) + bf_ref[...]
    logf = jnp.minimum(z, 0.0) - jnp.log1p(jnp.exp(-jnp.abs(z)))
    lane = lax.broadcasted_iota(jnp.int32, logf.shape, 1)
    logf = jnp.where(lane < FOX_HEADS, logf, 0.0)
    hi, mid, lo = _split3(logf)
    r = lax.broadcasted_iota(jnp.int32, (tm, tm), 0)
    cc = lax.broadcasted_iota(jnp.int32, (tm, tm), 1)
    tri = jnp.where(cc <= r, 1.0, 0.0).astype(BF16)
    c = _dot(tri, hi) + _dot(tri, mid) + _dot(tri, lo) + carry_ref[...]
    c_ref[...] = c
    carry_ref[...] = c[tm - 1:tm, :]


def _prenorm(x2, gain, wf_hi, wf_lo, bf, seq):
    T = x2.shape[0]
    tm = PRE_TM
    return pl.pallas_call(
        functools.partial(_prenorm_kernel, blocks_per_seq=seq // tm),
        grid=(T // tm,),
        in_specs=[
            pl.BlockSpec((tm, D_MODEL), lambda i: (i, 0)),
            pl.BlockSpec((1, D_MODEL), lambda i: (0, 0)),
            pl.BlockSpec((D_MODEL, LANES), lambda i: (0, 0)),
            pl.BlockSpec((D_MODEL, LANES), lambda i: (0, 0)),
            pl.BlockSpec((1, LANES), lambda i: (0, 0)),
        ],
        out_specs=[
            pl.BlockSpec((tm, D_MODEL), lambda i: (i, 0)),
            pl.BlockSpec((tm, LANES), lambda i: (i, 0)),
        ],
        out_shape=[
            jax.ShapeDtypeStruct((T, D_MODEL), BF16),
            jax.ShapeDtypeStruct((T, LANES), F32),
        ],
        scratch_shapes=[pltpu.VMEM((1, LANES), F32)],
        compiler_params=_cparams(("arbitrary",), 40),
        name="prenorm",
    )(x2, gain, wf_hi, wf_lo, bf)


def _mm_kernel(x_ref, w_ref, o_ref, *, sigmoid):
    acc = _dot(x_ref[...], w_ref[...])
    if sigmoid:
        acc = jax.nn.sigmoid(acc)
    o_ref[...] = acc.astype(o_ref.dtype)


def _matmul(x, w, out_dtype, sigmoid, name):
    M, K = x.shape
    N = w.shape[1]
    tm, tn = MM_TM, MM_TN
    return pl.pallas_call(
        functools.partial(_mm_kernel, sigmoid=sigmoid),
        grid=(M // tm, N // tn),
        in_specs=[
            pl.BlockSpec((tm, K), lambda i, j: (i, 0)),
            pl.BlockSpec((K, tn), lambda i, j: (0, j)),
        ],
        out_specs=pl.BlockSpec((tm, tn), lambda i, j: (i, j)),
        out_shape=jax.ShapeDtypeStruct((M, N), out_dtype),
        compiler_params=_cparams(("parallel", "parallel"), 48),
        name=name,
    )(x, w)


def _transpose_cast_kernel(w_ref, o_ref):
    o_ref[...] = w_ref[...].T.astype(o_ref.dtype)


def _transpose_cast(w_t):
    N, K = w_t.shape
    tn = 512
    return pl.pallas_call(
        _transpose_cast_kernel,
        grid=(N // tn,),
        in_specs=[pl.BlockSpec((tn, K), lambda i: (i, 0))],
        out_specs=pl.BlockSpec((K, tn), lambda i: (0, i)),
        out_shape=jax.ShapeDtypeStruct((K, N), BF16),
        compiler_params=_cparams(("parallel",), 32),
        name="weight_transpose_cast",
    )(w_t)


def _foxprep_kernel(q_ref, k_ref, v_ref, c_ref, gq_ref, gk_ref, bd_ref, selq_ref, selk_ref,
                    onesq_ref, onesk_ref, qa_ref, ka_ref, vt_ref):
    bd = bd_ref[...]

    def headnorm(x, g):
        hi, lo = _split2(x * x)
        w = bd.shape[0]
        ssq = jnp.concatenate([_dot(hi[:, c:c + w], bd) + _dot(lo[:, c:c + w], bd) for c in range(0, x.shape[1], w)],
                              axis=1)
        return x * lax.rsqrt(ssq * (1.0 / HEAD_DIM) + NORM_EPS) * g

    qn = headnorm(q_ref[...], gq_ref[...]) * (HEAD_DIM ** -0.5 * LOG2E)
    kn = headnorm(k_ref[...], gk_ref[...])
    v = v_ref[...]
    hi, mid, lo = _split3(c_ref[...] * LOG2E)
    c3 = jnp.concatenate([hi, mid, lo], axis=1)
    lane = lax.broadcasted_iota(jnp.int32, (qn.shape[0], LANES), 1)
    ones_col = jnp.where(lane == HEAD_DIM, 1.0, 0.0)
    for p in range(FOX_PAIRS):
        qp = qn[:, p * LANES:(p + 1) * LANES]
        vp = v[:, p * LANES:(p + 1) * LANES]
        for j in range(2):
            h = 2 * p + j
            keep = (lane < HEAD_DIM) if j == 0 else (lane >= HEAD_DIM)
            qa_ref[h, :, 0:LANES] = jnp.where(keep, qp, 0.0).astype(BF16)
            qa_ref[h, :, LANES:2 * LANES] = (_dot(c3, selq_ref[h]) + onesq_ref[h]).astype(BF16)
            vh = vp if j == 0 else pltpu.roll(vp, HEAD_DIM, axis=1)
            vt_ref[h, 0] = jnp.where(lane < HEAD_DIM, vh, ones_col).T.astype(BF16)
        ka_ref[p, :, 0:LANES] = kn[:, p * LANES:(p + 1) * LANES].astype(BF16)
        ka_ref[p, :, LANES:2 * LANES] = (_dot(c3, selk_ref[p]) + onesk_ref[...]).astype(BF16)


def _fox_tables():
    selq = np.zeros((FOX_HEADS, 3 * LANES, LANES), np.float32)
    selk = np.zeros((FOX_PAIRS, 3 * LANES, LANES), np.float32)
    onesq = np.zeros((FOX_HEADS, 1, LANES), np.float32)
    onesk = np.zeros((1, LANES), np.float32)
    for h in range(FOX_HEADS):
        p, j = divmod(h, 2)
        for piece in range(3):
            selq[h, piece * LANES + h, 6 * j + piece] = 1.0
            selk[p, piece * LANES + h, 6 * j + 3 + piece] = -1.0
            onesq[h, 0, 6 * j + 3 + piece] = 1.0
            onesk[0, 6 * j + piece] = 1.0
    bd = np.kron(np.eye(MXU_TILE // HEAD_DIM, dtype=np.float32), np.ones((HEAD_DIM, HEAD_DIM), np.float32))
    return (jnp.asarray(bd, BF16), jnp.asarray(selq, BF16), jnp.asarray(selk, BF16),
            jnp.asarray(onesq, F32), jnp.asarray(onesk, F32))


def _fox_prep(proj, c, gq, gk):
    T = proj.shape[0]
    tm = PREP_TM
    bd, selq, selk, onesq, onesk = _fox_tables()
    const2 = lambda i: (0, 0)
    const3 = lambda i: (0, 0, 0)
    return pl.pallas_call(
        _foxprep_kernel,
        grid=(T // tm,),
        in_specs=[
            pl.BlockSpec((tm, FOX_W), lambda i: (i, 0)),
            pl.BlockSpec((tm, FOX_W), lambda i: (i, 1)),
            pl.BlockSpec((tm, FOX_W), lambda i: (i, 2)),
            pl.BlockSpec((tm, LANES), lambda i: (i, 0)),
            pl.BlockSpec((1, FOX_W), const2),
            pl.BlockSpec((1, FOX_W), const2),
            pl.BlockSpec((MXU_TILE, MXU_TILE), const2),
            pl.BlockSpec((FOX_HEADS, 3 * LANES, LANES), const3),
            pl.BlockSpec((FOX_PAIRS, 3 * LANES, LANES), const3),
            pl.BlockSpec((FOX_HEADS, 1, LANES), const3),
            pl.BlockSpec((1, LANES), const2),
        ],
        out_specs=[
            pl.BlockSpec((FOX_HEADS, tm, 2 * LANES), lambda i: (0, i, 0)),
            pl.BlockSpec((FOX_PAIRS, tm, 2 * LANES), lambda i: (0, i, 0)),
            pl.BlockSpec((FOX_HEADS, 1, LANES, tm), lambda i: (0, i, 0, 0)),
        ],
        out_shape=[
            jax.ShapeDtypeStruct((FOX_HEADS, T, 2 * LANES), BF16),
            jax.ShapeDtypeStruct((FOX_PAIRS, T, 2 * LANES), BF16),
            jax.ShapeDtypeStruct((FOX_HEADS, T // tm, LANES, tm), BF16),
        ],
        compiler_params=_cparams(("parallel",), 48),
        name="fox_prep",
    )(proj, proj, proj, c, gq, gk, bd, selq, selk, onesq, onesk)


def _fox_attn_kernel(q_ref, k_ref, vt_ref, o_ref, m_ref, acc_ref, sa_ref, sb_ref):
    qi = pl.program_id(2)
    tk = FOX_TK
    m_ref[...] = jnp.full(m_ref.shape, NEG_BIG, F32)
    acc_ref[...] = jnp.zeros(acc_ref.shape, F32)

    def scores(kb, s_ref, q_lo=0):
        off = pl.multiple_of(kb * tk, tk)
        k = k_ref[pl.ds(off, tk), :]
        for j in range(2):
            s_ref[j, :, q_lo:] = _dot_nt(k, q_ref[j, q_lo:, :])

    def softmax_pv(kb, s_ref, diag_offset, q_lo=0):
        for j in range(2):
            s = s_ref[j, :, q_lo:]
            if diag_offset is not None:
                key = lax.broadcasted_iota(jnp.int32, s.shape, 0) + diag_offset
                qry = lax.broadcasted_iota(jnp.int32, s.shape, 1) + q_lo
                s = jnp.where(key <= qry, s, NEG_BIG)
            m_old = m_ref[j, :, q_lo:]
            m_new = jnp.maximum(m_old, jnp.max(s, axis=0, keepdims=True))
            alpha = jnp.exp2(m_old - m_new)
            p = jnp.exp2(s - m_new)
            acc_ref[j, :, q_lo:] = alpha * acc_ref[j, :, q_lo:] + _dot(vt_ref[j, kb], p.astype(BF16))
            m_ref[j, :, q_lo:] = m_new

    scores(0, sa_ref)

    def two_blocks(i, carry):
        scores(2 * i + 1, sb_ref)
        softmax_pv(2 * i, sa_ref, None)
        scores(2 * i + 2, sa_ref)
        softmax_pv(2 * i + 1, sb_ref, None)
        return carry

    lax.fori_loop(0, qi, two_blocks, 0)
    scores(2 * qi + 1, sb_ref, q_lo=tk)
    softmax_pv(2 * qi, sa_ref, 0)
    softmax_pv(2 * qi + 1, sb_ref, tk, q_lo=tk)

    outs = [acc_ref[j, 0:HEAD_DIM, :] / acc_ref[j, HEAD_DIM:HEAD_DIM + 1, :] for j in range(2)]
    o_ref[...] = jnp.concatenate(outs, axis=0).T.astype(o_ref.dtype)


def _fox_attn(qa, ka, vt, batch, seq):
    T = qa.shape[1]
    tq, tk = FOX_TQ, FOX_TK
    assert tq == 2 * tk and tk == PREP_TM
    nq = seq // tq
    nk = seq // tk
    qa4 = qa.reshape(FOX_PAIRS, 2, T, 2 * LANES)
    vt5 = vt.reshape(FOX_PAIRS, 2, batch * nk, LANES, tk)
    return pl.pallas_call(
        _fox_attn_kernel,
        grid=(batch, FOX_PAIRS, nq),
        in_specs=[
            pl.BlockSpec((None, 2, tq, 2 * LANES), lambda b, p, q: (p, 0, b * nq + q, 0)),
            pl.BlockSpec((None, seq, 2 * LANES), lambda b, p, q: (p, b, 0)),
            pl.BlockSpec((None, 2, nk, LANES, tk), lambda b, p, q: (p, 0, b, 0, 0)),
        ],
        out_specs=pl.BlockSpec((tq, LANES), lambda b, p, q: (b * nq + q, p)),
        out_shape=jax.ShapeDtypeStruct((T, FOX_W), BF16),
        scratch_shapes=[
            pltpu.VMEM((2, 1, tq), F32),
            pltpu.VMEM((2, LANES, tq), F32),
            pltpu.VMEM((2, tk, tq), F32),
            pltpu.VMEM((2, tk, tq), F32),
        ],
        compiler_params=_cparams(("parallel", "parallel", "arbitrary"), 48),
        name="fox_attn",
    )(qa4, ka, vt5)


SWA_HEAD_ORDER = (0, 3, 1, 4, 2, 5, 6, 9, 7, 10, 8, 11)
SWA_KV_TILES = SWA_KV_HEADS // 2
SWA_Q_TILES_PER_KV_TILE = SWA_HEADS // 2 // SWA_KV_TILES


def _swa_kernel(sinks_ref, q_ref, k_ref, v_ref, pos_ref, freq_ref, sign_ref, gq_ref, gk_ref, bdq_ref, bdk_ref,
                o_ref, kprev_ref, vtprev_ref):
    n = pl.program_id(1)
    W = WINDOW
    half = HEAD_DIM // 2

    @pl.when(n == 0)
    def _():
        kprev_ref[...] = jnp.zeros_like(kprev_ref)
        vtprev_ref[...] = jnp.zeros_like(vtprev_ref)

    lane = lax.broadcasted_iota(jnp.int32, (W, LANES), 1)
    ang = pos_ref[...].astype(F32) * freq_ref[...]
    cos1 = jnp.cos(ang)
    sin1 = jnp.sin(ang) * sign_ref[...]
    first_half1 = (lane & half) == 0

    def norm_rope(x, g, bd):
        reps = x.shape[1] // LANES
        hi, lo = _split2(x * x)
        w = bd.shape[0]
        ssq = jnp.concatenate([_dot(hi[:, c:c + w], bd) + _dot(lo[:, c:c + w], bd) for c in range(0, x.shape[1], w)],
                              axis=1)
        xn = x * lax.rsqrt(ssq * (1.0 / HEAD_DIM) + NORM_EPS) * g
        first_half = jnp.tile(first_half1, (1, reps))
        swapped = jnp.where(first_half, pltpu.roll(xn, x.shape[1] - half, axis=1), pltpu.roll(xn, half, axis=1))
        return xn * jnp.tile(cos1, (1, reps)) + swapped * jnp.tile(sin1, (1, reps))

    q = norm_rope(q_ref[...], gq_ref[...], bdq_ref[...]) * (HEAD_DIM ** -0.5 * LOG2E)
    k = norm_rope(k_ref[...], gk_ref[...], bdk_ref[...]).astype(BF16)
    v = v_ref[...]

    key = lax.broadcasted_iota(jnp.int32, (2 * W, 2 * W), 0)
    qry = lax.broadcasted_iota(jnp.int32, (2 * W, 2 * W), 1) & (W - 1)
    visible = ((key < W) & (key > qry) & (n > 0)) | ((key >= W) & (key - W <= qry))
    low_q = lax.broadcasted_iota(jnp.int32, (1, 2 * W), 1) < W

    for kt in range(SWA_KV_TILES):
        k_cur = k[:, kt * LANES:(kt + 1) * LANES]
        vt_cur = v[:, kt * LANES:(kt + 1) * LANES].T.astype(BF16)
        keys = jnp.concatenate([kprev_ref[kt], k_cur], axis=0)
        vt = jnp.concatenate([vtprev_ref[kt], vt_cur], axis=1)
        for r in range(SWA_Q_TILES_PER_KV_TILE):
            t = kt * SWA_Q_TILES_PER_KV_TILE + r
            qt = q[:, t * LANES:(t + 1) * LANES]
            qs = jnp.concatenate([jnp.where(lane < HEAD_DIM, qt, 0.0), jnp.where(lane >= HEAD_DIM, qt, 0.0)],
                                 axis=0).astype(BF16)
            s = jnp.where(visible, _dot_nt(keys, qs), NEG_BIG)
            sink = jnp.where(low_q, sinks_ref[2 * t], sinks_ref[2 * t + 1]) * LOG2E
            m = jnp.maximum(jnp.max(s, axis=0, keepdims=True), sink)
            p = jnp.exp2(s - m)
            denom = jnp.sum(p, axis=0, keepdims=True) + jnp.exp2(sink - m)
            ot = _dot(vt, p.astype(BF16)) / denom
            pair = jnp.concatenate([ot[0:HEAD_DIM, 0:W], ot[HEAD_DIM:LANES, W:2 * W]], axis=0)
            o_ref[:, t * LANES:(t + 1) * LANES] = pair.T.astype(o_ref.dtype)
        kprev_ref[kt] = k_cur
        vtprev_ref[kt] = vt_cur


def _swa_attn(proj, pos2, sinks, gq, gk, batch, seq):
    T = proj.shape[0]
    W = WINDOW
    assert W == LANES
    nb = seq // W
    half = HEAD_DIM // 2
    inv_freq = np.power(ROPE_THETA, -np.arange(0, HEAD_DIM, 2, dtype=np.float32) / HEAD_DIM).astype(np.float32)
    freq = np.tile(inv_freq, LANES // half).reshape(1, LANES)
    sign = np.tile(np.concatenate([-np.ones(half, np.float32), np.ones(half, np.float32)]),
                   LANES // HEAD_DIM).reshape(1, LANES)
    head_ones = np.ones((HEAD_DIM, HEAD_DIM), np.float32)
    bdq = bdk = jnp.asarray(np.kron(np.eye(MXU_TILE // HEAD_DIM, dtype=np.float32), head_ones), BF16)
    q_col = (3 * FOX_W) // SWA_QW
    k_col = (3 * FOX_W + SWA_QW) // SWA_KVW
    const2 = lambda b, n, s: (0, 0)
    grid_spec = pltpu.PrefetchScalarGridSpec(
        num_scalar_prefetch=1,
        grid=(batch, nb),
        in_specs=[
            pl.BlockSpec((W, SWA_QW), lambda b, n, s: (b * nb + n, q_col)),
            pl.BlockSpec((W, SWA_KVW), lambda b, n, s: (b * nb + n, k_col)),
            pl.BlockSpec((W, SWA_KVW), lambda b, n, s: (b * nb + n, k_col + 1)),
            pl.BlockSpec((W, 1), lambda b, n, s: (b * nb + n, 0)),
            pl.BlockSpec((1, LANES), const2),
            pl.BlockSpec((1, LANES), const2),
            pl.BlockSpec((1, SWA_QW), const2),
            pl.BlockSpec((1, SWA_KVW), const2),
            pl.BlockSpec((MXU_TILE, MXU_TILE), const2),
            pl.BlockSpec((MXU_TILE, MXU_TILE), const2),
        ],
        out_specs=pl.BlockSpec((W, SWA_QW), lambda b, n, s: (b * nb + n, 0)),
        scratch_shapes=[
            pltpu.VMEM((SWA_KV_TILES, W, LANES), BF16),
            pltpu.VMEM((SWA_KV_TILES, LANES, W), BF16),
        ],
    )
    return pl.pallas_call(
        _swa_kernel,
        grid_spec=grid_spec,
        out_shape=jax.ShapeDtypeStruct((T, SWA_QW), BF16),
        compiler_params=_cparams(("parallel", "arbitrary"), 32),
        name="swa_attn",
    )(sinks, proj, proj, proj, pos2, jnp.asarray(freq), jnp.asarray(sign), gq, gk, bdq, bdk)


def _memkv_kernel(mem_ref, g_ref, w_ref, gk_ref, k_ref, v_ref):
    x = mem_ref[...]
    ms = jnp.mean(x * x, axis=-1, keepdims=True)
    mn = (x * lax.rsqrt(ms + NORM_EPS) * g_ref[...]).astype(BF16)
    kv = _dot(mn, w_ref[...])
    for h in range(XMEM_HEADS):
        kh = kv[:, h * LANES:(h + 1) * LANES]
        ms_h = jnp.mean(kh * kh, axis=-1, keepdims=True)
        k_ref[:, h * LANES:(h + 1) * LANES] = (kh * lax.rsqrt(ms_h + NORM_EPS) * gk_ref[...]).astype(BF16)
    v_ref[...] = kv[:, XMEM_W:].astype(BF16)


def _mem_kv(mem2, gain, w, gk):
    R = mem2.shape[0]
    tm = 256
    return pl.pallas_call(
        _memkv_kernel,
        grid=(R // tm,),
        in_specs=[
            pl.BlockSpec((tm, D_MODEL), lambda i: (i, 0)),
            pl.BlockSpec((1, D_MODEL), lambda i: (0, 0)),
            pl.BlockSpec((D_MODEL, 2 * XMEM_W), lambda i: (0, 0)),
            pl.BlockSpec((1, XMEM_HEAD_DIM), lambda i: (0, 0)),
        ],
        out_specs=[
            pl.BlockSpec((tm, XMEM_W), lambda i: (i, 0)),
            pl.BlockSpec((tm, XMEM_W), lambda i: (i, 0)),
        ],
        out_shape=[
            jax.ShapeDtypeStruct((R, XMEM_W), BF16),
            jax.ShapeDtypeStruct((R, XMEM_W), BF16),
        ],
        compiler_params=_cparams(("parallel",), 32),
        name="mem_kv",
    )(mem2, gain, w, gk)


def _memattn_kernel(q_ref, k_ref, v_ref, gq_ref, o_ref):
    q = q_ref[...]
    for h in range(XMEM_HEADS):
        sl = slice(h * LANES, (h + 1) * LANES)
        qh = q[:, sl]
        ms = jnp.mean(qh * qh, axis=-1, keepdims=True)
        qn = (qh * lax.rsqrt(ms + NORM_EPS) * gq_ref[...] * (XMEM_HEAD_DIM ** -0.5)).astype(BF16)
        s = _dot_nt(qn, k_ref[:, sl])
        m = jnp.max(s, axis=-1, keepdims=True)
        p = jnp.exp(s - m)
        l = jnp.sum(p, axis=-1, keepdims=True)
        o_ref[:, sl] = (_dot(p.astype(BF16), v_ref[:, sl]) / l).astype(o_ref.dtype)


def _mem_attn(proj, mk, mv, gq, seq, n_mem):
    T = proj.shape[0]
    tq = MEM_TQ
    per_seq = seq // tq
    q_col = (3 * FOX_W + SWA_QW + 2 * SWA_KVW) // XMEM_W
    return pl.pallas_call(
        _memattn_kernel,
        grid=(T // tq,),
        in_specs=[
            pl.BlockSpec((tq, XMEM_W), lambda i: (i, q_col)),
            pl.BlockSpec((n_mem, XMEM_W), lambda i: (i // per_seq, 0)),
            pl.BlockSpec((n_mem, XMEM_W), lambda i: (i // per_seq, 0)),
            pl.BlockSpec((1, XMEM_HEAD_DIM), lambda i: (0, 0)),
        ],
        out_specs=pl.BlockSpec((tq, XMEM_W), lambda i: (i, 0)),
        out_shape=jax.ShapeDtypeStruct((T, XMEM_W), BF16),
        compiler_params=_cparams(("parallel",), 32),
        name="mem_attn",
    )(proj, mk, mv, gq)


def _merge_kernel(of_ref, os_ref, ox_ref, g0_ref, g1_ref, g2_ref, x_ref, wf_ref, ws_ref, wx_ref, wo_ref,
                  gn_ref, wr_hi_ref, wr_lo_ref, br_ref,
                  h_ref, hn_ref, logits_ref):
    merged = (g0_ref[...].astype(F32) * _dot(of_ref[...], wf_ref[...])
              + g1_ref[...].astype(F32) * _dot(os_ref[...], ws_ref[...])
              + g2_ref[...].astype(F32) * _dot(ox_ref[...], wx_ref[...]))
    h = x_ref[...] + _dot(merged.astype(BF16), wo_ref[...])
    h_ref[...] = h
    ms = jnp.mean(h * h, axis=-1, keepdims=True)
    hn = h * lax.rsqrt(ms + NORM_EPS) * gn_ref[...]
    hn_hi, hn_lo = _split2(hn)
    hn_ref[...] = hn
    wr_hi = wr_hi_ref[...]
    logits_ref[...] = _dot(hn_hi, wr_hi) + _dot(hn_lo, wr_hi) + _dot(hn_hi, wr_lo_ref[...]) + br_ref[...]


def _route_kernel(logits_ref, ri_ref, rg_ref, cnt_ref, carry_ref):
    i = pl.program_id(0)

    @pl.when(i == 0)
    def _():
        carry_ref[...] = jnp.zeros_like(carry_ref)

    logits = logits_ref[...]
    tm = logits.shape[0]
    lane = lax.broadcasted_iota(jnp.int32, (tm, LANES), 1).astype(F32)
    work = jnp.where(lane < N_EXPERTS, logits, NEG_BIG)
    vals, idxs = [], []
    for _ in range(TOP_K):
        mx = jnp.max(work, axis=-1, keepdims=True)
        ix = jnp.min(jnp.where(work == mx, lane, float(LANES)), axis=-1, keepdims=True)
        vals.append(mx)
        idxs.append(ix)
        work = jnp.where(lane == ix, NEG_BIG, work)
    es = [jnp.exp(v - vals[0]) for v in vals]
    den = es[0] + es[1] + es[2] + es[3]
    onehot = jnp.zeros((tm, LANES), F32)
    for ix in idxs:
        onehot = onehot + jnp.where(lane == ix, 1.0, 0.0)
    r = lax.broadcasted_iota(jnp.int32, (tm, tm), 0)
    cc = lax.broadcasted_iota(jnp.int32, (tm, tm), 1)
    tri = jnp.where(cc < r, 1.0, 0.0).astype(BF16)
    before = _dot(tri, onehot.astype(BF16)) + carry_ref[...]
    ri = jnp.zeros((tm, LANES), jnp.int32)
    rg = jnp.zeros((tm, LANES), F32)
    for k in range(TOP_K):
        rank = jnp.sum(jnp.where(lane == idxs[k], before, 0.0), axis=-1, keepdims=True)
        ri = jnp.where(lane == k, idxs[k].astype(jnp.int32), ri)
        ri = jnp.where(lane == TOP_K + k, rank.astype(jnp.int32), ri)
        rg = jnp.where(lane == k, es[k] / den, rg)
    ri_ref[...] = ri
    rg_ref[...] = rg
    total = carry_ref[...] + jnp.sum(onehot, axis=0, keepdims=True)
    carry_ref[...] = total
    cnt_ref[...] = total


def _merge(o_fox, o_swa, o_x, gates, x2, wf, ws, wx, wo, gn, wr_hi, wr_lo, br):
    T = x2.shape[0]
    tm = MERGE_TM
    row = lambda i: (i, 0)
    const = lambda i: (0, 0)
    resident = functools.partial(pl.BlockSpec, index_map=const, pipeline_mode=pl.Buffered(1))
    h1, hn, logits = pl.pallas_call(
        _merge_kernel,
        grid=(T // tm,),
        in_specs=[
            pl.BlockSpec((tm, FOX_W), row),
            pl.BlockSpec((tm, SWA_QW), row),
            pl.BlockSpec((tm, XMEM_W), row),
            pl.BlockSpec((tm, D_MODEL), lambda i: (i, 0)),
            pl.BlockSpec((tm, D_MODEL), lambda i: (i, 1)),
            pl.BlockSpec((tm, D_MODEL), lambda i: (i, 2)),
            pl.BlockSpec((tm, D_MODEL), row),
            resident((FOX_W, D_MODEL)),
            resident((SWA_QW, D_MODEL)),
            resident((XMEM_W, D_MODEL)),
            resident((D_MODEL, D_MODEL)),
            resident((1, D_MODEL)),
            resident((D_MODEL, LANES)),
            resident((D_MODEL, LANES)),
            resident((1, LANES)),
        ],
        out_specs=[
            pl.BlockSpec((tm, D_MODEL), row),
            pl.BlockSpec((tm, D_MODEL), row),
            pl.BlockSpec((tm, LANES), row),
        ],
        out_shape=[
            jax.ShapeDtypeStruct((T, D_MODEL), F32),
            jax.ShapeDtypeStruct((T, D_MODEL), F32),
            jax.ShapeDtypeStruct((T, LANES), F32),
        ],
        compiler_params=_cparams(("parallel",), 56),
        name="merge_router",
    )(o_fox, o_swa, o_x, gates, gates, gates, x2, wf, ws, wx, wo, gn, wr_hi, wr_lo, br)
    tr = ROUTE_TM
    ri, rg, counts = pl.pallas_call(
        _route_kernel,
        grid=(T // tr,),
        in_specs=[pl.BlockSpec((tr, LANES), row)],
        out_specs=[
            pl.BlockSpec((tr, LANES), row),
            pl.BlockSpec((tr, LANES), row),
            pl.BlockSpec((1, LANES), const),
        ],
        out_shape=[
            jax.ShapeDtypeStruct((T, LANES), jnp.int32),
            jax.ShapeDtypeStruct((T, LANES), F32),
            jax.ShapeDtypeStruct((1, LANES), F32),
        ],
        scratch_shapes=[pltpu.VMEM((1, LANES), F32)],
        compiler_params=_cparams(("arbitrary",), 32),
        name="route_topk",
    )(logits)
    return h1, hn, ri, rg, counts


def _dispatch_kernel(start_ref, cnt_ref, e_ref, r_ref, hn_ref, xs_ref, zero_ref, sem):
    i = pl.program_id(0)
    tb = DISPATCH_TB

    def row_copy(src_ref, t, dst_row):
        return pltpu.make_async_copy(src_ref.at[pl.ds(t, 1), :], xs_ref.at[pl.ds(dst_row, 1), :], sem)

    @pl.when(i == 0)
    def _():
        zero_ref[...] = jnp.zeros_like(zero_ref)

        def per_expert(e, carry):
            base = start_ref[e]
            n = cnt_ref[e]
            end = ((n + MOE_BM - 1) // MOE_BM) * MOE_BM

            def fill(r, c):
                row_copy(zero_ref, 0, base + r).start()
                return c

            def drain(r, c):
                row_copy(zero_ref, 0, base + r).wait()
                return c

            lax.fori_loop(n, end, fill, 0)
            lax.fori_loop(n, end, drain, 0)
            return carry

        lax.fori_loop(0, N_EXPERTS, per_expert, 0)

        last = N_EXPERTS - 1
        used = start_ref[last] + ((cnt_ref[last] + MOE_BM - 1) // MOE_BM) * MOE_BM
        zr = zero_ref.shape[0]

        def tail_copy(r):
            return pltpu.make_async_copy(zero_ref, xs_ref.at[pl.ds(pl.multiple_of(used + r * zr, zr), zr), :], sem)

        def tail_fill(r, c):
            tail_copy(r).start()
            return c

        def tail_drain(r, c):
            tail_copy(r).wait()
            return c

        n_tail = (xs_ref.shape[0] - used) // zr
        lax.fori_loop(0, n_tail, tail_fill, 0)
        lax.fori_loop(0, n_tail, tail_drain, 0)

    def token_copy(t, k):
        a = t * TOP_K + k
        return row_copy(hn_ref, t, start_ref[e_ref[a]] + r_ref[a])

    def issue(t, carry):
        for k in range(TOP_K):
            token_copy(t, k).start()
        return carry

    def drain(t, carry):
        for k in range(TOP_K):
            token_copy(t, k).wait()
        return carry

    lax.fori_loop(0, tb, issue, 0)
    lax.fori_loop(0, tb, drain, 0)


def _dispatch(pad_start, cnt, e_flat, r_flat, hn, n_slots):
    T, D = hn.shape
    tb = DISPATCH_TB
    grid_spec = pltpu.PrefetchScalarGridSpec(
        num_scalar_prefetch=2,
        grid=(T // tb,),
        in_specs=[
            pl.BlockSpec((tb * TOP_K,), lambda i, s, c: (i,), memory_space=pltpu.SMEM),
            pl.BlockSpec((tb * TOP_K,), lambda i, s, c: (i,), memory_space=pltpu.SMEM),
            pl.BlockSpec((tb, D), lambda i, s, c: (i, 0)),
        ],
        out_specs=pl.BlockSpec(memory_space=pl.ANY),
        scratch_shapes=[pltpu.VMEM((MOE_BM // 2, D), hn.dtype), pltpu.SemaphoreType.DMA(())],
    )
    return pl.pallas_call(
        _dispatch_kernel,
        grid_spec=grid_spec,
        out_shape=jax.ShapeDtypeStruct((n_slots, D), hn.dtype),
        compiler_params=_cparams(("arbitrary",), 32),
        name="dispatch",
    )(pad_start, cnt, e_flat, r_flat, hn)


def _expert_changed(be_ref, i):
    return (i == 0) | (be_ref[i] != be_ref[jnp.maximum(i - 1, 0)])


def _for_covering_rows(valid, bm, fn):
    for rows in range(MOE_SUB, bm + 1, MOE_SUB):
        @pl.when((valid > rows - MOE_SUB) & (valid <= rows))
        def _(rows=rows):
            fn(rows)


def _stream_expert_weights(be_ref, nx_ref, valid, tile_copies, consume):
    j = pl.program_id(0)
    i = pl.program_id(1)
    n_pass = pl.num_programs(0)

    @pl.when((i == 0) & (j == 0))
    def _():
        for cp in tile_copies(be_ref[0], 0):
            cp.start()

    @pl.when((valid > 0) & _expert_changed(be_ref, i))
    def _():
        e = be_ref[i]
        for cp in tile_copies(e, j):
            cp.wait()
        consume()
        e_next = nx_ref[e]
        j_next = j + (e_next <= e).astype(jnp.int32)

        @pl.when(j_next < n_pass)
        def _():
            for cp in tile_copies(e_next, j_next):
                cp.start()


def _gateup_kernel(be_ref, nu_ref, bv_ref, nx_ref, xs_ref, bg_ref, bu_ref, w_hbm, o_ref, wbuf, wg_s, wu_s, sem):
    i = pl.program_id(1)
    bm, tn = o_ref.shape
    up_off = w_hbm.shape[2] // 2 // tn
    valid = bv_ref[i]

    def tile_copies(e, j):
        def one(half):
            col = pl.multiple_of((half * up_off + j) * tn, tn)
            return pltpu.make_async_copy(w_hbm.at[e, :, pl.ds(col, tn)], wbuf.at[half], sem.at[half])
        return [one(0), one(1)]

    def consume():
        for r in range(0, wg_s.shape[0], WEIGHT_CAST_ROWS):
            rows = slice(r, r + WEIGHT_CAST_ROWS)
            wg_s[rows, :] = wbuf[0, rows, :].astype(BF16)
            wu_s[rows, :] = wbuf[1, rows, :].astype(BF16)

    _stream_expert_weights(be_ref, nx_ref, valid, tile_copies, consume)

    def compute(rows):
        x = xs_ref[0:rows, :].astype(BF16)
        gate = _dot(x, wg_s[...]) + bg_ref[0]
        up = _dot(x, wu_s[...]) + bu_ref[0]
        gate = jnp.minimum(gate, SWIGLU_LIMIT)
        up = jnp.clip(up, -SWIGLU_LIMIT, SWIGLU_LIMIT)
        glu = gate * jax.nn.sigmoid(gate * SWIGLU_ALPHA)
        o_ref[0:rows, :] = ((up + 1.0) * glu).astype(o_ref.dtype)
        if rows < bm:
            o_ref[rows:bm, :] = jnp.zeros((bm - rows, o_ref.shape[1]), o_ref.dtype)

    _for_covering_rows(valid, bm, compute)

    @pl.when(valid == 0)
    def _():
        o_ref[...] = jnp.zeros_like(o_ref)


def _down_kernel(be_ref, nu_ref, bv_ref, nx_ref, h_ref, b_ref, w_hbm, o_ref, wbuf, w_s, sem):
    i = pl.program_id(1)
    bm, tn = o_ref.shape
    valid = bv_ref[i]

    def tile_copies(e, j):
        col = pl.multiple_of(j * tn, tn)
        return [pltpu.make_async_copy(w_hbm.at[e, :, pl.ds(col, tn)], wbuf, sem)]

    def consume():
        for r in range(0, w_s.shape[0], WEIGHT_CAST_ROWS):
            rows = slice(r, r + WEIGHT_CAST_ROWS)
            w_s[rows, :] = wbuf[rows, :].astype(BF16)

    _stream_expert_weights(be_ref, nx_ref, valid, tile_copies, consume)

    def compute(rows):
        o_ref[0:rows, :] = _dot(h_ref[0:rows, :], w_s[...]) + b_ref[0]
        if rows < bm:
            o_ref[rows:bm, :] = jnp.zeros((bm - rows, o_ref.shape[1]), o_ref.dtype)

    _for_covering_rows(valid, bm, compute)

    @pl.when(valid == 0)
    def _():
        o_ref[...] = jnp.zeros_like(o_ref)


def _experts(blk_expert, n_used, blk_valid, next_expert, xs, w_gate_up, b_gate_up, w_down, b_down):
    P = xs.shape[0]
    bm, tn, tn_down = MOE_BM, MOE_TN, MOE_DOWN_TN
    n_blk = P // bm
    d_exp = w_down.shape[1]
    up_off = d_exp // tn

    def blk(i, nu):
        return jnp.minimum(i, nu[0] - 1)

    gateup_spec = pltpu.PrefetchScalarGridSpec(
        num_scalar_prefetch=4,
        grid=(d_exp // tn, n_blk),
        in_specs=[
            pl.BlockSpec((bm, xs.shape[1]), lambda j, i, be, nu, bv, nx: (blk(i, nu), 0)),
            pl.BlockSpec((1, 1, tn), lambda j, i, be, nu, bv, nx: (be[blk(i, nu)], 0, j)),
            pl.BlockSpec((1, 1, tn), lambda j, i, be, nu, bv, nx: (be[blk(i, nu)], 0, up_off + j)),
            pl.BlockSpec(memory_space=pl.ANY),
        ],
        out_specs=pl.BlockSpec((bm, tn), lambda j, i, be, nu, bv, nx: (i, j)),
        scratch_shapes=[pltpu.VMEM((2, D_MODEL, tn), F32), pltpu.VMEM((D_MODEL, tn), BF16),
                        pltpu.VMEM((D_MODEL, tn), BF16), pltpu.SemaphoreType.DMA((2,))],
    )
    hmid = pl.pallas_call(
        _gateup_kernel,
        grid_spec=gateup_spec,
        out_shape=jax.ShapeDtypeStruct((P, d_exp), BF16),
        compiler_params=_cparams(("arbitrary", "arbitrary"), 56),
        name="expert_gate_up",
    )(blk_expert, n_used, blk_valid, next_expert, xs, b_gate_up, b_gate_up, w_gate_up)

    down_spec = pltpu.PrefetchScalarGridSpec(
        num_scalar_prefetch=4,
        grid=(D_MODEL // tn_down, n_blk),
        in_specs=[
            pl.BlockSpec((bm, d_exp), lambda j, i, be, nu, bv, nx: (blk(i, nu), 0)),
            pl.BlockSpec((1, 1, tn_down), lambda j, i, be, nu, bv, nx: (be[blk(i, nu)], 0, j)),
            pl.BlockSpec(memory_space=pl.ANY),
        ],
        out_specs=pl.BlockSpec((bm, tn_down), lambda j, i, be, nu, bv, nx: (i, j)),
        scratch_shapes=[pltpu.VMEM((d_exp, tn_down), F32), pltpu.VMEM((d_exp, tn_down), BF16),
                        pltpu.SemaphoreType.DMA(())],
    )
    return pl.pallas_call(
        _down_kernel,
        grid_spec=down_spec,
        out_shape=jax.ShapeDtypeStruct((P, D_MODEL), F32),
        compiler_params=_cparams(("arbitrary", "arbitrary"), 58),
        name="expert_down",
    )(blk_expert, n_used, blk_valid, next_expert, hmid, b_down, w_down)


def _combine_kernel(start_ref, e_ref, r_ref, h_ref, g_ref, ys_ref, o_ref, buf, sem):
    tb = COMBINE_TB

    def row_copy(t, k):
        a = t * TOP_K + k
        src_row = start_ref[e_ref[a]] + r_ref[a]
        return pltpu.make_async_copy(ys_ref.at[pl.ds(src_row, 1), :], buf.at[k, pl.ds(t, 1), :], sem)

    def issue(t, carry):
        for k in range(TOP_K):
            row_copy(t, k).start()
        return carry

    def drain(t, carry):
        for k in range(TOP_K):
            row_copy(t, k).wait()
        return carry

    lax.fori_loop(0, tb, issue, 0)
    lax.fori_loop(0, tb, drain, 0)
    g = g_ref[...]
    acc = h_ref[...]
    for k in range(TOP_K):
        acc = acc + g[:, k:k + 1] * buf[k]
    o_ref[...] = acc


def _combine(pad_start, e_flat, r_flat, h1, gates, ys):
    T = h1.shape[0]
    tb = COMBINE_TB
    grid_spec = pltpu.PrefetchScalarGridSpec(
        num_scalar_prefetch=1,
        grid=(T // tb,),
        in_specs=[
            pl.BlockSpec((tb * TOP_K,), lambda i, s: (i,), memory_space=pltpu.SMEM),
            pl.BlockSpec((tb * TOP_K,), lambda i, s: (i,), memory_space=pltpu.SMEM),
            pl.BlockSpec((tb, D_MODEL), lambda i, s: (i, 0)),
            pl.BlockSpec((tb, LANES), lambda i, s: (i, 0)),
            pl.BlockSpec(memory_space=pl.ANY),
        ],
        out_specs=pl.BlockSpec((tb, D_MODEL), lambda i, s: (i, 0)),
        scratch_shapes=[pltpu.VMEM((TOP_K, tb, D_MODEL), F32), pltpu.SemaphoreType.DMA(())],
    )
    return pl.pallas_call(
        _combine_kernel,
        grid_spec=grid_spec,
        out_shape=jax.ShapeDtypeStruct((T, D_MODEL), F32),
        compiler_params=_cparams(("arbitrary",), 32),
        name="combine",
    )(pad_start, e_flat, r_flat, h1, gates, ys)


def _pad_lanes(a, width=LANES):
    return jnp.pad(a, ((0, 0), (0, width - a.shape[1])))


def _layer(h, mem, positions, norm_mix, w_in, b_forget, fox_q_norm, fox_k_norm, swa_q_norm, swa_k_norm,
           swa_sinks, xmem_q_norm, xmem_k_norm, norm_mem, w_mem_kv, w_up_fox, w_up_swa, w_up_xmem, w_out,
           norm_ffn, w_router, b_router, w_gate_up, b_gate_up, w_down, b_down):
    B, S, D = h.shape
    M = mem.shape[1]
    T = B * S
    x2 = h.reshape(T, D)

    attn_end = 3 * FOX_W + FOX_HEADS
    gate_start = attn_end + SWA_QW + 2 * SWA_KVW + XMEM_W
    head_order = np.asarray(SWA_HEAD_ORDER)
    w_t = jnp.transpose(w_in)
    w_sq_t = w_t[attn_end:attn_end + SWA_QW].reshape(SWA_HEADS, HEAD_DIM, D)[head_order].reshape(SWA_QW, D)
    w_attn = _transpose_cast(jnp.concatenate([w_t[:3 * FOX_W], w_sq_t, w_t[attn_end + SWA_QW:gate_start]], axis=0))
    w_gates = _transpose_cast(w_t[gate_start:])
    w_up_swa = w_up_swa.reshape(SWA_HEADS, HEAD_DIM, D)[head_order].reshape(SWA_QW, D)
    swa_sinks = swa_sinks[head_order]
    wf = _pad_lanes(jnp.transpose(w_t[3 * FOX_W:attn_end]))
    wf_hi, wf_lo = _split2(wf)
    bf = _pad_lanes(b_forget.reshape(1, FOX_HEADS))

    def tile_gain(g, reps):
        return jnp.tile(g.reshape(1, -1), (1, reps))

    xn, c = _prenorm(x2, norm_mix.reshape(1, D), wf_hi, wf_lo, bf, S)
    proj = _matmul(xn, w_attn, F32, False, "in_proj_attn")
    gates = _matmul(xn, w_gates, BF16, True, "in_proj_gates")

    qa, ka, vb = _fox_prep(proj, c, tile_gain(fox_q_norm, FOX_HEADS), tile_gain(fox_k_norm, FOX_HEADS))
    o_fox = _fox_attn(qa, ka, vb, B, S)

    o_swa = _swa_attn(proj, positions.reshape(T, 1), swa_sinks.astype(F32),
                      tile_gain(swa_q_norm, SWA_HEADS), tile_gain(swa_k_norm, SWA_KV_HEADS), B, S)

    mk, mv = _mem_kv(mem.reshape(B * M, D), norm_mem.reshape(1, D), w_mem_kv.astype(BF16),
                     xmem_k_norm.reshape(1, XMEM_HEAD_DIM))
    o_x = _mem_attn(proj, mk, mv, xmem_q_norm.reshape(1, XMEM_HEAD_DIM), S, M)

    wr = _pad_lanes(w_router)
    wr_hi, wr_lo = _split2(wr)
    h1, hn, route_i, route_g, counts = _merge(
        o_fox, o_swa, o_x, gates, x2, w_up_fox.astype(BF16), w_up_swa.astype(BF16), w_up_xmem.astype(BF16),
        w_out.astype(BF16), norm_ffn.reshape(1, D), wr_hi, wr_lo, _pad_lanes(b_router.reshape(1, N_EXPERTS)))

    bm = MOE_BM
    cnt = counts[0, :N_EXPERTS].astype(jnp.int32)
    padded = ((cnt + bm - 1) // bm) * bm
    pad_end = jnp.cumsum(padded)
    pad_start = (pad_end - padded).astype(jnp.int32)
    P = T * TOP_K + N_EXPERTS * bm
    n_blk = P // bm
    blk_first = jnp.arange(n_blk, dtype=jnp.int32) * bm
    blk_expert = jnp.minimum(jnp.sum((pad_end[None, :] <= blk_first[:, None]).astype(jnp.int32), axis=1),
                             N_EXPERTS - 1)
    n_used = (pad_end[-1:] // bm).astype(jnp.int32)
    is_blk_expert = blk_expert[:, None] == jnp.arange(N_EXPERTS, dtype=jnp.int32)[None, :]
    tokens_end = jnp.sum(jnp.where(is_blk_expert, (pad_start + cnt)[None, :], 0), axis=1)
    blk_valid = jnp.where(blk_first < pad_end[-1], jnp.clip(tokens_end - blk_first, 0, bm), 0).astype(jnp.int32)
    ids = jnp.arange(N_EXPERTS, dtype=jnp.int32)
    later = jnp.min(jnp.where((cnt > 0)[None, :] & (ids[None, :] > ids[:, None]), ids[None, :], N_EXPERTS), axis=1)
    first = jnp.min(jnp.where(cnt > 0, ids, N_EXPERTS))
    next_expert = jnp.where(later < N_EXPERTS, later, first).astype(jnp.int32)
    e_flat = route_i[:, :TOP_K].reshape(T * TOP_K)
    r_flat = route_i[:, TOP_K:2 * TOP_K].reshape(T * TOP_K)

    xs = _dispatch(pad_start, cnt, e_flat, r_flat, hn, P)
    ys = _experts(blk_expert, n_used, blk_valid, next_expert, xs, w_gate_up, b_gate_up.reshape(N_EXPERTS, 1, -1),
                  w_down, b_down.reshape(N_EXPERTS, 1, -1))
    out = _combine(pad_start, e_flat, r_flat, h1, route_g, ys)
    return out.reshape(B, S, D)


def kernel(x, mem, positions, norm_mix, w_in, b_forget, fox_q_norm, fox_k_norm, swa_q_norm, swa_k_norm, swa_sinks, xmem_q_norm, xmem_k_norm, norm_mem, w_mem_kv, w_up_fox, w_up_swa, w_up_xmem, w_out, norm_ffn, w_router, b_router, w_gate_up, b_gate_up, w_down, b_down):
    h = x
    for layer in range(norm_mix.shape[0]):
        h = _layer(
            h, mem, positions, norm_mix[layer], w_in[layer], b_forget[layer],
            fox_q_norm[layer], fox_k_norm[layer], swa_q_norm[layer], swa_k_norm[layer],
            swa_sinks[layer], xmem_q_norm[layer], xmem_k_norm[layer], norm_mem[layer],
            w_mem_kv[layer], w_up_fox[layer], w_up_swa[layer], w_up_xmem[layer],
            w_out[layer], norm_ffn[layer], w_router[layer], b_router[layer],
            w_gate_up[layer], b_gate_up[layer], w_down[layer], b_down[layer])
    return h
```
